```python
import jax
import jax.numpy as jnp
from jax import lax
import numpy as np

D_MODEL = 2048
BATCH = 32
SEQ = 256
DEPTH = 4
DEC_BATCH = 8
DEC_SEQ = 2048
PAST_LEN = 256

GRID_W = 64
HEAD_DIM = 128
N_HEADS = D_MODEL // HEAD_DIM
N_KV_HEADS = N_HEADS // 4
Q_GROUP = N_HEADS // N_KV_HEADS
Q_BLOCK = 128
ROPE_BASE = 10000.0
ROPE_FREQS = HEAD_DIM // 4
LRU_WIDTH = D_MODEL
LRU_BLOCKS = 16
LRU_BW = LRU_WIDTH // LRU_BLOCKS
CONV_W = 4
CONV_PAD_L = 2
LRU_C = 8.0
CHUNK = 128
CMLP_WIDTH = D_MODEL
CMLP_GROUPS = 16
CMLP_GW = CMLP_WIDTH // CMLP_GROUPS
D_FF = 4 * D_MODEL
N_BRANCH = 3
N_MOD = 6
EPS = 1e-6
Q_W = N_HEADS * HEAD_DIM
KV_W = N_KV_HEADS * HEAD_DIM
IN_SPLITS = (Q_W, KV_W, KV_W, LRU_WIDTH, LRU_WIDTH, CMLP_WIDTH, CMLP_WIDTH, N_BRANCH * D_MODEL)
IN_W = Q_W + 2 * KV_W + 2 * LRU_WIDTH + 2 * CMLP_WIDTH + N_BRANCH * D_MODEL

kernel_name = 'hybrid_gated_rglru_gqa_chunkmlp_step'


def _rmsnorm(x, g):
    xf = x.astype(jnp.float32)
    y = xf * lax.rsqrt(jnp.mean(xf * xf, axis=-1, keepdims=True) + EPS)
    return (y * g.astype(jnp.float32)).astype(x.dtype)


def _split_in(z):
    idx = []
    off = 0
    for s in IN_SPLITS[:-1]:
        off += s
        idx.append(off)
    return jnp.split(z, idx, axis=-1)


def _modulation(cond, w_mod, b_mod):
    m = jax.nn.silu(cond) @ w_mod + b_mod
    return [t[:, None, :] for t in jnp.split(m, N_MOD, axis=-1)]


def _axial_rope(n_tokens):
    rows = n_tokens // GRID_W
    pos_row = jnp.repeat(jnp.arange(rows), GRID_W).astype(jnp.float32)
    pos_col = (jnp.arange(n_tokens) % GRID_W).astype(jnp.float32)
    inv = ROPE_BASE ** (-jnp.arange(ROPE_FREQS, dtype=jnp.float32) / ROPE_FREQS)
    ang = jnp.stack([pos_row[:, None] * inv, pos_col[:, None] * inv], axis=1)
    return jnp.cos(ang), jnp.sin(ang)


def _apply_rope(x, cos, sin):
    xs = x.astype(jnp.float32).reshape(*x.shape[:-1], 2, 2, ROPE_FREQS)
    xa = xs[..., 0, :]
    xb = xs[..., 1, :]
    c = cos[None, :, None]
    s = sin[None, :, None]
    out = jnp.stack([xa * c - xb * s, xa * s + xb * c], axis=-2)
    return out.reshape(x.shape).astype(x.dtype)


def _attend(q, k, v):
    b, tq = q.shape[0], q.shape[1]
    nb = tq // Q_BLOCK
    qb = q.reshape(b, nb, Q_BLOCK, N_KV_HEADS, Q_GROUP, HEAD_DIM).transpose(1, 0, 2, 3, 4, 5)
    scale = HEAD_DIM ** -0.5

    def one_block(qblk):
        s = jnp.einsum('bqhgd,bkhd->bhgqk', qblk, k, preferred_element_type=jnp.float32) * scale
        p = jax.nn.softmax(s, axis=-1)
        return jnp.einsum('bhgqk,bkhd->bqhgd', p.astype(v.dtype), v)

    o = lax.map(one_block, qb)
    return o.transpose(1, 0, 2, 3, 4, 5).reshape(b, tq, Q_W)


def _dwconv(x, w, b):
    t = x.shape[1]
    xp = jnp.pad(x, ((0, 0), (CONV_PAD_L, CONV_W - 1 - CONV_PAD_L), (0, 0)))
    y = b
    for j in range(CONV_W):
        y = y + xp[:, j:j + t] * w[j]
    return y


def _blockdiag(x, w, b):
    xb = x.reshape(*x.shape[:-1], LRU_BLOCKS, LRU_BW)
    return jnp.einsum('btnc,ncd->btnd', xb, w).reshape(x.shape) + b


def _lin_combine(e1, e2):
    a1, b1 = e1
    a2, b2 = e2
    return (a1 * a2, a2 * b1 + b2)


def _rglru_dir(x, w_a, b_a, w_x, b_x, lam, h0, reverse):
    xf = x.astype(jnp.float32)
    r = jax.nn.sigmoid(_blockdiag(x, w_a, b_a).astype(jnp.float32))
    i = jax.nn.sigmoid(_blockdiag(x, w_x, b_x).astype(jnp.float32))
    log_a = -LRU_C * r * jax.nn.softplus(-lam.astype(jnp.float32))
    a = jnp.exp(log_a)
    bterm = jnp.sqrt(jnp.maximum(-jnp.expm1(2.0 * log_a), 0.0)) * (i * xf)
    edge = -1 if reverse else 0
    bterm = bterm.at[:, edge].add(a[:, edge] * h0.astype(jnp.float32))
    _, h = lax.associative_scan(_lin_combine, (a, bterm), axis=1, reverse=reverse)
    return h


def _chunk_mix(zu, zv, g_norm, w_s, b_s):
    b, t, _ = zu.shape
    u = jax.nn.gelu(zu)
    v = _rmsnorm(jax.nn.gelu(zv), g_norm)
    vb = v.reshape(b, t // CHUNK, CHUNK, CMLP_GROUPS, CMLP_GW)
    mixed = jnp.einsum('gqp,bnpgc->bnqgc', w_s, vb) + b_s.T[None, None, :, :, None]
    return u * mixed.reshape(b, t, CMLP_WIDTH)


def _layer(x, cond, p, rope, ctx_k, ctx_v, h0):
    b, t, _ = x.shape
    sh1, sc1, g1, sh2, sc2, g2 = _modulation(cond, p['w_mod'], p['b_mod'])
    h = _rmsnorm(x, p['g_pre_mix']) * (1 + sc1) + sh1
    zq, zk, zv, zlx, zlg, zcu, zcv, zg = _split_in(h @ p['w_in'])
    q = _rmsnorm(zq.reshape(b, t, N_HEADS, HEAD_DIM), p['g_q'])
    k = _rmsnorm(zk.reshape(b, t, N_KV_HEADS, HEAD_DIM), p['g_k'])
    v = zv.reshape(b, t, N_KV_HEADS, HEAD_DIM)
    if rope is None:
        k_all, v_all = k, v
    else:
        cos, sin = rope
        q = _apply_rope(q, cos, sin)
        k = _apply_rope(k, cos, sin)
        k_all = jnp.concatenate([k, ctx_k.astype(k.dtype)], axis=1)
        v_all = jnp.concatenate([v, ctx_v.astype(v.dtype)], axis=1)
    y_attn = _attend(q, k_all, v_all) @ p['w_attn_out']
    xc = _dwconv(zlx, p['conv_w'], p['conv_b'])
    if h0 is None:
        h0 = jnp.zeros((b, 2, LRU_WIDTH), jnp.float32)
    hf = _rglru_dir(xc, p['lru_wa'][0], p['lru_ba'][0], p['lru_wx'][0], p['lru_bx'][0], p['lru_lam'][0], h0[:, 0], False)
    hb = _rglru_dir(xc, p['lru_wa'][1], p['lru_ba'][1], p['lru_wx'][1], p['lru_bx'][1], p['lru_lam'][1], h0[:, 1], True)
    y_lru = ((hf + hb).astype(x.dtype) * jax.nn.gelu(zlg)) @ p['w_lru_out']
    lru_final = jnp.stack([hf[:, -1], hb[:, 0]], axis=1).astype(x.dtype)
    y_cm = _chunk_mix(zcu, zcv, p['cm_g'], p['cm_ws'], p['cm_bs']) @ p['w_cm_out']
    gates = jax.nn.sigmoid(zg.reshape(b, t, N_BRANCH, D_MODEL))
    merged = gates[:, :, 0] * y_attn + gates[:, :, 1] * y_lru + gates[:, :, 2] * y_cm
    x = x + g1 * _rmsnorm(merged @ p['w_out'], p['g_post_mix'])
    h2 = _rmsnorm(x, p['g_pre_ff']) * (1 + sc2) + sh2
    f = jnp.square(jax.nn.relu(h2 @ p['w_ff1'])) @ p['w_ff2']
    x = x + g2 * _rmsnorm(f, p['g_post_ff'])
    return x, k, v, lru_final


def setup_inputs(seed: int = 0) -> dict:
    key = jax.random.key(seed)
    ks = jax.random.split(key, 32)
    f32 = jnp.float32

    def nrm(k, shape, scale):
        return jax.random.normal(k, shape, f32) * scale

    def gain(k, shape):
        return 1.0 + 0.02 * jax.random.normal(k, shape, f32)

    u = jax.random.uniform(ks[23], (DEPTH, 2, LRU_WIDTH), f32, minval=0.9, maxval=0.999)
    sig = u ** (1.0 / LRU_C)
    lru_lam = jnp.log(sig) - jnp.log1p(-sig)
    return {
        'x_prompt': nrm(ks[0], (BATCH, SEQ, D_MODEL), 1.0),
        'x_sample': nrm(ks[1], (DEC_BATCH, DEC_SEQ, D_MODEL), 1.0),
        'cache_k': nrm(ks[2], (DEC_BATCH, DEPTH, PAST_LEN, N_KV_HEADS, HEAD_DIM), 1.0),
        'cache_v': nrm(ks[3], (DEC_BATCH, DEPTH, PAST_LEN, N_KV_HEADS, HEAD_DIM), 1.0),
        'state_lru': nrm(ks[4], (DEC_BATCH, DEPTH, 2, LRU_WIDTH), 0.5),
        'c': nrm(ks[5], (DEC_BATCH, D_MODEL), 1.0),
        'c_ctx': nrm(ks[6], (D_MODEL,), 1.0),
        'w_mod': nrm(ks[7], (DEPTH, D_MODEL, N_MOD * D_MODEL), 0.5 * D_MODEL ** -0.5),
        'b_mod': nrm(ks[8], (DEPTH, N_MOD * D_MODEL), 0.02),
        'g_pre_mix': gain(ks[9], (DEPTH, D_MODEL)),
        'g_post_mix': gain(ks[10], (DEPTH, D_MODEL)),
        'g_pre_ff': gain(ks[11], (DEPTH, D_MODEL)),
        'g_post_ff': gain(ks[12], (DEPTH, D_MODEL)),
        'w_in': nrm(ks[13], (DEPTH, D_MODEL, IN_W), D_MODEL ** -0.5),
        'g_q': gain(ks[14], (DEPTH, HEAD_DIM)),
        'g_k': gain(ks[15], (DEPTH, HEAD_DIM)),
        'w_attn_out': nrm(ks[16], (DEPTH, Q_W, D_MODEL), Q_W ** -0.5),
        'conv_w': nrm(ks[17], (DEPTH, CONV_W, LRU_WIDTH), CONV_W ** -0.5),
        'conv_b': nrm(ks[18], (DEPTH, LRU_WIDTH), 0.02),
        'lru_wa': nrm(ks[19], (DEPTH, 2, LRU_BLOCKS, LRU_BW, LRU_BW), LRU_BW ** -0.5),
        'lru_ba': nrm(ks[20], (DEPTH, 2, LRU_WIDTH), 0.1),
        'lru_wx': nrm(ks[21], (DEPTH, 2, LRU_BLOCKS, LRU_BW, LRU_BW), LRU_BW ** -0.5),
        'lru_bx': nrm(ks[22], (DEPTH, 2, LRU_WIDTH), 0.1),
        'lru_lam': lru_lam,
        'w_lru_out': nrm(ks[24], (DEPTH, LRU_WIDTH, D_MODEL), LRU_WIDTH ** -0.5),
        'cm_g': gain(ks[25], (DEPTH, CMLP_WIDTH)),
        'cm_ws': nrm(ks[26], (DEPTH, CMLP_GROUPS, CHUNK, CHUNK), CHUNK ** -0.5),
        'cm_bs': nrm(ks[27], (DEPTH, CMLP_GROUPS, CHUNK), 0.1),
        'w_cm_out': nrm(ks[28], (DEPTH, CMLP_WIDTH, D_MODEL), CMLP_WIDTH ** -0.5),
        'w_out': nrm(ks[29], (DEPTH, D_MODEL, D_MODEL), D_MODEL ** -0.5),
        'w_ff1': nrm(ks[30], (DEPTH, D_MODEL, D_FF), D_MODEL ** -0.5),
        'w_ff2': nrm(ks[31], (DEPTH, D_FF, D_MODEL), D_FF ** -0.5),
    }


def reference(x_prompt, x_sample, cache_k, cache_v, state_lru, c, c_ctx,
              w_mod, b_mod, g_pre_mix, g_post_mix, g_pre_ff, g_post_ff,
              w_in, g_q, g_k, w_attn_out, conv_w, conv_b,
              lru_wa, lru_ba, lru_wx, lru_bx, lru_lam, w_lru_out,
              cm_g, cm_ws, cm_bs, w_cm_out, w_out, w_ff1, w_ff2):
    rope = _axial_rope(x_sample.shape[1])
    cond_ctx = c_ctx[None, :]
    y_p = x_prompt
    y_s = x_sample
    new_k, new_v, new_s = [], [], []
    for l in range(DEPTH):
        p = {
            'w_mod': w_mod[l], 'b_mod': b_mod[l],
            'g_pre_mix': g_pre_mix[l], 'g_post_mix': g_post_mix[l],
            'g_pre_ff': g_pre_ff[l], 'g_post_ff': g_post_ff[l],
            'w_in': w_in[l], 'g_q': g_q[l], 'g_k': g_k[l], 'w_attn_out': w_attn_out[l],
            'conv_w': conv_w[l], 'conv_b': conv_b[l],
            'lru_wa': lru_wa[l], 'lru_ba': lru_ba[l], 'lru_wx': lru_wx[l], 'lru_bx': lru_bx[l],
            'lru_lam': lru_lam[l], 'w_lru_out': w_lru_out[l],
            'cm_g': cm_g[l], 'cm_ws': cm_ws[l], 'cm_bs': cm_bs[l], 'w_cm_out': w_cm_out[l],
            'w_out': w_out[l], 'w_ff1': w_ff1[l], 'w_ff2': w_ff2[l],
        }
        y_p, k_l, v_l, s_l = _layer(y_p, cond_ctx, p, None, None, None, None)
        new_k.append(k_l)
        new_v.append(v_l)
        new_s.append(s_l)
        y_s, _, _, _ = _layer(y_s, c, p, rope, cache_k[:, l], cache_v[:, l], state_lru[:, l])
    new_cache_k = jnp.stack(new_k, axis=1)
    new_cache_v = jnp.stack(new_v, axis=1)
    new_state_lru = jnp.stack(new_s, axis=1)
    return (y_prompt_out := y_p, y_s, new_cache_k, new_cache_v, new_state_lru)
```

```python
import functools

import jax
import jax.numpy as jnp
from jax import lax
from jax.experimental import pallas as pl
from jax.experimental.pallas import tpu as pltpu

F32 = jnp.float32
BF16 = jnp.bfloat16

EPS = 1e-6
HEAD_DIM = 128
GRID_W = 64
ROPE_BASE = 10000.0
ROPE_FREQS = HEAD_DIM // 4
CONV_W = 4
CONV_PAD_L = 2
LRU_C = 8.0
CHUNK = 128
N_MOD = 6
N_BRANCH = 3

LANES = 128
SUBLANES = 8
VMEM_LIMIT_BYTES = 56 * 1024 * 1024

MOD_SHIFT1, MOD_SCALE1, MOD_GATE1, MOD_SHIFT2, MOD_SCALE2, MOD_GATE2 = range(N_MOD)


def _params(*semantics):
    return pltpu.CompilerParams(dimension_semantics=semantics, vmem_limit_bytes=VMEM_LIMIT_BYTES)


def _rms(x, g):
    return x * lax.rsqrt(jnp.mean(x * x, axis=-1, keepdims=True) + EPS) * g


def _row_tile(m, cap):
    if m <= cap:
        return m
    t = cap - cap % LANES
    while m % t:
        t -= LANES
    assert t > 0
    return t


def _mod_kernel(c_ref, w_ref, b_ref, o_ref):
    c = c_ref[...]
    s = (c * jax.nn.sigmoid(c)).astype(BF16)
    o_ref[...] = jnp.dot(s, w_ref[...].astype(BF16), preferred_element_type=F32) + b_ref[...]


def _modulation(cond, w_mod, b_mod):
    nc, d = cond.shape
    depth, _, n = w_mod.shape
    tn = _row_tile(n, 1024)
    out = pl.pallas_call(
        _mod_kernel,
        grid=(depth, n // tn),
        in_specs=[
            pl.BlockSpec((nc, d), lambda l, j: (0, 0)),
            pl.BlockSpec((None, d, tn), lambda l, j: (l, 0, j)),
            pl.BlockSpec((None, 1, tn), lambda l, j: (l, 0, j)),
        ],
        out_specs=pl.BlockSpec((None, nc, tn), lambda l, j: (l, 0, j)),
        out_shape=jax.ShapeDtypeStruct((depth, nc, n), F32),
        compiler_params=_params("parallel", "parallel"),
        name="modulation",
    )(cond, w_mod, b_mod.reshape(depth, 1, n))
    return out.reshape(depth, nc, N_MOD, d)


def _prenorm_mm_kernel(x_ref, g_ref, mod_ref, w_ref, o_ref, h_ref, *, shift_row, scale_row, relu2):
    @pl.when(pl.program_id(1) == 0)
    def _():
        y = _rms(x_ref[...], g_ref[...])
        h = y * (1.0 + mod_ref[scale_row:scale_row + 1, :]) + mod_ref[shift_row:shift_row + 1, :]
        h_ref[...] = h.astype(BF16)

    acc = jnp.dot(h_ref[...], w_ref[...], preferred_element_type=F32)
    if relu2:
        acc = jnp.square(jnp.maximum(acc, 0.0))
    o_ref[...] = acc.astype(o_ref.dtype)


def _prenorm_matmul(x, g, mod, w, *, rows_per_cond, shift_row, scale_row, relu2, name):
    m, d = x.shape
    n = w.shape[1]
    tm = _row_tile(rows_per_cond, 1024)
    tn = _row_tile(n, 1024)
    assert rows_per_cond % tm == 0
    return pl.pallas_call(
        functools.partial(_prenorm_mm_kernel, shift_row=shift_row, scale_row=scale_row, relu2=relu2),
        grid=(m // tm, n // tn),
        in_specs=[
            pl.BlockSpec((tm, d), lambda i, j: (i, 0)),
            pl.BlockSpec((1, d), lambda i, j: (0, 0)),
            pl.BlockSpec((None, N_MOD, d), lambda i, j: ((i * tm) // rows_per_cond, 0, 0)),
            pl.BlockSpec((d, tn), lambda i, j: (0, j)),
        ],
        out_specs=pl.BlockSpec((tm, tn), lambda i, j: (i, j)),
        out_shape=jax.ShapeDtypeStruct((m, n), BF16),
        scratch_shapes=[pltpu.VMEM((tm, d), BF16)],
        compiler_params=_params("parallel", "arbitrary"),
        name=name,
    )(x, g.reshape(1, d), mod, w)


def _mm_norm_res_kernel(a_ref, w_ref, g_ref, mod_ref, x_ref, o_ref, acc_ref, *, gate_row, nk):
    k = pl.program_id(1)
    part = jnp.dot(a_ref[...], w_ref[...], preferred_element_type=F32)

    @pl.when(k == 0)
    def _():
        acc_ref[...] = part

    @pl.when(k > 0)
    def _():
        acc_ref[...] += part

    @pl.when(k == nk - 1)
    def _():
        o_ref[...] = x_ref[...] + mod_ref[gate_row:gate_row + 1, :] * _rms(acc_ref[...], g_ref[...])


def _matmul_norm_residual(a, w, g, mod, x, *, rows_per_cond, gate_row, name):
    m, kdim = a.shape
    d = w.shape[1]
    tm = _row_tile(rows_per_cond, 512)
    tk = _row_tile(kdim, 1024)
    nk = kdim // tk
    assert rows_per_cond % tm == 0
    return pl.pallas_call(
        functools.partial(_mm_norm_res_kernel, gate_row=gate_row, nk=nk),
        grid=(m // tm, nk),
        in_specs=[
            pl.BlockSpec((tm, tk), lambda i, k: (i, k)),
            pl.BlockSpec((tk, d), lambda i, k: (k, 0)),
            pl.BlockSpec((1, d), lambda i, k: (0, 0)),
            pl.BlockSpec((None, N_MOD, d), lambda i, k: ((i * tm) // rows_per_cond, 0, 0)),
            pl.BlockSpec((tm, d), lambda i, k: (i, 0)),
        ],
        out_specs=pl.BlockSpec((tm, d), lambda i, k: (i, 0)),
        out_shape=jax.ShapeDtypeStruct((m, d), F32),
        scratch_shapes=[pltpu.VMEM((tm, d), F32)],
        compiler_params=_params("parallel", "arbitrary"),
        name=name,
    )(a, w, g.reshape(1, d), mod, x)


def _merge_kernel(a0_ref, a1_ref, a2_ref, w0_ref, w1_ref, w2_ref, g0_ref, g1_ref, g2_ref, o_ref):
    acc = jax.nn.sigmoid(g0_ref[...].astype(F32)) * jnp.dot(a0_ref[...], w0_ref[...], preferred_element_type=F32)
    acc += jax.nn.sigmoid(g1_ref[...].astype(F32)) * jnp.dot(a1_ref[...], w1_ref[...], preferred_element_type=F32)
    acc += jax.nn.sigmoid(g2_ref[...].astype(F32)) * jnp.dot(a2_ref[...], w2_ref[...], preferred_element_type=F32)
    o_ref[...] = acc.astype(o_ref.dtype)


def _merge(branches, weights, z, gate_col0, name):
    m, d = branches[0].shape
    tm = _row_tile(m, 512)
    tn = _row_tile(d, 512)
    assert gate_col0 % tn == 0 and d % tn == 0
    a_spec = pl.BlockSpec((tm, d), lambda i, j: (i, 0))
    w_spec = pl.BlockSpec((d, tn), lambda i, j: (0, j))

    def gate_spec(b):
        off = (gate_col0 + b * d) // tn
        return pl.BlockSpec((tm, tn), lambda i, j: (i, off + j))

    return pl.pallas_call(
        _merge_kernel,
        grid=(m // tm, d // tn),
        in_specs=[a_spec] * N_BRANCH + [w_spec] * N_BRANCH + [gate_spec(b) for b in range(N_BRANCH)],
        out_specs=pl.BlockSpec((tm, tn), lambda i, j: (i, j)),
        out_shape=jax.ShapeDtypeStruct((m, d), BF16),
        compiler_params=_params("parallel", "arbitrary"),
        name=name,
    )(*branches, *weights, z, z, z)


def _rope(x, c, s):
    lane = lax.broadcasted_iota(jnp.int32, x.shape, 1)
    partner = jnp.where((lane % (2 * ROPE_FREQS)) < ROPE_FREQS,
                        pltpu.roll(x, HEAD_DIM - ROPE_FREQS, 1),
                        pltpu.roll(x, ROPE_FREQS, 1))
    return x * c + partner * s


def _attn_kernel(*refs, has_ctx, t, tq, q_group):
    if has_ctx:
        (zq_ref, zk_ref, zv_ref, gq_ref, gk_ref, cq_ref, sq_ref, ck_ref, sk_ref, pk_ref, pv_ref,
         o_ref, kt_ref, v_ref) = refs
    else:
        zq_ref, zk_ref, zv_ref, gq_ref, gk_ref, o_ref, ko_ref, vo_ref, kt_ref, v_ref = refs

    @pl.when(pl.program_id(2) == 0)
    def _():
        k = _rms(zk_ref[...].astype(F32), gk_ref[...])
        if has_ctx:
            k = _rope(k, ck_ref[...], sk_ref[...])
            kt_ref[:, 0:t] = k.T.astype(BF16)
            kt_ref[:, t:] = pk_ref[...].T.astype(BF16)
            v_ref[0:t, :] = zv_ref[...]
            v_ref[t:, :] = pv_ref[...].astype(BF16)
        else:
            ko_ref[...] = k
            vo_ref[...] = zv_ref[...].astype(F32)
            kt_ref[...] = k.T.astype(BF16)
            v_ref[...] = zv_ref[...]

    qs = []
    for g in range(q_group):
        q = _rms(zq_ref[:, g * HEAD_DIM:(g + 1) * HEAD_DIM].astype(F32), gq_ref[...])
        if has_ctx:
            q = _rope(q, cq_ref[...], sq_ref[...])
        qs.append(q.astype(BF16))
    q_all = jnp.concatenate(qs, axis=0)
    s = jnp.dot(q_all, kt_ref[...], preferred_element_type=F32) * (HEAD_DIM ** -0.5)
    p = jnp.exp(s - jnp.max(s, axis=-1, keepdims=True))
    denom = jnp.sum(p, axis=-1, keepdims=True)
    o = jnp.dot(p.astype(BF16), v_ref[...], preferred_element_type=F32) / denom
    for g in range(q_group):
        o_ref[:, g * HEAD_DIM:(g + 1) * HEAD_DIM] = o[g * tq:(g + 1) * tq].astype(o_ref.dtype)


def _attention(z, g_q, g_k, *, n_seq, t, d, kv_w, k_col, v_col, rope=None, past_k=None, past_v=None, layer=None, name):
    has_ctx = rope is not None
    n_kv = kv_w // HEAD_DIM
    q_group = d // kv_w
    qw = q_group * HEAD_DIM
    tq = _row_tile(t, 128)
    nq = t // tq
    past = past_k.shape[2] if has_ctx else 0
    tk = t + past

    in_specs = [
        pl.BlockSpec((tq, qw), lambda b, h, i: (b * nq + i, h)),
        pl.BlockSpec((t, HEAD_DIM), lambda b, h, i: (b, k_col // HEAD_DIM + h)),
        pl.BlockSpec((t, HEAD_DIM), lambda b, h, i: (b, v_col // HEAD_DIM + h)),
        pl.BlockSpec((1, HEAD_DIM), lambda b, h, i: (0, 0)),
        pl.BlockSpec((1, HEAD_DIM), lambda b, h, i: (0, 0)),
    ]
    args = [z, z, z, g_q.reshape(1, HEAD_DIM), g_k.reshape(1, HEAD_DIM)]
    o_spec = pl.BlockSpec((tq, qw), lambda b, h, i: (b * nq + i, h))
    o_shape = jax.ShapeDtypeStruct((n_seq * t, d), BF16)
    if has_ctx:
        cos, sin = rope
        in_specs += [
            pl.BlockSpec((tq, HEAD_DIM), lambda b, h, i: (i, 0)),
            pl.BlockSpec((tq, HEAD_DIM), lambda b, h, i: (i, 0)),
            pl.BlockSpec((t, HEAD_DIM), lambda b, h, i: (0, 0)),
            pl.BlockSpec((t, HEAD_DIM), lambda b, h, i: (0, 0)),
            pl.BlockSpec((None, None, past, HEAD_DIM), lambda b, h, i: (b, layer, 0, h)),
            pl.BlockSpec((None, None, past, HEAD_DIM), lambda b, h, i: (b, layer, 0, h)),
        ]
        args += [cos, sin, cos, sin, past_k, past_v]
        out_specs = o_spec
        out_shape = o_shape
    else:
        kv_spec = pl.BlockSpec((None, t, HEAD_DIM), lambda b, h, i: (b, 0, h))
        kv_shape = jax.ShapeDtypeStruct((n_seq, t, kv_w), F32)
        out_specs = [o_spec, kv_spec, kv_spec]
        out_shape = [o_shape, kv_shape, kv_shape]

    return pl.pallas_call(
        functools.partial(_attn_kernel, has_ctx=has_ctx, t=t, tq=tq, q_group=q_group),
        grid=(n_seq, n_kv, nq),
        in_specs=in_specs,
        out_specs=out_specs,
        out_shape=out_shape,
        scratch_shapes=[pltpu.VMEM((HEAD_DIM, tk), BF16), pltpu.VMEM((tk, HEAD_DIM), BF16)],
        compiler_params=_params("parallel", "parallel", "arbitrary"),
        name=name,
    )(*args)


EXPM1_SERIES_RANGE = 0.25
EXPM1_SERIES_COEFFS = (1.0 / 720.0, 1.0 / 120.0, 1.0 / 24.0, 1.0 / 6.0, 0.5, 1.0)


def _expm1_nonpositive(y, exp_y):
    series = jnp.full_like(y, EXPM1_SERIES_COEFFS[0])
    for coeff in EXPM1_SERIES_COEFFS[1:]:
        series = series * y + coeff
    return jnp.where(y > -EXPM1_SERIES_RANGE, series * y, exp_y - 1.0)


def _lru_kernel(zx_ref, zg_ref, cw_ref, cb_ref, wa_ref, wx_ref, ba_ref, bx_ref, lam_ref, h0_ref,
                o_ref, hfin_ref, xpad_ref, xc_ref, hf_ref, *, t, tc):
    cw = zx_ref.shape[1]
    n_lane_blocks = cw // LANES
    n_chunks = t // tc
    pad = SUBLANES
    groups = tc // SUBLANES

    xpad_ref[0:pad, :] = jnp.zeros((pad, cw), F32)
    xpad_ref[t + pad:t + 2 * pad, :] = jnp.zeros((pad, cw), F32)

    def copy_chunk(c, carry):
        r = pl.multiple_of(c * tc, tc)
        xpad_ref[pl.ds(r + pad, tc), :] = zx_ref[pl.ds(r, tc), :].astype(F32)
        return carry

    lax.fori_loop(0, n_chunks, copy_chunk, 0)

    def conv_chunk(c, carry):
        r = pl.multiple_of(c * tc, tc)
        xe = xpad_ref[pl.ds(r, tc + 2 * pad), :]
        n = tc + 2 * pad
        y = cb_ref[...] + pltpu.roll(xe, 2, 0)[pad:pad + tc] * cw_ref[0:1, :]
        y = y + pltpu.roll(xe, 1, 0)[pad:pad + tc] * cw_ref[1:2, :]
        y = y + xe[pad:pad + tc] * cw_ref[2:3, :]
        y = y + pltpu.roll(xe, n - 1, 0)[pad:pad + tc] * cw_ref[3:4, :]
        xc_ref[pl.ds(r, tc), :] = y
        return carry

    lax.fori_loop(0, n_chunks, conv_chunk, 0)

    row_in_group = lax.broadcasted_iota(jnp.int32, (tc, LANES), 0) % SUBLANES

    def gates(xn, direction, nb):
        lanes = slice(nb * LANES, (nb + 1) * LANES)
        xb = xn.astype(BF16)
        r = jax.nn.sigmoid(jnp.dot(xb, wa_ref[direction, nb], preferred_element_type=F32)
                           + ba_ref[direction:direction + 1, lanes])
        i = jax.nn.sigmoid(jnp.dot(xb, wx_ref[direction, nb], preferred_element_type=F32)
                           + bx_ref[direction:direction + 1, lanes])
        log_a = -LRU_C * r * jax.nn.softplus(-lam_ref[direction:direction + 1, lanes])
        a = jnp.exp(log_a)
        b = jnp.sqrt(jnp.maximum(-_expm1_nonpositive(2.0 * log_a, a * a), 0.0)) * (i * xn)
        return a, b

    def group_scan(a, b, reverse):
        for dist in (1, 2, 4):
            if reverse:
                edge = row_in_group >= SUBLANES - dist
                shift = tc - dist
            else:
                edge = row_in_group < dist
                shift = dist
            a_prev = jnp.where(edge, 1.0, pltpu.roll(a, shift, 0))
            b_prev = jnp.where(edge, 0.0, pltpu.roll(b, shift, 0))
            b = a * b_prev + b
            a = a * a_prev
        return a, b

    def forward_chunk(c, carry):
        r = pl.multiple_of(c * tc, tc)
        new_carry = []
        for nb in range(n_lane_blocks):
            lanes = slice(nb * LANES, (nb + 1) * LANES)
            a, b = gates(xc_ref[pl.ds(r, tc), lanes], 0, nb)
            a, b = group_scan(a, b, reverse=False)
            h_prev = carry[nb]
            for v in range(groups):
                rows = slice(v * SUBLANES, (v + 1) * SUBLANES)
                h = a[rows] * h_prev + b[rows]
                hf_ref[pl.ds(r + v * SUBLANES, SUBLANES), lanes] = h
                h_prev = jnp.broadcast_to(h[SUBLANES - 1:SUBLANES], (SUBLANES, LANES))
            new_carry.append(h_prev)
        return tuple(new_carry)

    h0f = tuple(jnp.broadcast_to(h0_ref[0:1, nb * LANES:(nb + 1) * LANES], (SUBLANES, LANES))
                for nb in range(n_lane_blocks))
    hf_last = lax.fori_loop(0, n_chunks, forward_chunk, h0f)

    def backward_chunk(c, carry):
        r = pl.multiple_of((n_chunks - 1 - c) * tc, tc)
        new_carry = []
        for nb in range(n_lane_blocks):
            lanes = slice(nb * LANES, (nb + 1) * LANES)
            a, b = gates(xc_ref[pl.ds(r, tc), lanes], 1, nb)
            a, b = group_scan(a, b, reverse=True)
            h_prev = carry[nb]
            for v in reversed(range(groups)):
                rows = slice(v * SUBLANES, (v + 1) * SUBLANES)
                row0 = r + v * SUBLANES
                h = a[rows] * h_prev + b[rows]
                gate = jax.nn.gelu(zg_ref[pl.ds(row0, SUBLANES), lanes].astype(F32))
                o_ref[pl.ds(row0, SUBLANES), lanes] = ((hf_ref[pl.ds(row0, SUBLANES), lanes] + h) * gate).astype(o_ref.dtype)
                h_prev = jnp.broadcast_to(h[0:1], (SUBLANES, LANES))
            new_carry.append(h_prev)
        return tuple(new_carry)

    h0b = tuple(jnp.broadcast_to(h0_ref[1:2, nb * LANES:(nb + 1) * LANES], (SUBLANES, LANES))
                for nb in range(n_lane_blocks))
    hb_first = lax.fori_loop(0, n_chunks, backward_chunk, h0b)

    for nb in range(n_lane_blocks):
        lanes = slice(nb * LANES, (nb + 1) * LANES)
        hfin_ref[0:1, lanes] = hf_last[nb][0:1]
        hfin_ref[1:2, lanes] = hb_first[nb][0:1]


def _rglru(z, conv_w, conv_b, wa, wx, ba, bx, lam, h0, *, n_seq, t, width, x_col, g_col, name):
    cw = _row_tile(width, 512)
    tc = _row_tile(t, 128)
    lb = cw // LANES
    assert wa.shape[-1] == LANES and x_col % cw == 0 and g_col % cw == 0
    vec_spec = pl.BlockSpec((2, cw), lambda s, c: (0, c))
    w_spec = pl.BlockSpec((2, lb, LANES, LANES), lambda s, c: (0, c, 0, 0))
    return pl.pallas_call(
        functools.partial(_lru_kernel, t=t, tc=tc),
        grid=(n_seq, width // cw),
        in_specs=[
            pl.BlockSpec((t, cw), lambda s, c: (s, x_col // cw + c)),
            pl.BlockSpec((t, cw), lambda s, c: (s, g_col // cw + c)),
            pl.BlockSpec((CONV_W, cw), lambda s, c: (0, c)),
            pl.BlockSpec((1, cw), lambda s, c: (0, c)),
            w_spec, w_spec, vec_spec, vec_spec, vec_spec,
            pl.BlockSpec((None, 2, cw), lambda s, c: (s, 0, c)),
        ],
        out_specs=[
            pl.BlockSpec((t, cw), lambda s, c: (s, c)),
            pl.BlockSpec((None, 2, cw), lambda s, c: (s, 0, c)),
        ],
        out_shape=[
            jax.ShapeDtypeStruct((n_seq * t, width), BF16),
            jax.ShapeDtypeStruct((n_seq, 2, width), F32),
        ],
        scratch_shapes=[
            pltpu.VMEM((t + 2 * SUBLANES, cw), F32),
            pltpu.VMEM((t, cw), F32),
            pltpu.VMEM((t, cw), F32),
        ],
        compiler_params=_params("parallel", "parallel"),
        name=name,
    )(z, z, conv_w, conv_b.reshape(1, width), wa, wx, ba, bx, lam, h0)


def _cmix_kernel(zu_ref, zv_ref, g_ref, ws_ref, bs_ref, o_ref, v_ref, *, n_groups):
    rows, width = zu_ref.shape
    gw = width // n_groups
    v_ref[...] = _rms(jax.nn.gelu(zv_ref[...].astype(F32)), g_ref[...]).astype(BF16)
    for ch in range(rows // CHUNK):
        r = slice(ch * CHUNK, (ch + 1) * CHUNK)
        for g in range(n_groups):
            cols = slice(g * gw, (g + 1) * gw)
            mixed = jnp.dot(ws_ref[g], v_ref[r, cols], preferred_element_type=F32) + bs_ref[g]
            o_ref[r, cols] = (jax.nn.gelu(zu_ref[r, cols].astype(F32)) * mixed).astype(o_ref.dtype)


def _chunk_mix(z, cm_g, ws, bs, *, width, u_col, v_col, name):
    m = z.shape[0]
    n_groups = ws.shape[0]
    tr = _row_tile(m, 2 * CHUNK)
    assert u_col % width == 0 and v_col % width == 0
    return pl.pallas_call(
        functools.partial(_cmix_kernel, n_groups=n_groups),
        grid=(m // tr,),
        in_specs=[
            pl.BlockSpec((tr, width), lambda i: (i, u_col // width)),
            pl.BlockSpec((tr, width), lambda i: (i, v_col // width)),
            pl.BlockSpec((1, width), lambda i: (0, 0)),
            pl.BlockSpec(ws.shape, lambda i: (0, 0, 0)),
            pl.BlockSpec(bs.shape, lambda i: (0, 0, 0)),
        ],
        out_specs=pl.BlockSpec((tr, width), lambda i: (i, 0)),
        out_shape=jax.ShapeDtypeStruct((m, width), BF16),
        scratch_shapes=[pltpu.VMEM((tr, width), BF16)],
        compiler_params=_params("parallel"),
        name=name,
    )(z, z, cm_g.reshape(1, width), ws, bs)


def _rope_tables(n_tokens):
    rows = n_tokens // GRID_W
    pos_row = jnp.repeat(jnp.arange(rows), GRID_W).astype(F32)
    pos_col = (jnp.arange(n_tokens) % GRID_W).astype(F32)
    inv = ROPE_BASE ** (-jnp.arange(ROPE_FREQS, dtype=F32) / ROPE_FREQS)
    cr, sr = jnp.cos(pos_row[:, None] * inv), jnp.sin(pos_row[:, None] * inv)
    cc, sc = jnp.cos(pos_col[:, None] * inv), jnp.sin(pos_col[:, None] * inv)
    return (jnp.concatenate([cr, cr, cc, cc], axis=-1), jnp.concatenate([-sr, sr, -sc, sc], axis=-1))


def _layer(x, mod, p, tag, *, n_seq, t, rows_per_cond, rope=None, past_k=None, past_v=None, layer=None, h0=None):
    d = x.shape[1]
    kv_w = p["kv_w"]
    col_lx, col_lg, col_cu, col_cv, col_g, col_k = d, 2 * d, 3 * d, 4 * d, 5 * d, 8 * d
    col_v = col_k + kv_w

    z = _prenorm_matmul(x, p["g_pre_mix"], mod, p["w_in"], rows_per_cond=rows_per_cond,
                        shift_row=MOD_SHIFT1, scale_row=MOD_SCALE1, relu2=False, name=f"in_proj_{tag}")
    attn = _attention(z, p["g_q"], p["g_k"], n_seq=n_seq, t=t, d=d, kv_w=kv_w, k_col=col_k, v_col=col_v,
                      rope=rope, past_k=past_k, past_v=past_v, layer=layer, name=f"attention_{tag}")
    if rope is None:
        attn_o, k_new, v_new = attn
    else:
        attn_o, k_new, v_new = attn, None, None
    lru_o, h_fin = _rglru(z, p["conv_w"], p["conv_b"], p["lru_wa"], p["lru_wx"], p["lru_ba"], p["lru_bx"],
                          p["lru_lam"], h0, n_seq=n_seq, t=t, width=d, x_col=col_lx, g_col=col_lg,
                          name=f"rglru_{tag}")
    cm_o = _chunk_mix(z, p["cm_g"], p["cm_ws"], p["cm_bs"], width=d, u_col=col_cu, v_col=col_cv,
                      name=f"chunk_mix_{tag}")
    merged = _merge((attn_o, lru_o, cm_o), (p["w_attn_out"], p["w_lru_out"], p["w_cm_out"]), z, col_g,
                    name=f"merge_{tag}")
    x = _matmul_norm_residual(merged, p["w_out"], p["g_post_mix"], mod, x, rows_per_cond=rows_per_cond,
                              gate_row=MOD_GATE1, name=f"out_proj_{tag}")
    f = _prenorm_matmul(x, p["g_pre_ff"], mod, p["w_ff1"], rows_per_cond=rows_per_cond,
                        shift_row=MOD_SHIFT2, scale_row=MOD_SCALE2, relu2=True, name=f"ff1_{tag}")
    x = _matmul_norm_residual(f, p["w_ff2"], p["g_post_ff"], mod, x, rows_per_cond=rows_per_cond,
                              gate_row=MOD_GATE2, name=f"ff2_{tag}")
    return x, k_new, v_new, h_fin


def kernel(x_prompt, x_sample, cache_k, cache_v, state_lru, c, c_ctx, w_mod, b_mod, g_pre_mix, g_post_mix, g_pre_ff, g_post_ff, w_in, g_q, g_k, w_attn_out, conv_w, conv_b, lru_wa, lru_ba, lru_wx, lru_bx, lru_lam, w_lru_out, cm_g, cm_ws, cm_bs, w_cm_out, w_out, w_ff1, w_ff2):
    batch, seq, d = x_prompt.shape
    dec_batch, dec_seq, _ = x_sample.shape
    depth = w_in.shape[0]
    in_w = w_in.shape[2]
    kv_w = (in_w - 8 * d) // 2
    n_kv = kv_w // HEAD_DIM
    past = cache_k.shape[2]
    n_groups = cm_ws.shape[1]

    n_cond = 1 + dec_batch
    cond = jnp.concatenate([c_ctx[None, :], c], axis=0)
    cond = jnp.pad(cond, ((0, -n_cond % SUBLANES), (0, 0)))
    mod = _modulation(cond, w_mod, b_mod)

    q_w = d
    o_k, o_v, o_rest = q_w, q_w + kv_w, q_w + 2 * kv_w
    w_in_b = jnp.concatenate([w_in[:, :, :q_w], w_in[:, :, o_rest:], w_in[:, :, o_k:o_rest]], axis=-1).astype(BF16)
    w_attn_out_b, w_lru_out_b, w_cm_out_b = w_attn_out.astype(BF16), w_lru_out.astype(BF16), w_cm_out.astype(BF16)
    w_out_b, w_ff1_b, w_ff2_b = w_out.astype(BF16), w_ff1.astype(BF16), w_ff2.astype(BF16)
    lru_wa_b, lru_wx_b, cm_ws_b = lru_wa.astype(BF16), lru_wx.astype(BF16), cm_ws.astype(BF16)
    cm_bs_b = jnp.broadcast_to(cm_bs[:, :, :, None], cm_bs.shape + (d // n_groups,))

    rope = _rope_tables(dec_seq)
    past_k = cache_k.reshape(dec_batch, depth, past, kv_w)
    past_v = cache_v.reshape(dec_batch, depth, past, kv_w)
    h0_prompt = jnp.zeros((batch, 2, d), F32)

    y_p = x_prompt.reshape(batch * seq, d)
    y_s = x_sample.reshape(dec_batch * dec_seq, d)
    new_k, new_v, new_s = [], [], []
    for l in range(depth):
        p = {
            "kv_w": kv_w, "g_pre_mix": g_pre_mix[l], "g_post_mix": g_post_mix[l], "g_pre_ff": g_pre_ff[l],
            "g_post_ff": g_post_ff[l], "w_in": w_in_b[l], "g_q": g_q[l], "g_k": g_k[l],
            "w_attn_out": w_attn_out_b[l], "conv_w": conv_w[l], "conv_b": conv_b[l],
            "lru_wa": lru_wa_b[l], "lru_ba": lru_ba[l], "lru_wx": lru_wx_b[l], "lru_bx": lru_bx[l],
            "lru_lam": lru_lam[l], "w_lru_out": w_lru_out_b[l], "cm_g": cm_g[l], "cm_ws": cm_ws_b[l],
            "cm_bs": cm_bs_b[l], "w_cm_out": w_cm_out_b[l], "w_out": w_out_b[l], "w_ff1": w_ff1_b[l],
            "w_ff2": w_ff2_b[l],
        }
        y_p, k_l, v_l, s_l = _layer(y_p, mod[l, 0:1], p, f"ctx{l}", n_seq=batch, t=seq,
                                    rows_per_cond=batch * seq, h0=h0_prompt)
        new_k.append(k_l.reshape(batch, seq, n_kv, HEAD_DIM))
        new_v.append(v_l.reshape(batch, seq, n_kv, HEAD_DIM))
        new_s.append(s_l)
        y_s, _, _, _ = _layer(y_s, mod[l, 1:1 + dec_batch], p, f"lat{l}", n_seq=dec_batch, t=dec_seq,
                              rows_per_cond=dec_seq, rope=rope, past_k=past_k, past_v=past_v, layer=l,
                              h0=state_lru[:, l])
    return (y_p.reshape(batch, seq, d), y_s.reshape(dec_batch, dec_seq, d),
            jnp.stack(new_k, axis=1), jnp.stack(new_v, axis=1), jnp.stack(new_s, axis=1))
```

```python
import functools

import jax
import jax.numpy as jnp
from jax import lax
from jax.experimental import pallas as pl
from jax.experimental.pallas import tpu as pltpu

F32 = jnp.float32
BF16 = jnp.bfloat16

EPS = 1e-6
HEAD_DIM = 128
GRID_W = 64
ROPE_BASE = 10000.0
ROPE_FREQS = HEAD_DIM // 4
CONV_W = 4
CONV_PAD_L = 2
LRU_C = 8.0
CHUNK = 128
N_MOD = 6
N_BRANCH = 3
LOG2_E = 1.4426950408889634

LANES = 128
SUBLANES = 8
VMEM_LIMIT_BYTES = 56 * 1024 * 1024
BF16_ROWS = 2 * SUBLANES
ROW_CHUNK = BF16_ROWS
NORM_UNROLL = 4
VREG_BUDGET_ELEMS = 32 * SUBLANES * LANES

MOD_SHIFT1, MOD_SCALE1, MOD_GATE1, MOD_SHIFT2, MOD_SCALE2, MOD_GATE2 = range(N_MOD)


def _params(*semantics):
    return pltpu.CompilerParams(dimension_semantics=semantics, vmem_limit_bytes=VMEM_LIMIT_BYTES)


def _rms(x, g):
    return x * lax.rsqrt(jnp.mean(x * x, axis=-1, keepdims=True) + EPS) * g


def _sigmoid(x):
    return 0.5 * jnp.tanh(0.5 * x) + 0.5


def _row_tile(m, cap):
    if m <= cap:
        return m
    t = cap - cap % LANES
    while m % t:
        t -= LANES
    assert t > 0
    return t


def _mod_kernel(c_ref, w_ref, b_ref, o_ref):
    c = c_ref[...]
    s = (c * jax.nn.sigmoid(c)).astype(BF16)
    o_ref[...] = jnp.dot(s, w_ref[...].astype(BF16), preferred_element_type=F32) + b_ref[...]


def _modulation(cond, w_mod, b_mod):
    nc, d = cond.shape
    depth, _, n = w_mod.shape
    tn = _row_tile(n, 1024)
    out = pl.pallas_call(
        _mod_kernel,
        grid=(depth, n // tn),
        in_specs=[
            pl.BlockSpec((nc, d), lambda l, j: (0, 0)),
            pl.BlockSpec((None, d, tn), lambda l, j: (l, 0, j)),
            pl.BlockSpec((None, 1, tn), lambda l, j: (l, 0, j)),
        ],
        out_specs=pl.BlockSpec((None, nc, tn), lambda l, j: (l, 0, j)),
        out_shape=jax.ShapeDtypeStruct((depth, nc, n), F32),
        compiler_params=_params("parallel", "parallel"),
        name="modulation",
    )(cond, w_mod, b_mod.reshape(depth, 1, n))
    return out.reshape(depth, nc, N_MOD, d)


def _prenorm_mm_kernel(x_ref, g_ref, mod_ref, w_ref, o_ref, h_ref, *, shift_row, scale_row, relu2):
    @pl.when(pl.program_id(1) == 0)
    def _():
        gain = g_ref[...] * (1.0 + mod_ref[scale_row:scale_row + 1, :])
        shift = mod_ref[shift_row:shift_row + 1, :]

        def chunk(c, carry):
            rows = pl.ds(pl.multiple_of(c * ROW_CHUNK, ROW_CHUNK), ROW_CHUNK)
            x = x_ref[rows, :]
            inv = lax.rsqrt(jnp.mean(x * x, axis=-1, keepdims=True) + EPS)
            h_ref[rows, :] = (x * inv * gain + shift).astype(BF16)
            return carry

        lax.fori_loop(0, x_ref.shape[0] // ROW_CHUNK, chunk, 0, unroll=NORM_UNROLL)

    acc = jnp.dot(h_ref[...], w_ref[...], preferred_element_type=F32)
    if relu2:
        acc = jnp.square(jnp.maximum(acc, 0.0))
    o_ref[...] = acc.astype(o_ref.dtype)


def _prenorm_matmul(x, g, mod, w, *, rows_per_cond, shift_row, scale_row, relu2, name, tn_cap=1024,
                    w_col_block=lambda j: j):
    m, d = x.shape
    n = w.shape[1]
    tm = _row_tile(rows_per_cond, 1024)
    tn = _row_tile(n, tn_cap)
    assert rows_per_cond % tm == 0
    return pl.pallas_call(
        functools.partial(_prenorm_mm_kernel, shift_row=shift_row, scale_row=scale_row, relu2=relu2),
        grid=(m // tm, n // tn),
        in_specs=[
            pl.BlockSpec((tm, d), lambda i, j: (i, 0)),
            pl.BlockSpec((1, d), lambda i, j: (0, 0)),
            pl.BlockSpec((None, N_MOD, d), lambda i, j: ((i * tm) // rows_per_cond, 0, 0)),
            pl.BlockSpec((d, tn), lambda i, j: (0, w_col_block(j))),
        ],
        out_specs=pl.BlockSpec((tm, tn), lambda i, j: (i, j)),
        out_shape=jax.ShapeDtypeStruct((m, n), BF16),
        scratch_shapes=[pltpu.VMEM((tm, d), BF16)],
        compiler_params=_params("parallel", "arbitrary"),
        name=name,
    )(x, g.reshape(1, d), mod, w)


def _mm_norm_res_kernel(a_ref, w_ref, g_ref, mod_ref, x_ref, o_ref, *acc_refs, gate_row, nk):
    def product():
        return jnp.dot(a_ref[...], w_ref[...], preferred_element_type=F32)

    def finish(y_ref):
        gain = mod_ref[gate_row:gate_row + 1, :] * g_ref[...]

        def chunk(c, carry):
            rows = pl.ds(pl.multiple_of(c * ROW_CHUNK, ROW_CHUNK), ROW_CHUNK)
            y = y_ref[rows, :]
            inv = lax.rsqrt(jnp.mean(y * y, axis=-1, keepdims=True) + EPS)
            o_ref[rows, :] = x_ref[rows, :] + y * inv * gain
            return carry

        lax.fori_loop(0, o_ref.shape[0] // ROW_CHUNK, chunk, 0, unroll=NORM_UNROLL)

    if nk == 1:
        o_ref[...] = product()
        finish(o_ref)
        return
    acc_ref, = acc_refs
    k = pl.program_id(1)

    @pl.when(k == 0)
    def _():
        acc_ref[...] = product()

    @pl.when(k > 0)
    def _():
        acc_ref[...] += product()

    @pl.when(k == nk - 1)
    def _():
        finish(acc_ref)


def _matmul_norm_residual(a, w, g, mod, x, *, rows_per_cond, gate_row, name):
    m, kdim = a.shape
    d = w.shape[1]
    tm = _row_tile(rows_per_cond, 512)
    tk = _row_tile(kdim, 2048)
    nk = kdim // tk
    assert rows_per_cond % tm == 0
    return pl.pallas_call(
        functools.partial(_mm_norm_res_kernel, gate_row=gate_row, nk=nk),
        grid=(m // tm, nk),
        in_specs=[
            pl.BlockSpec((tm, tk), lambda i, k: (i, k)),
            pl.BlockSpec((tk, d), lambda i, k: (k, 0)),
            pl.BlockSpec((1, d), lambda i, k: (0, 0)),
            pl.BlockSpec((None, N_MOD, d), lambda i, k: ((i * tm) // rows_per_cond, 0, 0)),
            pl.BlockSpec((tm, d), lambda i, k: (i, 0)),
        ],
        out_specs=pl.BlockSpec((tm, d), lambda i, k: (i, 0)),
        out_shape=jax.ShapeDtypeStruct((m, d), F32),
        scratch_shapes=[pltpu.VMEM((tm, d), F32)] if nk > 1 else [],
        compiler_params=_params("parallel", "arbitrary"),
        name=name,
    )(a, w, g.reshape(1, d), mod, x)


def _merge_kernel(a0_ref, a1_ref, a2_ref, w0_ref, w1_ref, w2_ref, g0_ref, g1_ref, g2_ref, o_ref):
    acc = _sigmoid(g0_ref[...].astype(F32)) * jnp.dot(a0_ref[...], w0_ref[...], preferred_element_type=F32)
    acc += _sigmoid(g1_ref[...].astype(F32)) * jnp.dot(a1_ref[...], w1_ref[...], preferred_element_type=F32)
    acc += _sigmoid(g2_ref[...].astype(F32)) * jnp.dot(a2_ref[...], w2_ref[...], preferred_element_type=F32)
    o_ref[...] = acc.astype(o_ref.dtype)


def _merge(branches, weights, z, gate_col0, name):
    m, d = branches[0].shape
    tm = _row_tile(m, 512)
    tn = _row_tile(d, 512)
    assert gate_col0 % tn == 0 and d % tn == 0
    a_spec = pl.BlockSpec((tm, d), lambda i, j: (i, 0))
    w_spec = pl.BlockSpec((d, tn), lambda i, j: (0, j))

    def gate_spec(b):
        off = (gate_col0 + b * d) // tn
        return pl.BlockSpec((tm, tn), lambda i, j: (i, off + j))

    return pl.pallas_call(
        _merge_kernel,
        grid=(m // tm, d // tn),
        in_specs=[a_spec] * N_BRANCH + [w_spec] * N_BRANCH + [gate_spec(b) for b in range(N_BRANCH)],
        out_specs=pl.BlockSpec((tm, tn), lambda i, j: (i, j)),
        out_shape=jax.ShapeDtypeStruct((m, d), BF16),
        compiler_params=_params("parallel", "arbitrary"),
        name=name,
    )(*branches, *weights, z, z, z)


def _rope(x, c, s):
    lane = lax.broadcasted_iota(jnp.int32, x.shape, 1)
    partner = jnp.where((lane % (2 * ROPE_FREQS)) < ROPE_FREQS,
                        pltpu.roll(x, HEAD_DIM - ROPE_FREQS, 1),
                        pltpu.roll(x, ROPE_FREQS, 1))
    return x * c + partner * s


def _attn_kernel(*refs, has_ctx, t, tq, q_group, unit_rows, softmax_rows):
    if has_ctx:
        (zq_ref, zk_ref, zv_ref, gq_ref, gk_ref, cq_ref, sq_ref, ck_ref, sk_ref, pk_ref, pv_ref,
         o_ref, kt_ref, v_ref, s_ref, p_ref) = refs
    else:
        zq_ref, zk_ref, zv_ref, gq_ref, gk_ref, o_ref, ko_ref, vo_ref, kt_ref, v_ref, s_ref, p_ref = refs

    @pl.when(pl.program_id(2) == 0)
    def _():
        k = _rms(zk_ref[...].astype(F32), gk_ref[...])
        if has_ctx:
            k = _rope(k, ck_ref[...], sk_ref[...])
            kt_ref[:, 0:t] = k.T.astype(BF16)
            kt_ref[:, t:] = pk_ref[...].T.astype(BF16)
            v_ref[0:t, :] = zv_ref[...]
            v_ref[t:, :] = pv_ref[...].astype(BF16)
        else:
            ko_ref[...] = k
            vo_ref[...] = zv_ref[...].astype(F32)
            kt_ref[...] = k.T.astype(BF16)
            v_ref[...] = zv_ref[...]

    q_scale = LOG2_E * HEAD_DIM ** -0.5

    units = [(g, r0) for r0 in range(0, tq, unit_rows) for g in range(q_group)]

    def scores(u):
        g, r0 = units[u]
        rows = slice(r0, r0 + unit_rows)
        q = _rms(zq_ref[rows, g * HEAD_DIM:(g + 1) * HEAD_DIM].astype(F32), gq_ref[...])
        if has_ctx:
            q = _rope(q, cq_ref[rows, :], sq_ref[rows, :])
        s_ref[u] = jnp.dot((q * q_scale).astype(BF16), kt_ref[...], preferred_element_type=F32)

    def exponentials(u):
        groups = [slice(r0, r0 + softmax_rows) for r0 in range(0, unit_rows, softmax_rows)]
        maxima = [jnp.max(s_ref[u, rows, :], axis=-1, keepdims=True) for rows in groups]
        sums = []
        for rows, m in zip(groups, maxima):
            p = jnp.exp2(s_ref[u, rows, :] - m)
            p_ref[u, rows, :] = p.astype(BF16)
            sums.append(jnp.sum(p, axis=-1, keepdims=True))
        return jnp.concatenate(sums, axis=0)

    def weighted_values(u, row_sums):
        g, r0 = units[u]
        o = jnp.dot(p_ref[u], v_ref[...], preferred_element_type=F32) * (1.0 / row_sums)
        o_ref[r0:r0 + unit_rows, g * HEAD_DIM:(g + 1) * HEAD_DIM] = o.astype(o_ref.dtype)

    scores(0)
    for u in range(len(units)):
        if u + 1 < len(units):
            scores(u + 1)
        weighted_values(u, exponentials(u))


def _attention(z, g_q, g_k, *, n_seq, t, d, kv_w, k_col, v_col, rope=None, past_k=None, past_v=None, layer=None, name):
    has_ctx = rope is not None
    n_kv = kv_w // HEAD_DIM
    q_group = d // kv_w
    qw = q_group * HEAD_DIM
    past = past_k.shape[2] if has_ctx else 0
    tk = t + past
    tq = _row_tile(t, 256)
    nq = t // tq
    unit_rows = _row_tile(tq, 128)
    n_units = q_group * (tq // unit_rows)
    softmax_rows = min(unit_rows, max(BF16_ROWS, (VREG_BUDGET_ELEMS // tk) // BF16_ROWS * BF16_ROWS))
    assert unit_rows % softmax_rows == 0

    in_specs = [
        pl.BlockSpec((tq, qw), lambda b, h, i: (b * nq + i, h)),
        pl.BlockSpec((t, HEAD_DIM), lambda b, h, i: (b, k_col // HEAD_DIM + h)),
        pl.BlockSpec((t, HEAD_DIM), lambda b, h, i: (b, v_col // HEAD_DIM + h)),
        pl.BlockSpec((1, HEAD_DIM), lambda b, h, i: (0, 0)),
        pl.BlockSpec((1, HEAD_DIM), lambda b, h, i: (0, 0)),
    ]
    args = [z, z, z, g_q.reshape(1, HEAD_DIM), g_k.reshape(1, HEAD_DIM)]
    o_spec = pl.BlockSpec((tq, qw), lambda b, h, i: (b * nq + i, h))
    o_shape = jax.ShapeDtypeStruct((n_seq * t, d), BF16)
    if has_ctx:
        cos, sin = rope
        in_specs += [
            pl.BlockSpec((tq, HEAD_DIM), lambda b, h, i: (i, 0)),
            pl.BlockSpec((tq, HEAD_DIM), lambda b, h, i: (i, 0)),
            pl.BlockSpec((t, HEAD_DIM), lambda b, h, i: (0, 0)),
            pl.BlockSpec((t, HEAD_DIM), lambda b, h, i: (0, 0)),
            pl.BlockSpec((None, None, past, HEAD_DIM), lambda b, h, i: (b, layer, 0, h)),
            pl.BlockSpec((None, None, past, HEAD_DIM), lambda b, h, i: (b, layer, 0, h)),
        ]
        args += [cos, sin, cos, sin, past_k, past_v]
        out_specs = o_spec
        out_shape = o_shape
    else:
        kv_spec = pl.BlockSpec((None, t, HEAD_DIM), lambda b, h, i: (b, 0, h))
        kv_shape = jax.ShapeDtypeStruct((n_seq, t, kv_w), F32)
        out_specs = [o_spec, kv_spec, kv_spec]
        out_shape = [o_shape, kv_shape, kv_shape]

    return pl.pallas_call(
        functools.partial(_attn_kernel, has_ctx=has_ctx, t=t, tq=tq, q_group=q_group, unit_rows=unit_rows,
                          softmax_rows=softmax_rows),
        grid=(n_seq, n_kv, nq),
        in_specs=in_specs,
        out_specs=out_specs,
        out_shape=out_shape,
        scratch_shapes=[
            pltpu.VMEM((HEAD_DIM, tk), BF16),
            pltpu.VMEM((tk, HEAD_DIM), BF16),
            pltpu.VMEM((n_units, unit_rows, tk), F32),
            pltpu.VMEM((n_units, unit_rows, tk), BF16),
        ],
        compiler_params=_params("parallel", "parallel", "arbitrary"),
        name=name,
    )(*args)


def _lru_kernel(zx_ref, zg_ref, cw_ref, cb_ref, wa_ref, wx_ref, ba_ref, bx_ref, lam_ref, h0_ref,
                o_ref, hfin_ref, xpad_ref, xc_ref, hf_ref, *, t, tc):
    cw = zx_ref.shape[1]
    n_lane_blocks = cw // LANES
    n_chunks = t // tc
    pad = SUBLANES
    groups = tc // SUBLANES

    xpad_ref[0:pad, :] = jnp.zeros((pad, cw), F32)
    xpad_ref[t + pad:t + 2 * pad, :] = jnp.zeros((pad, cw), F32)

    def copy_chunk(c, carry):
        r = pl.multiple_of(c * tc, tc)
        xpad_ref[pl.ds(r + pad, tc), :] = zx_ref[pl.ds(r, tc), :].astype(F32)
        return carry

    lax.fori_loop(0, n_chunks, copy_chunk, 0)

    def conv_chunk(c, carry):
        r = pl.multiple_of(c * tc, tc)
        xe = xpad_ref[pl.ds(r, tc + 2 * pad), :]
        n = tc + 2 * pad
        y = cb_ref[...] + pltpu.roll(xe, 2, 0)[pad:pad + tc] * cw_ref[0:1, :]
        y = y + pltpu.roll(xe, 1, 0)[pad:pad + tc] * cw_ref[1:2, :]
        y = y + xe[pad:pad + tc] * cw_ref[2:3, :]
        y = y + pltpu.roll(xe, n - 1, 0)[pad:pad + tc] * cw_ref[3:4, :]
        xc_ref[pl.ds(r, tc), :] = y
        return carry

    lax.fori_loop(0, n_chunks, conv_chunk, 0)

    row_in_group = lax.broadcasted_iota(jnp.int32, (groups, SUBLANES, LANES), 1)

    half_k = [[(-0.5 * LRU_C * LOG2_E) * jax.nn.softplus(-lam_ref[direction:direction + 1, nb * LANES:(nb + 1) * LANES])
               for nb in range(n_lane_blocks)] for direction in range(2)]

    def gates(xn, direction, nb):
        lanes = slice(nb * LANES, (nb + 1) * LANES)
        xb = xn.astype(BF16)
        tanh_r = jnp.tanh(0.5 * (jnp.dot(xb, wa_ref[direction, nb], preferred_element_type=F32)
                                 + ba_ref[direction:direction + 1, lanes]))
        i = _sigmoid(jnp.dot(xb, wx_ref[direction, nb], preferred_element_type=F32)
                     + bx_ref[direction:direction + 1, lanes])
        a = jnp.exp2(half_k[direction][nb] * tanh_r + half_k[direction][nb])
        b = jnp.sqrt(jnp.maximum(1.0 - a * a, 0.0)) * (i * xn)
        return a, b

    def group_scan(a, b, reverse):
        a = a.reshape(groups, SUBLANES, LANES)
        b = b.reshape(groups, SUBLANES, LANES)
        for dist in (1, 2, 4):
            if reverse:
                edge = row_in_group >= SUBLANES - dist
                shift = SUBLANES - dist
            else:
                edge = row_in_group < dist
                shift = dist
            a_prev = jnp.where(edge, 1.0, pltpu.roll(a, shift, 1))
            b_prev = jnp.where(edge, 0.0, pltpu.roll(b, shift, 1))
            b = a * b_prev + b
            a = a * a_prev
        return a, b

    def forward_chunk(c, carry):
        r = pl.multiple_of(c * tc, tc)
        new_carry = []
        for nb in range(n_lane_blocks):
            lanes = slice(nb * LANES, (nb + 1) * LANES)
            a, b = gates(xc_ref[pl.ds(r, tc), lanes], 0, nb)
            a, b = group_scan(a, b, reverse=False)
            h_prev = carry[nb]
            for v in range(groups):
                h = a[v] * h_prev + b[v]
                hf_ref[pl.ds(r + v * SUBLANES, SUBLANES), lanes] = h
                h_prev = jnp.broadcast_to(h[SUBLANES - 1:SUBLANES], (SUBLANES, LANES))
            new_carry.append(h_prev)
        return tuple(new_carry)

    h0f = tuple(jnp.broadcast_to(h0_ref[0:1, nb * LANES:(nb + 1) * LANES], (SUBLANES, LANES))
                for nb in range(n_lane_blocks))
    hf_last = lax.fori_loop(0, n_chunks, forward_chunk, h0f)

    def backward_chunk(c, carry):
        r = pl.multiple_of((n_chunks - 1 - c) * tc, tc)
        new_carry = []
        for nb in range(n_lane_blocks):
            lanes = slice(nb * LANES, (nb + 1) * LANES)
            a, b = gates(xc_ref[pl.ds(r, tc), lanes], 1, nb)
            a, b = group_scan(a, b, reverse=True)
            h_prev = carry[nb]
            for v in reversed(range(groups)):
                row0 = r + v * SUBLANES
                h = a[v] * h_prev + b[v]
                gate = jax.nn.gelu(zg_ref[pl.ds(row0, SUBLANES), lanes].astype(F32))
                o_ref[pl.ds(row0, SUBLANES), lanes] = ((hf_ref[pl.ds(row0, SUBLANES), lanes] + h) * gate).astype(o_ref.dtype)
                h_prev = jnp.broadcast_to(h[0:1], (SUBLANES, LANES))
            new_carry.append(h_prev)
        return tuple(new_carry)

    h0b = tuple(jnp.broadcast_to(h0_ref[1:2, nb * LANES:(nb + 1) * LANES], (SUBLANES, LANES))
                for nb in range(n_lane_blocks))
    hb_first = lax.fori_loop(0, n_chunks, backward_chunk, h0b)

    for nb in range(n_lane_blocks):
        lanes = slice(nb * LANES, (nb + 1) * LANES)
        hfin_ref[0:1, lanes] = hf_last[nb][0:1]
        hfin_ref[1:2, lanes] = hb_first[nb][0:1]


def _rglru(z, conv_w, conv_b, wa, wx, ba, bx, lam, h0, *, n_seq, t, width, x_col, g_col, name):
    cw = _row_tile(width, 512)
    tc = _row_tile(t, 128)
    lb = cw // LANES
    assert wa.shape[-1] == LANES and x_col % cw == 0 and g_col % cw == 0
    vec_spec = pl.BlockSpec((2, cw), lambda s, c: (0, c))
    w_spec = pl.BlockSpec((2, lb, LANES, LANES), lambda s, c: (0, c, 0, 0))
    return pl.pallas_call(
        functools.partial(_lru_kernel, t=t, tc=tc),
        grid=(n_seq, width // cw),
        in_specs=[
            pl.BlockSpec((t, cw), lambda s, c: (s, x_col // cw + c)),
            pl.BlockSpec((t, cw), lambda s, c: (s, g_col // cw + c)),
            pl.BlockSpec((CONV_W, cw), lambda s, c: (0, c)),
            pl.BlockSpec((1, cw), lambda s, c: (0, c)),
            w_spec, w_spec, vec_spec, vec_spec, vec_spec,
            pl.BlockSpec((None, 2, cw), lambda s, c: (s, 0, c)),
        ],
        out_specs=[
            pl.BlockSpec((t, cw), lambda s, c: (s, c)),
            pl.BlockSpec((None, 2, cw), lambda s, c: (s, 0, c)),
        ],
        out_shape=[
            jax.ShapeDtypeStruct((n_seq * t, width), BF16),
            jax.ShapeDtypeStruct((n_seq, 2, width), F32),
        ],
        scratch_shapes=[
            pltpu.VMEM((t + 2 * SUBLANES, cw), F32),
            pltpu.VMEM((t, cw), F32),
            pltpu.VMEM((t, cw), F32),
        ],
        compiler_params=_params("parallel", "parallel"),
        name=name,
    )(z, z, conv_w, conv_b.reshape(1, width), wa, wx, ba, bx, lam, h0)


def _cmix_kernel(zu_ref, zv_ref, g_ref, ws_ref, bs_ref, o_ref, v_ref, *, n_groups):
    rows, width = zu_ref.shape
    gw = width // n_groups
    def norm_chunk(c, carry):
        r = pl.ds(pl.multiple_of(c * ROW_CHUNK, ROW_CHUNK), ROW_CHUNK)
        v_ref[r, :] = _rms(jax.nn.gelu(zv_ref[r, :].astype(F32)), g_ref[...]).astype(BF16)
        return carry

    lax.fori_loop(0, rows // ROW_CHUNK, norm_chunk, 0, unroll=NORM_UNROLL)
    for ch in range(rows // CHUNK):
        r = slice(ch * CHUNK, (ch + 1) * CHUNK)
        for g in range(n_groups):
            cols = slice(g * gw, (g + 1) * gw)
            mixed = jnp.dot(ws_ref[g], v_ref[r, cols], preferred_element_type=F32) + bs_ref[g]
            o_ref[r, cols] = (jax.nn.gelu(zu_ref[r, cols].astype(F32)) * mixed).astype(o_ref.dtype)


def _chunk_mix(z, cm_g, ws, bs, *, width, u_col, v_col, name):
    m = z.shape[0]
    n_groups = ws.shape[0]
    tr = _row_tile(m, 2 * CHUNK)
    assert u_col % width == 0 and v_col % width == 0
    return pl.pallas_call(
        functools.partial(_cmix_kernel, n_groups=n_groups),
        grid=(m // tr,),
        in_specs=[
            pl.BlockSpec((tr, width), lambda i: (i, u_col // width)),
            pl.BlockSpec((tr, width), lambda i: (i, v_col // width)),
            pl.BlockSpec((1, width), lambda i: (0, 0)),
            pl.BlockSpec(ws.shape, lambda i: (0, 0, 0)),
            pl.BlockSpec(bs.shape, lambda i: (0, 0, 0)),
        ],
        out_specs=pl.BlockSpec((tr, width), lambda i: (i, 0)),
        out_shape=jax.ShapeDtypeStruct((m, width), BF16),
        scratch_shapes=[pltpu.VMEM((tr, width), BF16)],
        compiler_params=_params("parallel"),
        name=name,
    )(z, z, cm_g.reshape(1, width), ws, bs)


def _rope_tables(n_tokens):
    rows = n_tokens // GRID_W
    pos_row = jnp.repeat(jnp.arange(rows), GRID_W).astype(F32)
    pos_col = (jnp.arange(n_tokens) % GRID_W).astype(F32)
    inv = ROPE_BASE ** (-jnp.arange(ROPE_FREQS, dtype=F32) / ROPE_FREQS)
    cr, sr = jnp.cos(pos_row[:, None] * inv), jnp.sin(pos_row[:, None] * inv)
    cc, sc = jnp.cos(pos_col[:, None] * inv), jnp.sin(pos_col[:, None] * inv)
    return (jnp.concatenate([cr, cr, cc, cc], axis=-1), jnp.concatenate([-sr, sr, -sc, sc], axis=-1))


def _layer(x, mod, p, tag, *, n_seq, t, rows_per_cond, rope=None, past_k=None, past_v=None, layer=None, h0=None):
    d = x.shape[1]
    kv_w = p["kv_w"]
    col_lx, col_lg, col_cu, col_cv, col_g, col_k = d, 2 * d, 3 * d, 4 * d, 5 * d, 8 * d
    col_v = col_k + kv_w

    q_blocks, rest_blocks = d // kv_w, 7 * d // kv_w

    def w_col_block(j):
        return jnp.where(j < q_blocks, j, jnp.where(j < q_blocks + rest_blocks, j + 2, j - rest_blocks))

    z = _prenorm_matmul(x, p["g_pre_mix"], mod, p["w_in"], rows_per_cond=rows_per_cond,
                        shift_row=MOD_SHIFT1, scale_row=MOD_SCALE1, relu2=False, name=f"in_proj_{tag}",
                        tn_cap=kv_w, w_col_block=w_col_block)
    attn = _attention(z, p["g_q"], p["g_k"], n_seq=n_seq, t=t, d=d, kv_w=kv_w, k_col=col_k, v_col=col_v,
                      rope=rope, past_k=past_k, past_v=past_v, layer=layer, name=f"attention_{tag}")
    if rope is None:
        attn_o, k_new, v_new = attn
    else:
        attn_o, k_new, v_new = attn, None, None
    lru_o, h_fin = _rglru(z, p["conv_w"], p["conv_b"], p["lru_wa"], p["lru_wx"], p["lru_ba"], p["lru_bx"],
                          p["lru_lam"], h0, n_seq=n_seq, t=t, width=d, x_col=col_lx, g_col=col_lg,
                          name=f"rglru_{tag}")
    cm_o = _chunk_mix(z, p["cm_g"], p["cm_ws"], p["cm_bs"], width=d, u_col=col_cu, v_col=col_cv,
                      name=f"chunk_mix_{tag}")
    merged = _merge((attn_o, lru_o, cm_o), (p["w_attn_out"], p["w_lru_out"], p["w_cm_out"]), z, col_g,
                    name=f"merge_{tag}")
    x = _matmul_norm_residual(merged, p["w_out"], p["g_post_mix"], mod, x, rows_per_cond=rows_per_cond,
                              gate_row=MOD_GATE1, name=f"out_proj_{tag}")
    f = _prenorm_matmul(x, p["g_pre_ff"], mod, p["w_ff1"], rows_per_cond=rows_per_cond,
                        shift_row=MOD_SHIFT2, scale_row=MOD_SCALE2, relu2=True, name=f"ff1_{tag}")
    x = _matmul_norm_residual(f, p["w_ff2"], p["g_post_ff"], mod, x, rows_per_cond=rows_per_cond,
                              gate_row=MOD_GATE2, name=f"ff2_{tag}")
    return x, k_new, v_new, h_fin


def kernel(x_prompt, x_sample, cache_k, cache_v, state_lru, c, c_ctx, w_mod, b_mod, g_pre_mix, g_post_mix, g_pre_ff, g_post_ff, w_in, g_q, g_k, w_attn_out, conv_w, conv_b, lru_wa, lru_ba, lru_wx, lru_bx, lru_lam, w_lru_out, cm_g, cm_ws, cm_bs, w_cm_out, w_out, w_ff1, w_ff2):
    batch, seq, d = x_prompt.shape
    dec_batch, dec_seq, _ = x_sample.shape
    depth = w_in.shape[0]
    in_w = w_in.shape[2]
    kv_w = (in_w - 8 * d) // 2
    n_kv = kv_w // HEAD_DIM
    past = cache_k.shape[2]
    n_groups = cm_ws.shape[1]

    n_cond = 1 + dec_batch
    cond = jnp.concatenate([c_ctx[None, :], c], axis=0)
    cond = jnp.pad(cond, ((0, -n_cond % SUBLANES), (0, 0)))
    mod = _modulation(cond, w_mod, b_mod)

    w_in_b = w_in.astype(BF16)
    w_attn_out_b, w_lru_out_b, w_cm_out_b = w_attn_out.astype(BF16), w_lru_out.astype(BF16), w_cm_out.astype(BF16)
    w_out_b, w_ff1_b, w_ff2_b = w_out.astype(BF16), w_ff1.astype(BF16), w_ff2.astype(BF16)
    lru_wa_b, lru_wx_b, cm_ws_b = lru_wa.astype(BF16), lru_wx.astype(BF16), cm_ws.astype(BF16)
    cm_bs_b = jnp.broadcast_to(cm_bs[:, :, :, None], cm_bs.shape + (d // n_groups,))

    rope = _rope_tables(dec_seq)
    past_k = cache_k.reshape(dec_batch, depth, past, kv_w)
    past_v = cache_v.reshape(dec_batch, depth, past, kv_w)
    h0_prompt = jnp.zeros((batch, 2, d), F32)

    y_p = x_prompt.reshape(batch * seq, d)
    y_s = x_sample.reshape(dec_batch * dec_seq, d)
    new_k, new_v, new_s = [], [], []
    for l in range(depth):
        p = {
            "kv_w": kv_w, "g_pre_mix": g_pre_mix[l], "g_post_mix": g_post_mix[l], "g_pre_ff": g_pre_ff[l],
            "g_post_ff": g_post_ff[l], "w_in": w_in_b[l], "g_q": g_q[l], "g_k": g_k[l],
            "w_attn_out": w_attn_out_b[l], "conv_w": conv_w[l], "conv_b": conv_b[l],
            "lru_wa": lru_wa_b[l], "lru_ba": lru_ba[l], "lru_wx": lru_wx_b[l], "lru_bx": lru_bx[l],
            "lru_lam": lru_lam[l], "w_lru_out": w_lru_out_b[l], "cm_g": cm_g[l], "cm_ws": cm_ws_b[l],
            "cm_bs": cm_bs_b[l], "w_cm_out": w_cm_out_b[l], "w_out": w_out_b[l], "w_ff1": w_ff1_b[l],
            "w_ff2": w_ff2_b[l],
        }
        y_p, k_l, v_l, s_l = _layer(y_p, mod[l, 0:1], p, f"ctx{l}", n_seq=batch, t=seq,
                                    rows_per_cond=batch * seq, h0=h0_prompt)
        new_k.append(k_l.reshape(batch, seq, n_kv, HEAD_DIM))
        new_v.append(v_l.reshape(batch, seq, n_kv, HEAD_DIM))
        new_s.append(s_l)
        y_s, _, _, _ = _layer(y_s, mod[l, 1:1 + dec_batch], p, f"lat{l}", n_seq=dec_batch, t=dec_seq,
                              rows_per_cond=dec_seq, rope=rope, past_k=past_k, past_v=past_v, layer=l,
                              h0=state_lru[:, l])
    return (y_p.reshape(batch, seq, d), y_s.reshape(dec_batch, dec_seq, d),
            jnp.stack(new_k, axis=1), jnp.stack(new_v, axis=1), jnp.stack(new_s, axis=1))
```

```python
import functools

import jax
import jax.numpy as jnp
from jax import lax
from jax.experimental import pallas as pl
from jax.experimental.pallas import tpu as pltpu

F32 = jnp.float32
BF16 = jnp.bfloat16

EPS = 1e-6
HEAD_DIM = 128
GRID_W = 64
ROPE_BASE = 10000.0
ROPE_FREQS = HEAD_DIM // 4
CONV_W = 4
CONV_PAD_L = 2
LRU_C = 8.0
CHUNK = 128
N_MOD = 6
N_BRANCH = 3
LOG2_E = 1.4426950408889634

LANES = 128
SUBLANES = 8
VMEM_LIMIT_BYTES = 56 * 1024 * 1024
BF16_ROWS = 2 * SUBLANES
ROW_CHUNK = BF16_ROWS
NORM_UNROLL = 4
VREG_BUDGET_ELEMS = 32 * SUBLANES * LANES

MOD_SHIFT1, MOD_SCALE1, MOD_GATE1, MOD_SHIFT2, MOD_SCALE2, MOD_GATE2 = range(N_MOD)


def _params(*semantics):
    return pltpu.CompilerParams(dimension_semantics=semantics, vmem_limit_bytes=VMEM_LIMIT_BYTES)


def _rms(x, g):
    return x * lax.rsqrt(jnp.mean(x * x, axis=-1, keepdims=True) + EPS) * g


def _sigmoid(x):
    return 0.5 * jnp.tanh(0.5 * x) + 0.5


def _row_tile(m, cap):
    if m <= cap:
        return m
    t = cap - cap % LANES
    while m % t:
        t -= LANES
    assert t > 0
    return t


def _mod_kernel(c_ref, w_ref, b_ref, o_ref):
    c = c_ref[...]
    s = (c * jax.nn.sigmoid(c)).astype(BF16)
    o_ref[...] = jnp.dot(s, w_ref[...].astype(BF16), preferred_element_type=F32) + b_ref[...]


def _modulation(cond, w_mod, b_mod):
    nc, d = cond.shape
    depth, _, n = w_mod.shape
    tn = _row_tile(n, 1024)
    out = pl.pallas_call(
        _mod_kernel,
        grid=(depth, n // tn),
        in_specs=[
            pl.BlockSpec((nc, d), lambda l, j: (0, 0)),
            pl.BlockSpec((None, d, tn), lambda l, j: (l, 0, j)),
            pl.BlockSpec((None, 1, tn), lambda l, j: (l, 0, j)),
        ],
        out_specs=pl.BlockSpec((None, nc, tn), lambda l, j: (l, 0, j)),
        out_shape=jax.ShapeDtypeStruct((depth, nc, n), F32),
        compiler_params=_params("parallel", "parallel"),
        name="modulation",
    )(cond, w_mod, b_mod.reshape(depth, 1, n))
    return out.reshape(depth, nc, N_MOD, d)


def _prenorm_mm_kernel(x_ref, g_ref, mod_ref, w_ref, o_ref, h_ref, *, shift_row, scale_row, relu2):
    @pl.when(pl.program_id(1) == 0)
    def _():
        gain = g_ref[...] * (1.0 + mod_ref[scale_row:scale_row + 1, :])
        shift = mod_ref[shift_row:shift_row + 1, :]

        def chunk(c, carry):
            rows = pl.ds(pl.multiple_of(c * ROW_CHUNK, ROW_CHUNK), ROW_CHUNK)
            x = x_ref[rows, :]
            inv = lax.rsqrt(jnp.mean(x * x, axis=-1, keepdims=True) + EPS)
            h_ref[rows, :] = (x * inv * gain + shift).astype(BF16)
            return carry

        lax.fori_loop(0, x_ref.shape[0] // ROW_CHUNK, chunk, 0, unroll=NORM_UNROLL)

    acc = jnp.dot(h_ref[...], w_ref[...], preferred_element_type=F32)
    if relu2:
        acc = jnp.square(jnp.maximum(acc, 0.0))
    o_ref[...] = acc.astype(o_ref.dtype)


def _prenorm_matmul(x, g, mod, w, layer, *, rows_per_cond, shift_row, scale_row, relu2, name, tn=None,
                    w_col_block=lambda j: j):
    m, d = x.shape
    n = w.shape[2]
    tm = _row_tile(rows_per_cond, 1024)
    tn = tn or _row_tile(n, 1024)
    assert rows_per_cond % tm == 0 and n % tn == 0
    return pl.pallas_call(
        functools.partial(_prenorm_mm_kernel, shift_row=shift_row, scale_row=scale_row, relu2=relu2),
        grid=(m // tm, n // tn),
        in_specs=[
            pl.BlockSpec((tm, d), lambda i, j: (i, 0)),
            pl.BlockSpec((1, d), lambda i, j: (0, 0)),
            pl.BlockSpec((None, N_MOD, d), lambda i, j: ((i * tm) // rows_per_cond, 0, 0)),
            pl.BlockSpec((None, d, tn), lambda i, j: (layer, 0, w_col_block(j))),
        ],
        out_specs=pl.BlockSpec((tm, tn), lambda i, j: (i, j)),
        out_shape=jax.ShapeDtypeStruct((m, n), BF16),
        scratch_shapes=[pltpu.VMEM((tm, d), BF16)],
        compiler_params=_params("parallel", "arbitrary"),
        name=name,
    )(x, g.reshape(1, d), mod, w)


def _mm_norm_res_kernel(a_ref, w_ref, g_ref, mod_ref, x_ref, o_ref, *acc_refs, gate_row, nk):
    def product():
        return jnp.dot(a_ref[...], w_ref[...], preferred_element_type=F32)

    def finish(y_ref):
        gain = mod_ref[gate_row:gate_row + 1, :] * g_ref[...]

        def chunk(c, carry):
            rows = pl.ds(pl.multiple_of(c * ROW_CHUNK, ROW_CHUNK), ROW_CHUNK)
            y = y_ref[rows, :]
            inv = lax.rsqrt(jnp.mean(y * y, axis=-1, keepdims=True) + EPS)
            o_ref[rows, :] = x_ref[rows, :] + y * inv * gain
            return carry

        lax.fori_loop(0, o_ref.shape[0] // ROW_CHUNK, chunk, 0, unroll=NORM_UNROLL)

    if nk == 1:
        o_ref[...] = product()
        finish(o_ref)
        return
    acc_ref, = acc_refs
    k = pl.program_id(1)

    @pl.when(k == 0)
    def _():
        acc_ref[...] = product()

    @pl.when(k > 0)
    def _():
        acc_ref[...] += product()

    @pl.when(k == nk - 1)
    def _():
        finish(acc_ref)


def _matmul_norm_residual(a, w, layer, g, mod, x, *, rows_per_cond, gate_row, name):
    m, kdim = a.shape
    d = w.shape[2]
    tm = _row_tile(rows_per_cond, 512)
    tk = _row_tile(kdim, 2048)
    nk = kdim // tk
    assert rows_per_cond % tm == 0
    return pl.pallas_call(
        functools.partial(_mm_norm_res_kernel, gate_row=gate_row, nk=nk),
        grid=(m // tm, nk),
        in_specs=[
            pl.BlockSpec((tm, tk), lambda i, k: (i, k)),
            pl.BlockSpec((None, tk, d), lambda i, k: (layer, k, 0)),
            pl.BlockSpec((1, d), lambda i, k: (0, 0)),
            pl.BlockSpec((None, N_MOD, d), lambda i, k: ((i * tm) // rows_per_cond, 0, 0)),
            pl.BlockSpec((tm, d), lambda i, k: (i, 0)),
        ],
        out_specs=pl.BlockSpec((tm, d), lambda i, k: (i, 0)),
        out_shape=jax.ShapeDtypeStruct((m, d), F32),
        scratch_shapes=[pltpu.VMEM((tm, d), F32)] if nk > 1 else [],
        compiler_params=_params("parallel", "arbitrary"),
        name=name,
    )(a, w, g.reshape(1, d), mod, x)


def _merge_kernel(a0_ref, a1_ref, a2_ref, w0_ref, w1_ref, w2_ref, g0_ref, g1_ref, g2_ref, o_ref):
    acc = _sigmoid(g0_ref[...].astype(F32)) * jnp.dot(a0_ref[...], w0_ref[...], preferred_element_type=F32)
    acc += _sigmoid(g1_ref[...].astype(F32)) * jnp.dot(a1_ref[...], w1_ref[...], preferred_element_type=F32)
    acc += _sigmoid(g2_ref[...].astype(F32)) * jnp.dot(a2_ref[...], w2_ref[...], preferred_element_type=F32)
    o_ref[...] = acc.astype(o_ref.dtype)


def _merge(branches, weights, layer, z, gate_col0, name):
    m, d = branches[0].shape
    tm = _row_tile(m, 512)
    tn = _row_tile(d, 512)
    assert gate_col0 % tn == 0 and d % tn == 0
    a_spec = pl.BlockSpec((tm, d), lambda i, j: (i, 0))
    w_spec = pl.BlockSpec((None, d, tn), lambda i, j: (layer, 0, j))

    def gate_spec(b):
        off = (gate_col0 + b * d) // tn
        return pl.BlockSpec((tm, tn), lambda i, j: (i, off + j))

    return pl.pallas_call(
        _merge_kernel,
        grid=(m // tm, d // tn),
        in_specs=[a_spec] * N_BRANCH + [w_spec] * N_BRANCH + [gate_spec(b) for b in range(N_BRANCH)],
        out_specs=pl.BlockSpec((tm, tn), lambda i, j: (i, j)),
        out_shape=jax.ShapeDtypeStruct((m, d), BF16),
        compiler_params=_params("parallel", "arbitrary"),
        name=name,
    )(*branches, *weights, z, z, z)


def _rope(x, c, s):
    lane = lax.broadcasted_iota(jnp.int32, x.shape, 1)
    partner = jnp.where((lane % (2 * ROPE_FREQS)) < ROPE_FREQS,
                        pltpu.roll(x, HEAD_DIM - ROPE_FREQS, 1),
                        pltpu.roll(x, ROPE_FREQS, 1))
    return x * c + partner * s


def _attn_kernel(*refs, has_ctx, t, tq, q_group, unit_rows, softmax_rows):
    if has_ctx:
        (zq_ref, zk_ref, zv_ref, gq_ref, gk_ref, cq_ref, sq_ref, ck_ref, sk_ref, pk_ref, pv_ref,
         o_ref, kt_ref, v_ref, s0_ref, s1_ref, p0_ref, p1_ref) = refs
    else:
        (zq_ref, zk_ref, zv_ref, gq_ref, gk_ref, o_ref, ko_ref, vo_ref,
         kt_ref, v_ref, s0_ref, s1_ref, p0_ref, p1_ref) = refs
    s_refs, p_refs = (s0_ref, s1_ref), (p0_ref, p1_ref)

    @pl.when(pl.program_id(2) == 0)
    def _():
        k = _rms(zk_ref[...].astype(F32), gk_ref[...])
        v_ref[:, HEAD_DIM:] = jnp.ones((v_ref.shape[0], HEAD_DIM), BF16)
        if has_ctx:
            k = _rope(k, ck_ref[...], sk_ref[...])
            kt_ref[:, 0:t] = k.T.astype(BF16)
            kt_ref[:, t:] = pk_ref[...].T.astype(BF16)
            v_ref[0:t, 0:HEAD_DIM] = zv_ref[...]
            v_ref[t:, 0:HEAD_DIM] = pv_ref[...].astype(BF16)
        else:
            ko_ref[...] = k
            vo_ref[...] = zv_ref[...].astype(F32)
            kt_ref[...] = k.T.astype(BF16)
            v_ref[:, 0:HEAD_DIM] = zv_ref[...]

    q_scale = LOG2_E * HEAD_DIM ** -0.5

    units = [(g, r0) for r0 in range(0, tq, unit_rows) for g in range(q_group)]

    def scores(u):
        g, r0 = units[u]
        rows = slice(r0, r0 + unit_rows)
        q = _rms(zq_ref[rows, g * HEAD_DIM:(g + 1) * HEAD_DIM].astype(F32), gq_ref[...])
        if has_ctx:
            q = _rope(q, cq_ref[rows, :], sq_ref[rows, :])
        s_refs[u % 2][...] = jnp.dot((q * q_scale).astype(BF16), kt_ref[...], preferred_element_type=F32)

    def exponentials(u):
        s_ref, p_ref = s_refs[u % 2], p_refs[u % 2]
        groups = [slice(r0, r0 + softmax_rows) for r0 in range(0, unit_rows, softmax_rows)]
        maxima = [jnp.max(s_ref[rows, :], axis=-1, keepdims=True) for rows in groups]
        for rows, m in zip(groups, maxima):
            p_ref[rows, :] = jnp.exp2(s_ref[rows, :] - m).astype(BF16)

    def weighted_values(u):
        g, r0 = units[u]
        o = jnp.dot(p_refs[u % 2][...], v_ref[...], preferred_element_type=F32)
        o_ref[r0:r0 + unit_rows, g * HEAD_DIM:(g + 1) * HEAD_DIM] = (
            o[:, :HEAD_DIM] * (1.0 / o[:, HEAD_DIM:])).astype(o_ref.dtype)

    scores(0)
    for u in range(len(units)):
        if u + 1 < len(units):
            scores(u + 1)
        exponentials(u)
        weighted_values(u)


def _attention(z, g_q, g_k, *, n_seq, t, d, kv_w, k_col, v_col, rope=None, past_k=None, past_v=None, layer=None, name):
    has_ctx = rope is not None
    n_kv = kv_w // HEAD_DIM
    q_group = d // kv_w
    qw = q_group * HEAD_DIM
    past = past_k.shape[2] if has_ctx else 0
    tk = t + past
    tq = _row_tile(t, 256)
    nq = t // tq
    unit_rows = _row_tile(tq, 128)
    softmax_rows = min(unit_rows, max(BF16_ROWS, (VREG_BUDGET_ELEMS // tk) // BF16_ROWS * BF16_ROWS))
    assert unit_rows % softmax_rows == 0

    in_specs = [
        pl.BlockSpec((tq, qw), lambda b, h, i: (b * nq + i, h)),
        pl.BlockSpec((t, HEAD_DIM), lambda b, h, i: (b, k_col // HEAD_DIM + h)),
        pl.BlockSpec((t, HEAD_DIM), lambda b, h, i: (b, v_col // HEAD_DIM + h)),
        pl.BlockSpec((1, HEAD_DIM), lambda b, h, i: (0, 0)),
        pl.BlockSpec((1, HEAD_DIM), lambda b, h, i: (0, 0)),
    ]
    args = [z, z, z, g_q.reshape(1, HEAD_DIM), g_k.reshape(1, HEAD_DIM)]
    o_spec = pl.BlockSpec((tq, qw), lambda b, h, i: (b * nq + i, h))
    o_shape = jax.ShapeDtypeStruct((n_seq * t, d), BF16)
    if has_ctx:
        cos, sin = rope
        in_specs += [
            pl.BlockSpec((tq, HEAD_DIM), lambda b, h, i: (i, 0)),
            pl.BlockSpec((tq, HEAD_DIM), lambda b, h, i: (i, 0)),
            pl.BlockSpec((t, HEAD_DIM), lambda b, h, i: (0, 0)),
            pl.BlockSpec((t, HEAD_DIM), lambda b, h, i: (0, 0)),
            pl.BlockSpec((None, None, past, HEAD_DIM), lambda b, h, i: (b, layer, 0, h)),
            pl.BlockSpec((None, None, past, HEAD_DIM), lambda b, h, i: (b, layer, 0, h)),
        ]
        args += [cos, sin, cos, sin, past_k, past_v]
        out_specs = o_spec
        out_shape = o_shape
    else:
        kv_spec = pl.BlockSpec((None, t, HEAD_DIM), lambda b, h, i: (b, 0, h))
        kv_shape = jax.ShapeDtypeStruct((n_seq, t, kv_w), F32)
        out_specs = [o_spec, kv_spec, kv_spec]
        out_shape = [o_shape, kv_shape, kv_shape]

    return pl.pallas_call(
        functools.partial(_attn_kernel, has_ctx=has_ctx, t=t, tq=tq, q_group=q_group, unit_rows=unit_rows,
                          softmax_rows=softmax_rows),
        grid=(n_seq, n_kv, nq),
        in_specs=in_specs,
        out_specs=out_specs,
        out_shape=out_shape,
        scratch_shapes=[
            pltpu.VMEM((HEAD_DIM, tk), BF16),
            pltpu.VMEM((tk, 2 * HEAD_DIM), BF16),
            pltpu.VMEM((unit_rows, tk), F32),
            pltpu.VMEM((unit_rows, tk), F32),
            pltpu.VMEM((unit_rows, tk), BF16),
            pltpu.VMEM((unit_rows, tk), BF16),
        ],
        compiler_params=_params("parallel", "parallel", "arbitrary"),
        name=name,
    )(*args)


def _lru_kernel(zx_ref, zg_ref, cw_ref, cb_ref, wa_ref, wx_ref, ba_ref, bx_ref, lam_ref, h0_ref,
                o_ref, hfin_ref, xpad_ref, xc_ref, hf_ref, *, t, tc):
    cw = zx_ref.shape[1]
    n_lane_blocks = cw // LANES
    n_chunks = t // tc
    pad = SUBLANES
    groups = tc // SUBLANES

    xpad_ref[0:pad, :] = jnp.zeros((pad, cw), F32)
    xpad_ref[t + pad:t + 2 * pad, :] = jnp.zeros((pad, cw), F32)

    def copy_chunk(c, carry):
        r = pl.multiple_of(c * tc, tc)
        xpad_ref[pl.ds(r + pad, tc), :] = zx_ref[pl.ds(r, tc), :].astype(F32)
        return carry

    lax.fori_loop(0, n_chunks, copy_chunk, 0)

    def conv_chunk(c, carry):
        r = pl.multiple_of(c * tc, tc)
        xe = xpad_ref[pl.ds(r, tc + 2 * pad), :]
        n = tc + 2 * pad
        y = cb_ref[...] + pltpu.roll(xe, 2, 0)[pad:pad + tc] * cw_ref[0:1, :]
        y = y + pltpu.roll(xe, 1, 0)[pad:pad + tc] * cw_ref[1:2, :]
        y = y + xe[pad:pad + tc] * cw_ref[2:3, :]
        y = y + pltpu.roll(xe, n - 1, 0)[pad:pad + tc] * cw_ref[3:4, :]
        xc_ref[pl.ds(r, tc), :] = y
        return carry

    lax.fori_loop(0, n_chunks, conv_chunk, 0)

    row_in_group = lax.broadcasted_iota(jnp.int32, (groups, SUBLANES, LANES), 1)

    half_k = [[(-0.5 * LRU_C * LOG2_E) * jax.nn.softplus(-lam_ref[direction:direction + 1, nb * LANES:(nb + 1) * LANES])
               for nb in range(n_lane_blocks)] for direction in range(2)]

    def gates(xn, direction, nb):
        lanes = slice(nb * LANES, (nb + 1) * LANES)
        xb = xn.astype(BF16)
        tanh_r = jnp.tanh(0.5 * (jnp.dot(xb, wa_ref[direction, nb], preferred_element_type=F32)
                                 + ba_ref[direction:direction + 1, lanes]))
        i = _sigmoid(jnp.dot(xb, wx_ref[direction, nb], preferred_element_type=F32)
                     + bx_ref[direction:direction + 1, lanes])
        a = jnp.exp2(half_k[direction][nb] * tanh_r + half_k[direction][nb])
        b = jnp.sqrt(jnp.maximum(1.0 - a * a, 0.0)) * (i * xn)
        return a, b

    def group_scan(a, b, reverse):
        a = a.reshape(groups, SUBLANES, LANES)
        b = b.reshape(groups, SUBLANES, LANES)
        for dist in (1, 2, 4):
            if reverse:
                edge = row_in_group >= SUBLANES - dist
                shift = SUBLANES - dist
            else:
                edge = row_in_group < dist
                shift = dist
            a_prev = jnp.where(edge, 1.0, pltpu.roll(a, shift, 1))
            b_prev = jnp.where(edge, 0.0, pltpu.roll(b, shift, 1))
            b = a * b_prev + b
            a = a * a_prev
        return a, b

    def forward_chunk(c, carry):
        r = pl.multiple_of(c * tc, tc)
        new_carry = []
        for nb in range(n_lane_blocks):
            lanes = slice(nb * LANES, (nb + 1) * LANES)
            a, b = gates(xc_ref[pl.ds(r, tc), lanes], 0, nb)
            a, b = group_scan(a, b, reverse=False)
            h_prev = carry[nb]
            for v in range(groups):
                h = a[v] * h_prev + b[v]
                hf_ref[pl.ds(r + v * SUBLANES, SUBLANES), lanes] = h
                h_prev = jnp.broadcast_to(h[SUBLANES - 1:SUBLANES], (SUBLANES, LANES))
            new_carry.append(h_prev)
        return tuple(new_carry)

    h0f = tuple(jnp.broadcast_to(h0_ref[0:1, nb * LANES:(nb + 1) * LANES], (SUBLANES, LANES))
                for nb in range(n_lane_blocks))
    hf_last = lax.fori_loop(0, n_chunks, forward_chunk, h0f)

    def backward_chunk(c, carry):
        r = pl.multiple_of((n_chunks - 1 - c) * tc, tc)
        new_carry = []
        for nb in range(n_lane_blocks):
            lanes = slice(nb * LANES, (nb + 1) * LANES)
            a, b = gates(xc_ref[pl.ds(r, tc), lanes], 1, nb)
            a, b = group_scan(a, b, reverse=True)
            h_prev = carry[nb]
            for v in reversed(range(groups)):
                row0 = r + v * SUBLANES
                h = a[v] * h_prev + b[v]
                gate = jax.nn.gelu(zg_ref[pl.ds(row0, SUBLANES), lanes].astype(F32))
                o_ref[pl.ds(row0, SUBLANES), lanes] = ((hf_ref[pl.ds(row0, SUBLANES), lanes] + h) * gate).astype(o_ref.dtype)
                h_prev = jnp.broadcast_to(h[0:1], (SUBLANES, LANES))
            new_carry.append(h_prev)
        return tuple(new_carry)

    h0b = tuple(jnp.broadcast_to(h0_ref[1:2, nb * LANES:(nb + 1) * LANES], (SUBLANES, LANES))
                for nb in range(n_lane_blocks))
    hb_first = lax.fori_loop(0, n_chunks, backward_chunk, h0b)

    for nb in range(n_lane_blocks):
        lanes = slice(nb * LANES, (nb + 1) * LANES)
        hfin_ref[0:1, lanes] = hf_last[nb][0:1]
        hfin_ref[1:2, lanes] = hb_first[nb][0:1]


def _rglru(z, conv_w, conv_b, wa, wx, ba, bx, lam, h0, *, n_seq, t, width, x_col, g_col, name):
    cw = _row_tile(width, 512)
    tc = _row_tile(t, 128)
    lb = cw // LANES
    assert wa.shape[-1] == LANES and x_col % cw == 0 and g_col % cw == 0
    vec_spec = pl.BlockSpec((2, cw), lambda s, c: (0, c))
    w_spec = pl.BlockSpec((2, lb, LANES, LANES), lambda s, c: (0, c, 0, 0))
    return pl.pallas_call(
        functools.partial(_lru_kernel, t=t, tc=tc),
        grid=(n_seq, width // cw),
        in_specs=[
            pl.BlockSpec((t, cw), lambda s, c: (s, x_col // cw + c)),
            pl.BlockSpec((t, cw), lambda s, c: (s, g_col // cw + c)),
            pl.BlockSpec((CONV_W, cw), lambda s, c: (0, c)),
            pl.BlockSpec((1, cw), lambda s, c: (0, c)),
            w_spec, w_spec, vec_spec, vec_spec, vec_spec,
            pl.BlockSpec((None, 2, cw), lambda s, c: (s, 0, c)),
        ],
        out_specs=[
            pl.BlockSpec((t, cw), lambda s, c: (s, c)),
            pl.BlockSpec((None, 2, cw), lambda s, c: (s, 0, c)),
        ],
        out_shape=[
            jax.ShapeDtypeStruct((n_seq * t, width), BF16),
            jax.ShapeDtypeStruct((n_seq, 2, width), F32),
        ],
        scratch_shapes=[
            pltpu.VMEM((t + 2 * SUBLANES, cw), F32),
            pltpu.VMEM((t, cw), F32),
            pltpu.VMEM((t, cw), F32),
        ],
        compiler_params=_params("parallel", "parallel"),
        name=name,
    )(z, z, conv_w, conv_b.reshape(1, width), wa, wx, ba, bx, lam, h0)


def _cmix_kernel(zu_ref, zv_ref, g_ref, ws_ref, bs_ref, o_ref, v_ref, *, n_groups):
    rows, width = zu_ref.shape
    gw = width // n_groups
    def norm_chunk(c, carry):
        r = pl.ds(pl.multiple_of(c * ROW_CHUNK, ROW_CHUNK), ROW_CHUNK)
        v_ref[r, :] = _rms(jax.nn.gelu(zv_ref[r, :].astype(F32)), g_ref[...]).astype(BF16)
        return carry

    lax.fori_loop(0, rows // ROW_CHUNK, norm_chunk, 0, unroll=NORM_UNROLL)
    for ch in range(rows // CHUNK):
        r = slice(ch * CHUNK, (ch + 1) * CHUNK)
        for g in range(n_groups):
            cols = slice(g * gw, (g + 1) * gw)
            mixed = jnp.dot(ws_ref[g], v_ref[r, cols], preferred_element_type=F32) + bs_ref[g]
            o_ref[r, cols] = (jax.nn.gelu(zu_ref[r, cols].astype(F32)) * mixed).astype(o_ref.dtype)


def _chunk_mix(z, cm_g, ws, bs, *, width, u_col, v_col, name):
    m = z.shape[0]
    n_groups = ws.shape[0]
    tr = _row_tile(m, 2 * CHUNK)
    assert u_col % width == 0 and v_col % width == 0
    return pl.pallas_call(
        functools.partial(_cmix_kernel, n_groups=n_groups),
        grid=(m // tr,),
        in_specs=[
            pl.BlockSpec((tr, width), lambda i: (i, u_col // width)),
            pl.BlockSpec((tr, width), lambda i: (i, v_col // width)),
            pl.BlockSpec((1, width), lambda i: (0, 0)),
            pl.BlockSpec(ws.shape, lambda i: (0, 0, 0)),
            pl.BlockSpec(bs.shape, lambda i: (0, 0, 0)),
        ],
        out_specs=pl.BlockSpec((tr, width), lambda i: (i, 0)),
        out_shape=jax.ShapeDtypeStruct((m, width), BF16),
        scratch_shapes=[pltpu.VMEM((tr, width), BF16)],
        compiler_params=_params("parallel"),
        name=name,
    )(z, z, cm_g.reshape(1, width), ws, bs)


def _rope_tables(n_tokens):
    rows = n_tokens // GRID_W
    pos_row = jnp.repeat(jnp.arange(rows), GRID_W).astype(F32)
    pos_col = (jnp.arange(n_tokens) % GRID_W).astype(F32)
    inv = ROPE_BASE ** (-jnp.arange(ROPE_FREQS, dtype=F32) / ROPE_FREQS)
    cr, sr = jnp.cos(pos_row[:, None] * inv), jnp.sin(pos_row[:, None] * inv)
    cc, sc = jnp.cos(pos_col[:, None] * inv), jnp.sin(pos_col[:, None] * inv)
    return (jnp.concatenate([cr, cr, cc, cc], axis=-1), jnp.concatenate([-sr, sr, -sc, sc], axis=-1))


def _layer(x, mod, p, tag, *, n_seq, t, rows_per_cond, rope=None, past_k=None, past_v=None, layer=None, h0=None):
    d = x.shape[1]
    kv_w = p["kv_w"]
    col_lx, col_lg, col_cu, col_cv, col_g, col_k = d, 2 * d, 3 * d, 4 * d, 5 * d, 8 * d
    col_v = col_k + kv_w

    tn = 2 * kv_w
    assert d % tn == 0
    q_blocks, rest_blocks = d // tn, 7 * d // tn

    def w_col_block(j):
        return jnp.where(j < q_blocks, j, jnp.where(j < q_blocks + rest_blocks, j + 1, q_blocks))

    li = p["layer"]
    z = _prenorm_matmul(x, p["g_pre_mix"], mod, p["w_in"], li, rows_per_cond=rows_per_cond,
                        shift_row=MOD_SHIFT1, scale_row=MOD_SCALE1, relu2=False, name=f"in_proj_{tag}",
                        tn=tn, w_col_block=w_col_block)
    attn = _attention(z, p["g_q"], p["g_k"], n_seq=n_seq, t=t, d=d, kv_w=kv_w, k_col=col_k, v_col=col_v,
                      rope=rope, past_k=past_k, past_v=past_v, layer=layer, name=f"attention_{tag}")
    if rope is None:
        attn_o, k_new, v_new = attn
    else:
        attn_o, k_new, v_new = attn, None, None
    lru_o, h_fin = _rglru(z, p["conv_w"], p["conv_b"], p["lru_wa"], p["lru_wx"], p["lru_ba"], p["lru_bx"],
                          p["lru_lam"], h0, n_seq=n_seq, t=t, width=d, x_col=col_lx, g_col=col_lg,
                          name=f"rglru_{tag}")
    cm_o = _chunk_mix(z, p["cm_g"], p["cm_ws"], p["cm_bs"], width=d, u_col=col_cu, v_col=col_cv,
                      name=f"chunk_mix_{tag}")
    merged = _merge((attn_o, lru_o, cm_o), (p["w_attn_out"], p["w_lru_out"], p["w_cm_out"]), li, z, col_g,
                    name=f"merge_{tag}")
    x = _matmul_norm_residual(merged, p["w_out"], li, p["g_post_mix"], mod, x, rows_per_cond=rows_per_cond,
                              gate_row=MOD_GATE1, name=f"out_proj_{tag}")
    f = _prenorm_matmul(x, p["g_pre_ff"], mod, p["w_ff1"], li, rows_per_cond=rows_per_cond,
                        shift_row=MOD_SHIFT2, scale_row=MOD_SCALE2, relu2=True, name=f"ff1_{tag}")
    x = _matmul_norm_residual(f, p["w_ff2"], li, p["g_post_ff"], mod, x, rows_per_cond=rows_per_cond,
                              gate_row=MOD_GATE2, name=f"ff2_{tag}")
    return x, k_new, v_new, h_fin


def kernel(x_prompt, x_sample, cache_k, cache_v, state_lru, c, c_ctx, w_mod, b_mod, g_pre_mix, g_post_mix, g_pre_ff, g_post_ff, w_in, g_q, g_k, w_attn_out, conv_w, conv_b, lru_wa, lru_ba, lru_wx, lru_bx, lru_lam, w_lru_out, cm_g, cm_ws, cm_bs, w_cm_out, w_out, w_ff1, w_ff2):
    batch, seq, d = x_prompt.shape
    dec_batch, dec_seq, _ = x_sample.shape
    depth = w_in.shape[0]
    in_w = w_in.shape[2]
    kv_w = (in_w - 8 * d) // 2
    n_kv = kv_w // HEAD_DIM
    past = cache_k.shape[2]
    n_groups = cm_ws.shape[1]

    n_cond = 1 + dec_batch
    cond = jnp.concatenate([c_ctx[None, :], c], axis=0)
    cond = jnp.pad(cond, ((0, -n_cond % SUBLANES), (0, 0)))
    mod = _modulation(cond, w_mod, b_mod)

    w_in_b = w_in.astype(BF16)
    w_attn_out_b, w_lru_out_b, w_cm_out_b = w_attn_out.astype(BF16), w_lru_out.astype(BF16), w_cm_out.astype(BF16)
    w_out_b, w_ff1_b, w_ff2_b = w_out.astype(BF16), w_ff1.astype(BF16), w_ff2.astype(BF16)
    lru_wa_b, lru_wx_b, cm_ws_b = lru_wa.astype(BF16), lru_wx.astype(BF16), cm_ws.astype(BF16)
    cm_bs_b = jnp.broadcast_to(cm_bs[:, :, :, None], cm_bs.shape + (d // n_groups,))

    rope = _rope_tables(dec_seq)
    past_k = cache_k.reshape(dec_batch, depth, past, kv_w)
    past_v = cache_v.reshape(dec_batch, depth, past, kv_w)
    h0_prompt = jnp.zeros((batch, 2, d), F32)

    y_p = x_prompt.reshape(batch * seq, d)
    y_s = x_sample.reshape(dec_batch * dec_seq, d)
    new_k, new_v, new_s = [], [], []
    for l in range(depth):
        p = {
            "kv_w": kv_w, "layer": l, "g_pre_mix": g_pre_mix[l], "g_post_mix": g_post_mix[l],
            "g_pre_ff": g_pre_ff[l], "g_post_ff": g_post_ff[l], "w_in": w_in_b, "g_q": g_q[l], "g_k": g_k[l],
            "w_attn_out": w_attn_out_b, "conv_w": conv_w[l], "conv_b": conv_b[l],
            "lru_wa": lru_wa_b[l], "lru_ba": lru_ba[l], "lru_wx": lru_wx_b[l], "lru_bx": lru_bx[l],
            "lru_lam": lru_lam[l], "w_lru_out": w_lru_out_b, "cm_g": cm_g[l], "cm_ws": cm_ws_b[l],
            "cm_bs": cm_bs_b[l], "w_cm_out": w_cm_out_b, "w_out": w_out_b, "w_ff1": w_ff1_b,
            "w_ff2": w_ff2_b,
        }
        y_p, k_l, v_l, s_l = _layer(y_p, mod[l, 0:1], p, f"ctx{l}", n_seq=batch, t=seq,
                                    rows_per_cond=batch * seq, h0=h0_prompt)
        new_k.append(k_l.reshape(batch, seq, n_kv, HEAD_DIM))
        new_v.append(v_l.reshape(batch, seq, n_kv, HEAD_DIM))
        new_s.append(s_l)
        y_s, _, _, _ = _layer(y_s, mod[l, 1:1 + dec_batch], p, f"lat{l}", n_seq=dec_batch, t=dec_seq,
                              rows_per_cond=dec_seq, rope=rope, past_k=past_k, past_v=past_v, layer=l,
                              h0=state_lru[:, l])
    return (y_p.reshape(batch, seq, d), y_s.reshape(dec_batch, dec_seq, d),
            jnp.stack(new_k, axis=1), jnp.stack(new_v, axis=1), jnp.stack(new_s, axis=1))
```

```python
import functools

import jax
import jax.numpy as jnp
from jax import lax
from jax.experimental import pallas as pl
from jax.experimental.pallas import tpu as pltpu

F32 = jnp.float32
BF16 = jnp.bfloat16

EPS = 1e-6
HEAD_DIM = 128
GRID_W = 64
ROPE_BASE = 10000.0
ROPE_FREQS = HEAD_DIM // 4
CONV_W = 4
CONV_PAD_L = 2
LRU_C = 8.0
CHUNK = 128
N_MOD = 6
N_BRANCH = 3
LOG2_E = 1.4426950408889634

LANES = 128
SUBLANES = 8
VMEM_LIMIT_BYTES = 56 * 1024 * 1024
BF16_ROWS = 2 * SUBLANES
ROW_CHUNK = BF16_ROWS
NORM_UNROLL = 4
VREG_BUDGET_ELEMS = 32 * SUBLANES * LANES

MOD_SHIFT1, MOD_SCALE1, MOD_GATE1, MOD_SHIFT2, MOD_SCALE2, MOD_GATE2 = range(N_MOD)


def _params(*semantics):
    return pltpu.CompilerParams(dimension_semantics=semantics, vmem_limit_bytes=VMEM_LIMIT_BYTES)


def _rms(x, g):
    return x * lax.rsqrt(jnp.mean(x * x, axis=-1, keepdims=True) + EPS) * g


def _sigmoid(x):
    return 0.5 * jnp.tanh(0.5 * x) + 0.5


def _row_tile(m, cap):
    if m <= cap:
        return m
    t = cap - cap % LANES
    while m % t:
        t -= LANES
    assert t > 0
    return t


def _mod_kernel(c_ref, w_ref, b_ref, o_ref):
    c = c_ref[...]
    s = (c * jax.nn.sigmoid(c)).astype(BF16)
    o_ref[...] = jnp.dot(s, w_ref[...].astype(BF16), preferred_element_type=F32) + b_ref[...]


def _modulation(cond, w_mod, b_mod):
    nc, d = cond.shape
    depth, _, n = w_mod.shape
    tn = _row_tile(n, 1024)
    out = pl.pallas_call(
        _mod_kernel,
        grid=(depth, n // tn),
        in_specs=[
            pl.BlockSpec((nc, d), lambda l, j: (0, 0)),
            pl.BlockSpec((None, d, tn), lambda l, j: (l, 0, j)),
            pl.BlockSpec((None, 1, tn), lambda l, j: (l, 0, j)),
        ],
        out_specs=pl.BlockSpec((None, nc, tn), lambda l, j: (l, 0, j)),
        out_shape=jax.ShapeDtypeStruct((depth, nc, n), F32),
        compiler_params=_params("parallel", "parallel"),
        name="modulation",
    )(cond, w_mod, b_mod.reshape(depth, 1, n))
    return out.reshape(depth, nc, N_MOD, d)


def _prenorm_mm_kernel(x_ref, g_ref, mod_ref, w_ref, o_ref, h_ref, *, shift_row, scale_row, relu2):
    @pl.when(pl.program_id(1) == 0)
    def _():
        gain = g_ref[...] * (1.0 + mod_ref[scale_row:scale_row + 1, :])
        shift = mod_ref[shift_row:shift_row + 1, :]

        def chunk(c, carry):
            rows = pl.ds(pl.multiple_of(c * ROW_CHUNK, ROW_CHUNK), ROW_CHUNK)
            x = x_ref[rows, :]
            inv = lax.rsqrt(jnp.mean(x * x, axis=-1, keepdims=True) + EPS)
            h_ref[rows, :] = (x * inv * gain + shift).astype(BF16)
            return carry

        lax.fori_loop(0, x_ref.shape[0] // ROW_CHUNK, chunk, 0, unroll=NORM_UNROLL)

    acc = jnp.dot(h_ref[...], w_ref[...], preferred_element_type=F32)
    if relu2:
        acc = jnp.square(jnp.maximum(acc, 0.0))
    o_ref[...] = acc.astype(o_ref.dtype)


def _prenorm_matmul(x, g, mod, w, layer, *, rows_per_cond, shift_row, scale_row, relu2, name, tn=None,
                    w_col_block=lambda j: j):
    m, d = x.shape
    n = w.shape[2]
    tm = _row_tile(rows_per_cond, 1024)
    tn = tn or _row_tile(n, 2048)
    assert rows_per_cond % tm == 0 and n % tn == 0
    return pl.pallas_call(
        functools.partial(_prenorm_mm_kernel, shift_row=shift_row, scale_row=scale_row, relu2=relu2),
        grid=(m // tm, n // tn),
        in_specs=[
            pl.BlockSpec((tm, d), lambda i, j: (i, 0)),
            pl.BlockSpec((1, d), lambda i, j: (0, 0)),
            pl.BlockSpec((None, N_MOD, d), lambda i, j: ((i * tm) // rows_per_cond, 0, 0)),
            pl.BlockSpec((None, d, tn), lambda i, j: (layer, 0, w_col_block(j))),
        ],
        out_specs=pl.BlockSpec((tm, tn), lambda i, j: (i, j)),
        out_shape=jax.ShapeDtypeStruct((m, n), BF16),
        scratch_shapes=[pltpu.VMEM((tm, d), BF16)],
        compiler_params=_params("parallel", "arbitrary"),
        name=name,
    )(x, g.reshape(1, d), mod, w)


def _mm_norm_res_kernel(a_ref, w_ref, g_ref, mod_ref, x_ref, o_ref, *acc_refs, gate_row, nk):
    def product():
        return jnp.dot(a_ref[...], w_ref[...], preferred_element_type=F32)

    def finish(y_ref):
        gain = mod_ref[gate_row:gate_row + 1, :] * g_ref[...]

        def chunk(c, carry):
            rows = pl.ds(pl.multiple_of(c * ROW_CHUNK, ROW_CHUNK), ROW_CHUNK)
            y = y_ref[rows, :]
            inv = lax.rsqrt(jnp.mean(y * y, axis=-1, keepdims=True) + EPS)
            o_ref[rows, :] = x_ref[rows, :] + y * inv * gain
            return carry

        lax.fori_loop(0, o_ref.shape[0] // ROW_CHUNK, chunk, 0, unroll=NORM_UNROLL)

    if nk == 1:
        o_ref[...] = product()
        finish(o_ref)
        return
    acc_ref, = acc_refs
    k = pl.program_id(1)

    @pl.when(k == 0)
    def _():
        acc_ref[...] = product()

    @pl.when(k > 0)
    def _():
        acc_ref[...] += product()

    @pl.when(k == nk - 1)
    def _():
        finish(acc_ref)


def _matmul_norm_residual(a, w, layer, g, mod, x, *, rows_per_cond, gate_row, name):
    m, kdim = a.shape
    d = w.shape[2]
    tm = _row_tile(rows_per_cond, 512)
    tk = _row_tile(kdim, 2048)
    nk = kdim // tk
    assert rows_per_cond % tm == 0
    return pl.pallas_call(
        functools.partial(_mm_norm_res_kernel, gate_row=gate_row, nk=nk),
        grid=(m // tm, nk),
        in_specs=[
            pl.BlockSpec((tm, tk), lambda i, k: (i, k)),
            pl.BlockSpec((None, tk, d), lambda i, k: (layer, k, 0)),
            pl.BlockSpec((1, d), lambda i, k: (0, 0)),
            pl.BlockSpec((None, N_MOD, d), lambda i, k: ((i * tm) // rows_per_cond, 0, 0)),
            pl.BlockSpec((tm, d), lambda i, k: (i, 0)),
        ],
        out_specs=pl.BlockSpec((tm, d), lambda i, k: (i, 0)),
        out_shape=jax.ShapeDtypeStruct((m, d), F32),
        scratch_shapes=[pltpu.VMEM((tm, d), F32)] if nk > 1 else [],
        compiler_params=_params("parallel", "arbitrary"),
        name=name,
    )(a, w, g.reshape(1, d), mod, x)


def _merge_kernel(a0_ref, a1_ref, a2_ref, w0_ref, w1_ref, w2_ref, g0_ref, g1_ref, g2_ref, o_ref):
    acc = _sigmoid(g0_ref[...].astype(F32)) * jnp.dot(a0_ref[...], w0_ref[...], preferred_element_type=F32)
    acc += _sigmoid(g1_ref[...].astype(F32)) * jnp.dot(a1_ref[...], w1_ref[...], preferred_element_type=F32)
    acc += _sigmoid(g2_ref[...].astype(F32)) * jnp.dot(a2_ref[...], w2_ref[...], preferred_element_type=F32)
    o_ref[...] = acc.astype(o_ref.dtype)


def _merge(branches, weights, layer, z, gate_col0, name):
    m, d = branches[0].shape
    tm = _row_tile(m, 512)
    tn = _row_tile(d, 1024)
    assert gate_col0 % tn == 0 and d % tn == 0
    a_spec = pl.BlockSpec((tm, d), lambda i, j: (i, 0))
    w_spec = pl.BlockSpec((None, d, tn), lambda i, j: (layer, 0, j))

    def gate_spec(b):
        off = (gate_col0 + b * d) // tn
        return pl.BlockSpec((tm, tn), lambda i, j: (i, off + j))

    return pl.pallas_call(
        _merge_kernel,
        grid=(m // tm, d // tn),
        in_specs=[a_spec] * N_BRANCH + [w_spec] * N_BRANCH + [gate_spec(b) for b in range(N_BRANCH)],
        out_specs=pl.BlockSpec((tm, tn), lambda i, j: (i, j)),
        out_shape=jax.ShapeDtypeStruct((m, d), BF16),
        compiler_params=_params("parallel", "arbitrary"),
        name=name,
    )(*branches, *weights, z, z, z)


def _rope(x, c, s):
    lane = lax.broadcasted_iota(jnp.int32, x.shape, 1)
    partner = jnp.where((lane % (2 * ROPE_FREQS)) < ROPE_FREQS,
                        pltpu.roll(x, HEAD_DIM - ROPE_FREQS, 1),
                        pltpu.roll(x, ROPE_FREQS, 1))
    return x * c + partner * s


def _attn_kernel(*refs, has_ctx, t, tq, q_group, unit_rows, softmax_rows):
    if has_ctx:
        (zq_ref, zk_ref, zv_ref, gq_ref, gk_ref, cq_ref, sq_ref, ck_ref, sk_ref, pk_ref, pv_ref,
         o_ref, kt_ref, v_ref, s0_ref, s1_ref, p0_ref, p1_ref) = refs
    else:
        (zq_ref, zk_ref, zv_ref, gq_ref, gk_ref, o_ref, ko_ref, vo_ref,
         kt_ref, v_ref, s0_ref, s1_ref, p0_ref, p1_ref) = refs
    s_refs, p_refs = (s0_ref, s1_ref), (p0_ref, p1_ref)

    @pl.when(pl.program_id(2) == 0)
    def _():
        k = _rms(zk_ref[...].astype(F32), gk_ref[...])
        v_ref[:, HEAD_DIM:] = jnp.ones((v_ref.shape[0], HEAD_DIM), BF16)
        if has_ctx:
            k = _rope(k, ck_ref[...], sk_ref[...])
            kt_ref[:, 0:t] = k.T.astype(BF16)
            kt_ref[:, t:] = pk_ref[...].T.astype(BF16)
            v_ref[0:t, 0:HEAD_DIM] = zv_ref[...]
            v_ref[t:, 0:HEAD_DIM] = pv_ref[...].astype(BF16)
        else:
            ko_ref[...] = k
            vo_ref[...] = zv_ref[...].astype(F32)
            kt_ref[...] = k.T.astype(BF16)
            v_ref[:, 0:HEAD_DIM] = zv_ref[...]

    q_scale = LOG2_E * HEAD_DIM ** -0.5

    units = [(g, r0) for r0 in range(0, tq, unit_rows) for g in range(q_group)]

    def scores(u):
        g, r0 = units[u]
        rows = slice(r0, r0 + unit_rows)
        q = _rms(zq_ref[rows, g * HEAD_DIM:(g + 1) * HEAD_DIM].astype(F32), gq_ref[...])
        if has_ctx:
            q = _rope(q, cq_ref[rows, :], sq_ref[rows, :])
        s_refs[u % 2][...] = jnp.dot((q * q_scale).astype(BF16), kt_ref[...], preferred_element_type=F32)

    def exponentials(u):
        s_ref, p_ref = s_refs[u % 2], p_refs[u % 2]
        groups = [slice(r0, r0 + softmax_rows) for r0 in range(0, unit_rows, softmax_rows)]
        maxima = [jnp.max(s_ref[rows, :], axis=-1, keepdims=True) for rows in groups]
        for rows, m in zip(groups, maxima):
            p_ref[rows, :] = jnp.exp2(s_ref[rows, :] - m).astype(BF16)

    def weighted_values(u):
        g, r0 = units[u]
        o = jnp.dot(p_refs[u % 2][...], v_ref[...], preferred_element_type=F32)
        o_ref[r0:r0 + unit_rows, g * HEAD_DIM:(g + 1) * HEAD_DIM] = (
            o[:, :HEAD_DIM] * (1.0 / o[:, HEAD_DIM:])).astype(o_ref.dtype)

    scores(0)
    for u in range(len(units)):
        if u + 1 < len(units):
            scores(u + 1)
        exponentials(u)
        weighted_values(u)


def _attention(z, g_q, g_k, *, n_seq, t, d, kv_w, k_col, v_col, rope=None, past_k=None, past_v=None, layer=None, name):
    has_ctx = rope is not None
    n_kv = kv_w // HEAD_DIM
    q_group = d // kv_w
    qw = q_group * HEAD_DIM
    past = past_k.shape[2] if has_ctx else 0
    tk = t + past
    tq = _row_tile(t, 256)
    nq = t // tq
    unit_rows = _row_tile(tq, 128)
    softmax_rows = min(unit_rows, max(BF16_ROWS, (VREG_BUDGET_ELEMS // tk) // BF16_ROWS * BF16_ROWS))
    assert unit_rows % softmax_rows == 0

    in_specs = [
        pl.BlockSpec((tq, qw), lambda b, h, i: (b * nq + i, h)),
        pl.BlockSpec((t, HEAD_DIM), lambda b, h, i: (b, k_col // HEAD_DIM + h)),
        pl.BlockSpec((t, HEAD_DIM), lambda b, h, i: (b, v_col // HEAD_DIM + h)),
        pl.BlockSpec((1, HEAD_DIM), lambda b, h, i: (0, 0)),
        pl.BlockSpec((1, HEAD_DIM), lambda b, h, i: (0, 0)),
    ]
    args = [z, z, z, g_q.reshape(1, HEAD_DIM), g_k.reshape(1, HEAD_DIM)]
    o_spec = pl.BlockSpec((tq, qw), lambda b, h, i: (b * nq + i, h))
    o_shape = jax.ShapeDtypeStruct((n_seq * t, d), BF16)
    if has_ctx:
        cos, sin = rope
        in_specs += [
            pl.BlockSpec((tq, HEAD_DIM), lambda b, h, i: (i, 0)),
            pl.BlockSpec((tq, HEAD_DIM), lambda b, h, i: (i, 0)),
            pl.BlockSpec((t, HEAD_DIM), lambda b, h, i: (0, 0)),
            pl.BlockSpec((t, HEAD_DIM), lambda b, h, i: (0, 0)),
            pl.BlockSpec((None, None, past, HEAD_DIM), lambda b, h, i: (b, layer, 0, h)),
            pl.BlockSpec((None, None, past, HEAD_DIM), lambda b, h, i: (b, layer, 0, h)),
        ]
        args += [cos, sin, cos, sin, past_k, past_v]
        out_specs = o_spec
        out_shape = o_shape
    else:
        kv_spec = pl.BlockSpec((None, t, HEAD_DIM), lambda b, h, i: (b, 0, h))
        kv_shape = jax.ShapeDtypeStruct((n_seq, t, kv_w), F32)
        out_specs = [o_spec, kv_spec, kv_spec]
        out_shape = [o_shape, kv_shape, kv_shape]

    return pl.pallas_call(
        functools.partial(_attn_kernel, has_ctx=has_ctx, t=t, tq=tq, q_group=q_group, unit_rows=unit_rows,
                          softmax_rows=softmax_rows),
        grid=(n_seq, n_kv, nq),
        in_specs=in_specs,
        out_specs=out_specs,
        out_shape=out_shape,
        scratch_shapes=[
            pltpu.VMEM((HEAD_DIM, tk), BF16),
            pltpu.VMEM((tk, 2 * HEAD_DIM), BF16),
            pltpu.VMEM((unit_rows, tk), F32),
            pltpu.VMEM((unit_rows, tk), F32),
            pltpu.VMEM((unit_rows, tk), BF16),
            pltpu.VMEM((unit_rows, tk), BF16),
        ],
        compiler_params=_params("parallel", "parallel", "arbitrary"),
        name=name,
    )(*args)


def _lru_kernel(zx_ref, zg_ref, cw_ref, cb_ref, wa_ref, wx_ref, ba_ref, bx_ref, lam_ref, h0_ref,
                o_ref, hfin_ref, xpad_ref, xc_ref, hs_ref, *, t, tc):
    cw = zx_ref.shape[1]
    n_lane_blocks = cw // LANES
    n_chunks = t // tc
    pad = SUBLANES
    steps = tc // SUBLANES
    assert steps % SUBLANES == 0

    def natural_rows(r, m):
        seg, j0 = divmod(m * SUBLANES, steps)
        return pl.ds(r + SUBLANES * j0 + seg, SUBLANES, stride=SUBLANES)

    xpad_ref[0:pad, :] = jnp.zeros((pad, cw), F32)
    xpad_ref[t + pad:t + 2 * pad, :] = jnp.zeros((pad, cw), F32)

    def copy_chunk(c, carry):
        r = pl.multiple_of(c * tc, tc)
        xpad_ref[pl.ds(r + pad, tc), :] = zx_ref[pl.ds(r, tc), :].astype(F32)
        return carry

    lax.fori_loop(0, n_chunks, copy_chunk, 0)

    def conv_chunk(c, carry):
        r = pl.multiple_of(c * tc, tc)
        xe = xpad_ref[pl.ds(r, tc + 2 * pad), :]
        n = tc + 2 * pad
        y = cb_ref[...] + pltpu.roll(xe, 2, 0)[pad:pad + tc] * cw_ref[0:1, :]
        y = y + pltpu.roll(xe, 1, 0)[pad:pad + tc] * cw_ref[1:2, :]
        y = y + xe[pad:pad + tc] * cw_ref[2:3, :]
        y = y + pltpu.roll(xe, n - 1, 0)[pad:pad + tc] * cw_ref[3:4, :]
        for nb in range(n_lane_blocks):
            for m in range(steps):
                xc_ref[nb, natural_rows(r, m), :] = y[m * SUBLANES:(m + 1) * SUBLANES, nb * LANES:(nb + 1) * LANES]
        return carry

    lax.fori_loop(0, n_chunks, conv_chunk, 0)

    sublane = lax.broadcasted_iota(jnp.int32, (SUBLANES, LANES), 0)

    half_k = [[(-0.5 * LRU_C * LOG2_E) * jax.nn.softplus(-lam_ref[direction:direction + 1, nb * LANES:(nb + 1) * LANES])
               for nb in range(n_lane_blocks)] for direction in range(2)]

    def gates(xn, direction, nb):
        lanes = slice(nb * LANES, (nb + 1) * LANES)
        xb = xn.astype(BF16)
        tanh_r = jnp.tanh(0.5 * (jnp.dot(xb, wa_ref[direction, nb], preferred_element_type=F32)
                                 + ba_ref[direction:direction + 1, lanes]))
        i = _sigmoid(jnp.dot(xb, wx_ref[direction, nb], preferred_element_type=F32)
                     + bx_ref[direction:direction + 1, lanes])
        a = jnp.exp2(half_k[direction][nb] * tanh_r + half_k[direction][nb])
        b = jnp.sqrt(jnp.maximum(1.0 - a * a, 0.0)) * (i * xn)
        return a, b

    def chunk_scan(a, b, carry, reverse):
        order = range(steps - 1, -1, -1) if reverse else range(steps)
        local, decay = [None] * steps, [None] * steps
        h, p = None, None
        for j in order:
            aj, bj = a[j * SUBLANES:(j + 1) * SUBLANES], b[j * SUBLANES:(j + 1) * SUBLANES]
            h = bj if h is None else aj * h + bj
            p = aj if p is None else aj * p
            local[j], decay[j] = h, p
        seg_h, seg_p = h, p
        for dist in (1, 2, 4):
            if reverse:
                edge, shift = sublane >= SUBLANES - dist, SUBLANES - dist
            else:
                edge, shift = sublane < dist, dist
            h_prev = jnp.where(edge, 0.0, pltpu.roll(seg_h, shift, 0))
            p_prev = jnp.where(edge, 1.0, pltpu.roll(seg_p, shift, 0))
            seg_h = seg_p * h_prev + seg_h
            seg_p = seg_p * p_prev
        seg_state = seg_h + seg_p * carry
        if reverse:
            entering = jnp.where(sublane == SUBLANES - 1, carry, pltpu.roll(seg_state, SUBLANES - 1, 0))
            leaving = jnp.broadcast_to(seg_state[0:1], (SUBLANES, LANES))
        else:
            entering = jnp.where(sublane == 0, carry, pltpu.roll(seg_state, 1, 0))
            leaving = jnp.broadcast_to(seg_state[SUBLANES - 1:SUBLANES], (SUBLANES, LANES))
        return [local[j] + decay[j] * entering for j in range(steps)], leaving

    def forward_chunk(c, carry):
        r = pl.multiple_of(c * tc, tc)
        new_carry = []
        for nb in range(n_lane_blocks):
            lanes = slice(nb * LANES, (nb + 1) * LANES)
            a, b = gates(xc_ref[nb, pl.ds(r, tc), :], 0, nb)
            states, leaving = chunk_scan(a, b, carry[nb], reverse=False)
            for j in range(steps):
                hs_ref[nb, pl.ds(r + j * SUBLANES, SUBLANES), :] = states[j]
            new_carry.append(leaving)
        return tuple(new_carry)

    h0f = tuple(jnp.broadcast_to(h0_ref[0:1, nb * LANES:(nb + 1) * LANES], (SUBLANES, LANES))
                for nb in range(n_lane_blocks))
    hf_last = lax.fori_loop(0, n_chunks, forward_chunk, h0f, unroll=2)

    def backward_chunk(c, carry):
        r = pl.multiple_of((n_chunks - 1 - c) * tc, tc)
        new_carry = []
        for nb in range(n_lane_blocks):
            lanes = slice(nb * LANES, (nb + 1) * LANES)
            a, b = gates(xc_ref[nb, pl.ds(r, tc), :], 1, nb)
            states, leaving = chunk_scan(a, b, carry[nb], reverse=True)
            for j in range(steps):
                hs_ref[nb, pl.ds(r + j * SUBLANES, SUBLANES), :] += states[j]
            new_carry.append(leaving)
        return tuple(new_carry)

    h0b = tuple(jnp.broadcast_to(h0_ref[1:2, nb * LANES:(nb + 1) * LANES], (SUBLANES, LANES))
                for nb in range(n_lane_blocks))
    hb_first = lax.fori_loop(0, n_chunks, backward_chunk, h0b, unroll=2)

    for nb in range(n_lane_blocks):
        lanes = slice(nb * LANES, (nb + 1) * LANES)
        hfin_ref[0:1, lanes] = hf_last[nb][0:1]
        hfin_ref[1:2, lanes] = hb_first[nb][0:1]

    def gate_chunk(c, carry):
        r = pl.multiple_of(c * tc, tc)
        for nb in range(n_lane_blocks):
            lanes = slice(nb * LANES, (nb + 1) * LANES)
            for m in range(0, steps, 2):
                rows = pl.ds(r + m * SUBLANES, BF16_ROWS)
                hsum = jnp.concatenate([hs_ref[nb, natural_rows(r, m), :], hs_ref[nb, natural_rows(r, m + 1), :]],
                                       axis=0)
                o_ref[rows, lanes] = (hsum * jax.nn.gelu(zg_ref[rows, lanes].astype(F32))).astype(o_ref.dtype)
        return carry

    lax.fori_loop(0, n_chunks, gate_chunk, 0)


def _rglru(z, conv_w, conv_b, wa, wx, ba, bx, lam, h0, *, n_seq, t, width, x_col, g_col, name):
    cw = _row_tile(width, 512)
    tc = _row_tile(t, 128)
    lb = cw // LANES
    assert wa.shape[-1] == LANES and x_col % cw == 0 and g_col % cw == 0
    vec_spec = pl.BlockSpec((2, cw), lambda s, c: (0, c))
    w_spec = pl.BlockSpec((2, lb, LANES, LANES), lambda s, c: (0, c, 0, 0))
    return pl.pallas_call(
        functools.partial(_lru_kernel, t=t, tc=tc),
        grid=(n_seq, width // cw),
        in_specs=[
            pl.BlockSpec((t, cw), lambda s, c: (s, x_col // cw + c)),
            pl.BlockSpec((t, cw), lambda s, c: (s, g_col // cw + c)),
            pl.BlockSpec((CONV_W, cw), lambda s, c: (0, c)),
            pl.BlockSpec((1, cw), lambda s, c: (0, c)),
            w_spec, w_spec, vec_spec, vec_spec, vec_spec,
            pl.BlockSpec((None, 2, cw), lambda s, c: (s, 0, c)),
        ],
        out_specs=[
            pl.BlockSpec((t, cw), lambda s, c: (s, c)),
            pl.BlockSpec((None, 2, cw), lambda s, c: (s, 0, c)),
        ],
        out_shape=[
            jax.ShapeDtypeStruct((n_seq * t, width), BF16),
            jax.ShapeDtypeStruct((n_seq, 2, width), F32),
        ],
        scratch_shapes=[
            pltpu.VMEM((t + 2 * SUBLANES, cw), F32),
            pltpu.VMEM((lb, t, LANES), F32),
            pltpu.VMEM((lb, t, LANES), F32),
        ],
        compiler_params=_params("parallel", "parallel"),
        name=name,
    )(z, z, conv_w, conv_b.reshape(1, width), wa, wx, ba, bx, lam, h0)


def _cmix_kernel(zu_ref, zv_ref, g_ref, ws_ref, bs_ref, o_ref, v_ref, *, n_groups):
    rows, width = zu_ref.shape
    gw = width // n_groups
    def norm_chunk(c, carry):
        r = pl.ds(pl.multiple_of(c * ROW_CHUNK, ROW_CHUNK), ROW_CHUNK)
        v_ref[r, :] = _rms(jax.nn.gelu(zv_ref[r, :].astype(F32)), g_ref[...]).astype(BF16)
        return carry

    lax.fori_loop(0, rows // ROW_CHUNK, norm_chunk, 0, unroll=NORM_UNROLL)
    for ch in range(rows // CHUNK):
        r = slice(ch * CHUNK, (ch + 1) * CHUNK)
        for g in range(n_groups):
            cols = slice(g * gw, (g + 1) * gw)
            mixed = jnp.dot(ws_ref[g], v_ref[r, cols], preferred_element_type=F32) + bs_ref[g]
            o_ref[r, cols] = (jax.nn.gelu(zu_ref[r, cols].astype(F32)) * mixed).astype(o_ref.dtype)


def _chunk_mix(z, cm_g, ws, bs, *, width, u_col, v_col, name):
    m = z.shape[0]
    n_groups = ws.shape[0]
    tr = _row_tile(m, 2 * CHUNK)
    assert u_col % width == 0 and v_col % width == 0
    return pl.pallas_call(
        functools.partial(_cmix_kernel, n_groups=n_groups),
        grid=(m // tr,),
        in_specs=[
            pl.BlockSpec((tr, width), lambda i: (i, u_col // width)),
            pl.BlockSpec((tr, width), lambda i: (i, v_col // width)),
            pl.BlockSpec((1, width), lambda i: (0, 0)),
            pl.BlockSpec(ws.shape, lambda i: (0, 0, 0)),
            pl.BlockSpec(bs.shape, lambda i: (0, 0, 0)),
        ],
        out_specs=pl.BlockSpec((tr, width), lambda i: (i, 0)),
        out_shape=jax.ShapeDtypeStruct((m, width), BF16),
        scratch_shapes=[pltpu.VMEM((tr, width), BF16)],
        compiler_params=_params("parallel"),
        name=name,
    )(z, z, cm_g.reshape(1, width), ws, bs)


def _rope_tables(n_tokens):
    rows = n_tokens // GRID_W
    pos_row = jnp.repeat(jnp.arange(rows), GRID_W).astype(F32)
    pos_col = (jnp.arange(n_tokens) % GRID_W).astype(F32)
    inv = ROPE_BASE ** (-jnp.arange(ROPE_FREQS, dtype=F32) / ROPE_FREQS)
    cr, sr = jnp.cos(pos_row[:, None] * inv), jnp.sin(pos_row[:, None] * inv)
    cc, sc = jnp.cos(pos_col[:, None] * inv), jnp.sin(pos_col[:, None] * inv)
    return (jnp.concatenate([cr, cr, cc, cc], axis=-1), jnp.concatenate([-sr, sr, -sc, sc], axis=-1))


def _layer(x, mod, p, tag, *, n_seq, t, rows_per_cond, rope=None, past_k=None, past_v=None, layer=None, h0=None):
    d = x.shape[1]
    kv_w = p["kv_w"]
    col_lx, col_lg, col_cu, col_cv, col_g, col_k = d, 2 * d, 3 * d, 4 * d, 5 * d, 8 * d
    col_v = col_k + kv_w

    tn = 2 * kv_w
    assert d % tn == 0
    q_blocks, rest_blocks = d // tn, 7 * d // tn

    def w_col_block(j):
        return jnp.where(j < q_blocks, j, jnp.where(j < q_blocks + rest_blocks, j + 1, q_blocks))

    li = p["layer"]
    z = _prenorm_matmul(x, p["g_pre_mix"], mod, p["w_in"], li, rows_per_cond=rows_per_cond,
                        shift_row=MOD_SHIFT1, scale_row=MOD_SCALE1, relu2=False, name=f"in_proj_{tag}",
                        tn=tn, w_col_block=w_col_block)
    attn = _attention(z, p["g_q"], p["g_k"], n_seq=n_seq, t=t, d=d, kv_w=kv_w, k_col=col_k, v_col=col_v,
                      rope=rope, past_k=past_k, past_v=past_v, layer=layer, name=f"attention_{tag}")
    if rope is None:
        attn_o, k_new, v_new = attn
    else:
        attn_o, k_new, v_new = attn, None, None
    lru_o, h_fin = _rglru(z, p["conv_w"], p["conv_b"], p["lru_wa"], p["lru_wx"], p["lru_ba"], p["lru_bx"],
                          p["lru_lam"], h0, n_seq=n_seq, t=t, width=d, x_col=col_lx, g_col=col_lg,
                          name=f"rglru_{tag}")
    cm_o = _chunk_mix(z, p["cm_g"], p["cm_ws"], p["cm_bs"], width=d, u_col=col_cu, v_col=col_cv,
                      name=f"chunk_mix_{tag}")
    merged = _merge((attn_o, lru_o, cm_o), (p["w_attn_out"], p["w_lru_out"], p["w_cm_out"]), li, z, col_g,
                    name=f"merge_{tag}")
    x = _matmul_norm_residual(merged, p["w_out"], li, p["g_post_mix"], mod, x, rows_per_cond=rows_per_cond,
                              gate_row=MOD_GATE1, name=f"out_proj_{tag}")
    f = _prenorm_matmul(x, p["g_pre_ff"], mod, p["w_ff1"], li, rows_per_cond=rows_per_cond,
                        shift_row=MOD_SHIFT2, scale_row=MOD_SCALE2, relu2=True, name=f"ff1_{tag}")
    x = _matmul_norm_residual(f, p["w_ff2"], li, p["g_post_ff"], mod, x, rows_per_cond=rows_per_cond,
                              gate_row=MOD_GATE2, name=f"ff2_{tag}")
    return x, k_new, v_new, h_fin


def kernel(x_prompt, x_sample, cache_k, cache_v, state_lru, c, c_ctx, w_mod, b_mod, g_pre_mix, g_post_mix, g_pre_ff, g_post_ff, w_in, g_q, g_k, w_attn_out, conv_w, conv_b, lru_wa, lru_ba, lru_wx, lru_bx, lru_lam, w_lru_out, cm_g, cm_ws, cm_bs, w_cm_out, w_out, w_ff1, w_ff2):
    batch, seq, d = x_prompt.shape
    dec_batch, dec_seq, _ = x_sample.shape
    depth = w_in.shape[0]
    in_w = w_in.shape[2]
    kv_w = (in_w - 8 * d) // 2
    n_kv = kv_w // HEAD_DIM
    past = cache_k.shape[2]
    n_groups = cm_ws.shape[1]

    n_cond = 1 + dec_batch
    cond = jnp.concatenate([c_ctx[None, :], c], axis=0)
    cond = jnp.pad(cond, ((0, -n_cond % SUBLANES), (0, 0)))
    mod = _modulation(cond, w_mod, b_mod)

    w_in_b = w_in.astype(BF16)
    w_attn_out_b, w_lru_out_b, w_cm_out_b = w_attn_out.astype(BF16), w_lru_out.astype(BF16), w_cm_out.astype(BF16)
    w_out_b, w_ff1_b, w_ff2_b = w_out.astype(BF16), w_ff1.astype(BF16), w_ff2.astype(BF16)
    lru_wa_b, lru_wx_b, cm_ws_b = lru_wa.astype(BF16), lru_wx.astype(BF16), cm_ws.astype(BF16)
    cm_bs_b = jnp.broadcast_to(cm_bs[:, :, :, None], cm_bs.shape + (d // n_groups,))

    rope = _rope_tables(dec_seq)
    past_k = cache_k.reshape(dec_batch, depth, past, kv_w)
    past_v = cache_v.reshape(dec_batch, depth, past, kv_w)
    h0_prompt = jnp.zeros((batch, 2, d), F32)

    y_p = x_prompt.reshape(batch * seq, d)
    y_s = x_sample.reshape(dec_batch * dec_seq, d)
    new_k, new_v, new_s = [], [], []
    for l in range(depth):
        p = {
            "kv_w": kv_w, "layer": l, "g_pre_mix": g_pre_mix[l], "g_post_mix": g_post_mix[l],
            "g_pre_ff": g_pre_ff[l], "g_post_ff": g_post_ff[l], "w_in": w_in_b, "g_q": g_q[l], "g_k": g_k[l],
            "w_attn_out": w_attn_out_b, "conv_w": conv_w[l], "conv_b": conv_b[l],
            "lru_wa": lru_wa_b[l], "lru_ba": lru_ba[l], "lru_wx": lru_wx_b[l], "lru_bx": lru_bx[l],
            "lru_lam": lru_lam[l], "w_lru_out": w_lru_out_b, "cm_g": cm_g[l], "cm_ws": cm_ws_b[l],
            "cm_bs": cm_bs_b[l], "w_cm_out": w_cm_out_b, "w_out": w_out_b, "w_ff1": w_ff1_b,
            "w_ff2": w_ff2_b,
        }
        y_p, k_l, v_l, s_l = _layer(y_p, mod[l, 0:1], p, f"ctx{l}", n_seq=batch, t=seq,
                                    rows_per_cond=batch * seq, h0=h0_prompt)
        new_k.append(k_l.reshape(batch, seq, n_kv, HEAD_DIM))
        new_v.append(v_l.reshape(batch, seq, n_kv, HEAD_DIM))
        new_s.append(s_l)
        y_s, _, _, _ = _layer(y_s, mod[l, 1:1 + dec_batch], p, f"lat{l}", n_seq=dec_batch, t=dec_seq,
                              rows_per_cond=dec_seq, rope=rope, past_k=past_k, past_v=past_v, layer=l,
                              h0=state_lru[:, l])
    return (y_p.reshape(batch, seq, d), y_s.reshape(dec_batch, dec_seq, d),
            jnp.stack(new_k, axis=1), jnp.stack(new_v, axis=1), jnp.stack(new_s, axis=1))
```

```python
import functools

import jax
import jax.numpy as jnp
from jax import lax
from jax.experimental import pallas as pl
from jax.experimental.pallas import tpu as pltpu

F32 = jnp.float32
BF16 = jnp.bfloat16

EPS = 1e-6
HEAD_DIM = 128
GRID_W = 64
ROPE_BASE = 10000.0
ROPE_FREQS = HEAD_DIM // 4
CONV_W = 4
CONV_PAD_L = 2
LRU_C = 8.0
CHUNK = 128
N_MOD = 6
N_BRANCH = 3
LOG2_E = 1.4426950408889634

LANES = 128
SUBLANES = 8
VMEM_LIMIT_BYTES = 56 * 1024 * 1024
BF16_ROWS = 2 * SUBLANES
ROW_CHUNK = BF16_ROWS
NORM_UNROLL = 4
VREG_BUDGET_ELEMS = 32 * SUBLANES * LANES

MOD_SHIFT1, MOD_SCALE1, MOD_GATE1, MOD_SHIFT2, MOD_SCALE2, MOD_GATE2 = range(N_MOD)


def _params(*semantics):
    return pltpu.CompilerParams(dimension_semantics=semantics, vmem_limit_bytes=VMEM_LIMIT_BYTES)


def _rms(x, g):
    return x * lax.rsqrt(jnp.mean(x * x, axis=-1, keepdims=True) + EPS) * g


def _sigmoid(x):
    return 0.5 * jnp.tanh(0.5 * x) + 0.5


def _row_tile(m, cap):
    if m <= cap:
        return m
    t = cap - cap % LANES
    while m % t:
        t -= LANES
    assert t > 0
    return t


def _mod_kernel(c_ref, w_ref, b_ref, o_ref):
    c = c_ref[...]
    s = (c * jax.nn.sigmoid(c)).astype(BF16)
    o_ref[...] = jnp.dot(s, w_ref[...].astype(BF16), preferred_element_type=F32) + b_ref[...]


def _modulation(cond, w_mod, b_mod):
    nc, d = cond.shape
    depth, _, n = w_mod.shape
    tn = _row_tile(n, 1024)
    out = pl.pallas_call(
        _mod_kernel,
        grid=(depth, n // tn),
        in_specs=[
            pl.BlockSpec((nc, d), lambda l, j: (0, 0)),
            pl.BlockSpec((None, d, tn), lambda l, j: (l, 0, j)),
            pl.BlockSpec((None, 1, tn), lambda l, j: (l, 0, j)),
        ],
        out_specs=pl.BlockSpec((None, nc, tn), lambda l, j: (l, 0, j)),
        out_shape=jax.ShapeDtypeStruct((depth, nc, n), F32),
        compiler_params=_params("parallel", "parallel"),
        name="modulation",
    )(cond, w_mod, b_mod.reshape(depth, 1, n))
    return out.reshape(depth, nc, N_MOD, d)


def _prenorm_mm_kernel(x_ref, g_ref, mod_ref, w_ref, o_ref, h_ref, *, shift_row, scale_row, relu2,
                       rows_per_cond, n_tiles, n_slices):
    i, j = pl.program_id(0), pl.program_id(1)
    tm = x_ref.shape[0]

    def gain_shift(tile):
        cond = (tile * tm) // rows_per_cond
        return (g_ref[...] * (1.0 + mod_ref[cond, scale_row:scale_row + 1, :]),
                mod_ref[cond, shift_row:shift_row + 1, :])

    def normalise(rows, slot, gain, shift):
        x = x_ref[rows, :]
        inv = lax.rsqrt(jnp.mean(x * x, axis=-1, keepdims=True) + EPS)
        h_ref[slot, rows, :] = (x * inv * gain + shift).astype(BF16)

    @pl.when(jnp.logical_and(i == 0, j == 0))
    def _():
        gain, shift = gain_shift(0)

        def chunk(c, carry):
            normalise(pl.ds(pl.multiple_of(c * ROW_CHUNK, ROW_CHUNK), ROW_CHUNK), 0, gain, shift)
            return carry

        lax.fori_loop(0, tm // ROW_CHUNK, chunk, 0, unroll=NORM_UNROLL)

    slice_rows = tm // n_slices
    gain, shift = gain_shift(jnp.minimum(i + 1, n_tiles - 1))

    def normalise_slice(s):
        row0 = pl.multiple_of(s * slice_rows, slice_rows)
        for c in range(slice_rows // ROW_CHUNK):
            normalise(pl.ds(row0 + c * ROW_CHUNK, ROW_CHUNK), (i + 1) % 2, gain, shift)

    @pl.when(jnp.logical_and(i == 0, j == 1))
    def _():
        normalise_slice(0)

    normalise_slice(jnp.minimum(j, n_slices - 1))

    acc = jnp.dot(h_ref[i % 2], w_ref[...], preferred_element_type=F32)
    if relu2:
        acc = jnp.square(jnp.maximum(acc, 0.0))
    o_ref[...] = acc.astype(o_ref.dtype)


def _prenorm_matmul(x, g, mod, w, layer, *, rows_per_cond, shift_row, scale_row, relu2, name, tn=None,
                    w_col_block=lambda j: j):
    m, d = x.shape
    n = w.shape[2]
    tm = _row_tile(rows_per_cond, 1024)
    tn = tn or _row_tile(n, 1024)
    n_tiles, nj = m // tm, n // tn
    assert rows_per_cond % tm == 0 and n % tn == 0 and nj >= 2
    n_slices = max(s for s in range(1, nj + 1) if (tm // ROW_CHUNK) % s == 0)

    def x_tile(i, j):
        return jnp.where(jnp.logical_and(i == 0, j == 0), 0, jnp.minimum(i + 1, n_tiles - 1))

    return pl.pallas_call(
        functools.partial(_prenorm_mm_kernel, shift_row=shift_row, scale_row=scale_row, relu2=relu2,
                          rows_per_cond=rows_per_cond, n_tiles=n_tiles, n_slices=n_slices),
        grid=(n_tiles, nj),
        in_specs=[
            pl.BlockSpec((tm, d), lambda i, j: (x_tile(i, j), 0)),
            pl.BlockSpec((1, d), lambda i, j: (0, 0)),
            pl.BlockSpec(mod.shape, lambda i, j: (0, 0, 0)),
            pl.BlockSpec((None, d, tn), lambda i, j: (layer, 0, w_col_block(j))),
        ],
        out_specs=pl.BlockSpec((tm, tn), lambda i, j: (i, j)),
        out_shape=jax.ShapeDtypeStruct((m, n), BF16),
        scratch_shapes=[pltpu.VMEM((2, tm, d), BF16)],
        compiler_params=_params("arbitrary", "arbitrary"),
        name=name,
    )(x, g.reshape(1, d), mod, w)


def _mm_norm_res_kernel(a_ref, w_ref, g_ref, mod_ref, x_ref, o_ref, *acc_refs, gate_row, nk):
    def product():
        return jnp.dot(a_ref[...], w_ref[...], preferred_element_type=F32)

    def finish(y_ref):
        gain = mod_ref[gate_row:gate_row + 1, :] * g_ref[...]

        def chunk(c, carry):
            rows = pl.ds(pl.multiple_of(c * ROW_CHUNK, ROW_CHUNK), ROW_CHUNK)
            y = y_ref[rows, :]
            inv = lax.rsqrt(jnp.mean(y * y, axis=-1, keepdims=True) + EPS)
            o_ref[rows, :] = x_ref[rows, :] + y * inv * gain
            return carry

        lax.fori_loop(0, o_ref.shape[0] // ROW_CHUNK, chunk, 0, unroll=NORM_UNROLL)

    if nk == 1:
        o_ref[...] = product()
        finish(o_ref)
        return
    acc_ref, = acc_refs
    k = pl.program_id(1)

    @pl.when(k == 0)
    def _():
        acc_ref[...] = product()

    @pl.when(k > 0)
    def _():
        acc_ref[...] += product()

    @pl.when(k == nk - 1)
    def _():
        finish(acc_ref)


def _matmul_norm_residual(a, w, layer, g, mod, x, *, rows_per_cond, gate_row, name):
    m, kdim = a.shape
    d = w.shape[2]
    tm = _row_tile(rows_per_cond, 512)
    tk = _row_tile(kdim, 2048)
    nk = kdim // tk
    assert rows_per_cond % tm == 0
    return pl.pallas_call(
        functools.partial(_mm_norm_res_kernel, gate_row=gate_row, nk=nk),
        grid=(m // tm, nk),
        in_specs=[
            pl.BlockSpec((tm, tk), lambda i, k: (i, k)),
            pl.BlockSpec((None, tk, d), lambda i, k: (layer, k, 0)),
            pl.BlockSpec((1, d), lambda i, k: (0, 0)),
            pl.BlockSpec((None, N_MOD, d), lambda i, k: ((i * tm) // rows_per_cond, 0, 0)),
            pl.BlockSpec((tm, d), lambda i, k: (i, 0)),
        ],
        out_specs=pl.BlockSpec((tm, d), lambda i, k: (i, 0)),
        out_shape=jax.ShapeDtypeStruct((m, d), F32),
        scratch_shapes=[pltpu.VMEM((tm, d), F32)] if nk > 1 else [],
        compiler_params=_params("parallel", "arbitrary"),
        name=name,
    )(a, w, g.reshape(1, d), mod, x)


def _merge_kernel(a0_ref, a1_ref, a2_ref, w0_ref, w1_ref, w2_ref, g0_ref, g1_ref, g2_ref, o_ref):
    acc = _sigmoid(g0_ref[...].astype(F32)) * jnp.dot(a0_ref[...], w0_ref[...], preferred_element_type=F32)
    acc += _sigmoid(g1_ref[...].astype(F32)) * jnp.dot(a1_ref[...], w1_ref[...], preferred_element_type=F32)
    acc += _sigmoid(g2_ref[...].astype(F32)) * jnp.dot(a2_ref[...], w2_ref[...], preferred_element_type=F32)
    o_ref[...] = acc.astype(o_ref.dtype)


def _merge(branches, weights, layer, z, gate_col0, name):
    m, d = branches[0].shape
    tm = _row_tile(m, 512)
    tn = _row_tile(d, 1024)
    assert gate_col0 % tn == 0 and d % tn == 0
    a_spec = pl.BlockSpec((tm, d), lambda i, j: (i, 0))
    w_spec = pl.BlockSpec((None, d, tn), lambda i, j: (layer, 0, j))

    def gate_spec(b):
        off = (gate_col0 + b * d) // tn
        return pl.BlockSpec((tm, tn), lambda i, j: (i, off + j))

    return pl.pallas_call(
        _merge_kernel,
        grid=(m // tm, d // tn),
        in_specs=[a_spec] * N_BRANCH + [w_spec] * N_BRANCH + [gate_spec(b) for b in range(N_BRANCH)],
        out_specs=pl.BlockSpec((tm, tn), lambda i, j: (i, j)),
        out_shape=jax.ShapeDtypeStruct((m, d), BF16),
        compiler_params=_params("parallel", "arbitrary"),
        name=name,
    )(*branches, *weights, z, z, z)


def _rope(x, c, s):
    lane = lax.broadcasted_iota(jnp.int32, x.shape, 1)
    partner = jnp.where((lane % (2 * ROPE_FREQS)) < ROPE_FREQS,
                        pltpu.roll(x, HEAD_DIM - ROPE_FREQS, 1),
                        pltpu.roll(x, ROPE_FREQS, 1))
    return x * c + partner * s


def _attn_kernel(*refs, has_ctx, t, tq, q_group, unit_rows, softmax_rows):
    if has_ctx:
        (zq_ref, zk_ref, zv_ref, gq_ref, gk_ref, cq_ref, sq_ref, ck_ref, sk_ref, pk_ref, pv_ref,
         o_ref, kt_ref, v_ref, s0_ref, s1_ref, p0_ref, p1_ref) = refs
    else:
        (zq_ref, zk_ref, zv_ref, gq_ref, gk_ref, o_ref, ko_ref, vo_ref,
         kt_ref, v_ref, s0_ref, s1_ref, p0_ref, p1_ref) = refs
    s_refs, p_refs = (s0_ref, s1_ref), (p0_ref, p1_ref)

    @pl.when(pl.program_id(2) == 0)
    def _():
        k = _rms(zk_ref[...].astype(F32), gk_ref[...])
        v_ref[:, HEAD_DIM:] = jnp.ones((v_ref.shape[0], HEAD_DIM), BF16)
        if has_ctx:
            k = _rope(k, ck_ref[...], sk_ref[...])
            kt_ref[:, 0:t] = k.T.astype(BF16)
            kt_ref[:, t:] = pk_ref[...].T.astype(BF16)
            v_ref[0:t, 0:HEAD_DIM] = zv_ref[...]
            v_ref[t:, 0:HEAD_DIM] = pv_ref[...].astype(BF16)
        else:
            ko_ref[...] = k
            vo_ref[...] = zv_ref[...].astype(F32)
            kt_ref[...] = k.T.astype(BF16)
            v_ref[:, 0:HEAD_DIM] = zv_ref[...]

    q_scale = LOG2_E * HEAD_DIM ** -0.5

    units = [(g, r0) for r0 in range(0, tq, unit_rows) for g in range(q_group)]

    def scores(u):
        g, r0 = units[u]
        rows = slice(r0, r0 + unit_rows)
        q = _rms(zq_ref[rows, g * HEAD_DIM:(g + 1) * HEAD_DIM].astype(F32), gq_ref[...])
        if has_ctx:
            q = _rope(q, cq_ref[rows, :], sq_ref[rows, :])
        s_refs[u % 2][...] = jnp.dot((q * q_scale).astype(BF16), kt_ref[...], preferred_element_type=F32)

    def exponentials(u):
        s_ref, p_ref = s_refs[u % 2], p_refs[u % 2]
        groups = [slice(r0, r0 + softmax_rows) for r0 in range(0, unit_rows, softmax_rows)]
        maxima = [jnp.max(s_ref[rows, :], axis=-1, keepdims=True) for rows in groups]
        for rows, m in zip(groups, maxima):
            p_ref[rows, :] = jnp.exp2(s_ref[rows, :] - m).astype(BF16)

    def weighted_values(u):
        g, r0 = units[u]
        o = jnp.dot(p_refs[u % 2][...], v_ref[...], preferred_element_type=F32)
        o_ref[r0:r0 + unit_rows, g * HEAD_DIM:(g + 1) * HEAD_DIM] = (
            o[:, :HEAD_DIM] * (1.0 / o[:, HEAD_DIM:])).astype(o_ref.dtype)

    scores(0)
    for u in range(len(units)):
        if u + 1 < len(units):
            scores(u + 1)
        exponentials(u)
        weighted_values(u)


def _attention(z, g_q, g_k, *, n_seq, t, d, kv_w, k_col, v_col, rope=None, past_k=None, past_v=None, layer=None, name):
    has_ctx = rope is not None
    n_kv = kv_w // HEAD_DIM
    q_group = d // kv_w
    qw = q_group * HEAD_DIM
    past = past_k.shape[2] if has_ctx else 0
    tk = t + past
    tq = _row_tile(t, 256)
    nq = t // tq
    unit_rows = _row_tile(tq, 128)
    softmax_rows = min(unit_rows, max(BF16_ROWS, (VREG_BUDGET_ELEMS // tk) // BF16_ROWS * BF16_ROWS))
    assert unit_rows % softmax_rows == 0

    in_specs = [
        pl.BlockSpec((tq, qw), lambda b, h, i: (b * nq + i, h)),
        pl.BlockSpec((t, HEAD_DIM), lambda b, h, i: (b, k_col // HEAD_DIM + h)),
        pl.BlockSpec((t, HEAD_DIM), lambda b, h, i: (b, v_col // HEAD_DIM + h)),
        pl.BlockSpec((1, HEAD_DIM), lambda b, h, i: (0, 0)),
        pl.BlockSpec((1, HEAD_DIM), lambda b, h, i: (0, 0)),
    ]
    args = [z, z, z, g_q.reshape(1, HEAD_DIM), g_k.reshape(1, HEAD_DIM)]
    o_spec = pl.BlockSpec((tq, qw), lambda b, h, i: (b * nq + i, h))
    o_shape = jax.ShapeDtypeStruct((n_seq * t, d), BF16)
    if has_ctx:
        cos, sin = rope
        in_specs += [
            pl.BlockSpec((tq, HEAD_DIM), lambda b, h, i: (i, 0)),
            pl.BlockSpec((tq, HEAD_DIM), lambda b, h, i: (i, 0)),
            pl.BlockSpec((t, HEAD_DIM), lambda b, h, i: (0, 0)),
            pl.BlockSpec((t, HEAD_DIM), lambda b, h, i: (0, 0)),
            pl.BlockSpec((None, None, past, HEAD_DIM), lambda b, h, i: (b, layer, 0, h)),
            pl.BlockSpec((None, None, past, HEAD_DIM), lambda b, h, i: (b, layer, 0, h)),
        ]
        args += [cos, sin, cos, sin, past_k, past_v]
        out_specs = o_spec
        out_shape = o_shape
    else:
        kv_spec = pl.BlockSpec((None, t, HEAD_DIM), lambda b, h, i: (b, 0, h))
        kv_shape = jax.ShapeDtypeStruct((n_seq, t, kv_w), F32)
        out_specs = [o_spec, kv_spec, kv_spec]
        out_shape = [o_shape, kv_shape, kv_shape]

    return pl.pallas_call(
        functools.partial(_attn_kernel, has_ctx=has_ctx, t=t, tq=tq, q_group=q_group, unit_rows=unit_rows,
                          softmax_rows=softmax_rows),
        grid=(n_seq, n_kv, nq),
        in_specs=in_specs,
        out_specs=out_specs,
        out_shape=out_shape,
        scratch_shapes=[
            pltpu.VMEM((HEAD_DIM, tk), BF16),
            pltpu.VMEM((tk, 2 * HEAD_DIM), BF16),
            pltpu.VMEM((unit_rows, tk), F32),
            pltpu.VMEM((unit_rows, tk), F32),
            pltpu.VMEM((unit_rows, tk), BF16),
            pltpu.VMEM((unit_rows, tk), BF16),
        ],
        compiler_params=_params("parallel", "parallel", "arbitrary"),
        name=name,
    )(*args)


def _lru_kernel(zx_ref, zg_ref, cw_ref, cb_ref, wa_ref, wx_ref, ba_ref, bx_ref, lam_ref, h0_ref,
                o_ref, hfin_ref, xpad_ref, xc_ref, hs_ref, *, t, tc):
    cw = zx_ref.shape[1]
    n_lane_blocks = cw // LANES
    n_chunks = t // tc
    pad = SUBLANES
    steps = tc // SUBLANES
    assert steps % SUBLANES == 0

    def natural_rows(r, m):
        seg, j0 = divmod(m * SUBLANES, steps)
        return pl.ds(r + SUBLANES * j0 + seg, SUBLANES, stride=SUBLANES)

    xpad_ref[0:pad, :] = jnp.zeros((pad, cw), F32)
    xpad_ref[t + pad:t + 2 * pad, :] = jnp.zeros((pad, cw), F32)

    def copy_chunk(c, carry):
        r = pl.multiple_of(c * tc, tc)
        xpad_ref[pl.ds(r + pad, tc), :] = zx_ref[pl.ds(r, tc), :].astype(F32)
        return carry

    lax.fori_loop(0, n_chunks, copy_chunk, 0)

    def conv_chunk(c, carry):
        r = pl.multiple_of(c * tc, tc)
        xe = xpad_ref[pl.ds(r, tc + 2 * pad), :]
        n = tc + 2 * pad
        y = cb_ref[...] + pltpu.roll(xe, 2, 0)[pad:pad + tc] * cw_ref[0:1, :]
        y = y + pltpu.roll(xe, 1, 0)[pad:pad + tc] * cw_ref[1:2, :]
        y = y + xe[pad:pad + tc] * cw_ref[2:3, :]
        y = y + pltpu.roll(xe, n - 1, 0)[pad:pad + tc] * cw_ref[3:4, :]
        for nb in range(n_lane_blocks):
            for m in range(steps):
                xc_ref[nb, natural_rows(r, m), :] = y[m * SUBLANES:(m + 1) * SUBLANES, nb * LANES:(nb + 1) * LANES]
        return carry

    lax.fori_loop(0, n_chunks, conv_chunk, 0)

    sublane = lax.broadcasted_iota(jnp.int32, (SUBLANES, LANES), 0)

    half_k = [[(-0.5 * LRU_C * LOG2_E) * jax.nn.softplus(-lam_ref[direction:direction + 1, nb * LANES:(nb + 1) * LANES])
               for nb in range(n_lane_blocks)] for direction in range(2)]

    def gates(xn, direction, nb):
        lanes = slice(nb * LANES, (nb + 1) * LANES)
        xb = xn.astype(BF16)
        tanh_r = jnp.tanh(0.5 * (jnp.dot(xb, wa_ref[direction, nb], preferred_element_type=F32)
                                 + ba_ref[direction:direction + 1, lanes]))
        i = _sigmoid(jnp.dot(xb, wx_ref[direction, nb], preferred_element_type=F32)
                     + bx_ref[direction:direction + 1, lanes])
        a = jnp.exp2(half_k[direction][nb] * tanh_r + half_k[direction][nb])
        b = jnp.sqrt(jnp.maximum(1.0 - a * a, 0.0)) * (i * xn)
        return a, b

    def chunk_scan(a, b, carry, reverse):
        order = range(steps - 1, -1, -1) if reverse else range(steps)
        local, decay = [None] * steps, [None] * steps
        h, p = None, None
        for j in order:
            aj, bj = a[j * SUBLANES:(j + 1) * SUBLANES], b[j * SUBLANES:(j + 1) * SUBLANES]
            h = bj if h is None else aj * h + bj
            p = aj if p is None else aj * p
            local[j], decay[j] = h, p
        seg_h, seg_p = h, p
        for dist in (1, 2, 4):
            if reverse:
                edge, shift = sublane >= SUBLANES - dist, SUBLANES - dist
            else:
                edge, shift = sublane < dist, dist
            h_prev = jnp.where(edge, 0.0, pltpu.roll(seg_h, shift, 0))
            p_prev = jnp.where(edge, 1.0, pltpu.roll(seg_p, shift, 0))
            seg_h = seg_p * h_prev + seg_h
            seg_p = seg_p * p_prev
        seg_state = seg_h + seg_p * carry
        if reverse:
            entering = jnp.where(sublane == SUBLANES - 1, carry, pltpu.roll(seg_state, SUBLANES - 1, 0))
            leaving = jnp.broadcast_to(seg_state[0:1], (SUBLANES, LANES))
        else:
            entering = jnp.where(sublane == 0, carry, pltpu.roll(seg_state, 1, 0))
            leaving = jnp.broadcast_to(seg_state[SUBLANES - 1:SUBLANES], (SUBLANES, LANES))
        return [local[j] + decay[j] * entering for j in range(steps)], leaving

    def forward_chunk(c, carry):
        r = pl.multiple_of(c * tc, tc)
        new_carry = []
        for nb in range(n_lane_blocks):
            lanes = slice(nb * LANES, (nb + 1) * LANES)
            a, b = gates(xc_ref[nb, pl.ds(r, tc), :], 0, nb)
            states, leaving = chunk_scan(a, b, carry[nb], reverse=False)
            for j in range(steps):
                hs_ref[nb, pl.ds(r + j * SUBLANES, SUBLANES), :] = states[j]
            new_carry.append(leaving)
        return tuple(new_carry)

    h0f = tuple(jnp.broadcast_to(h0_ref[0:1, nb * LANES:(nb + 1) * LANES], (SUBLANES, LANES))
                for nb in range(n_lane_blocks))
    hf_last = lax.fori_loop(0, n_chunks, forward_chunk, h0f, unroll=2)

    def backward_chunk(c, carry):
        r = pl.multiple_of((n_chunks - 1 - c) * tc, tc)
        new_carry = []
        for nb in range(n_lane_blocks):
            lanes = slice(nb * LANES, (nb + 1) * LANES)
            a, b = gates(xc_ref[nb, pl.ds(r, tc), :], 1, nb)
            states, leaving = chunk_scan(a, b, carry[nb], reverse=True)
            for j in range(steps):
                hs_ref[nb, pl.ds(r + j * SUBLANES, SUBLANES), :] += states[j]
            new_carry.append(leaving)
        return tuple(new_carry)

    h0b = tuple(jnp.broadcast_to(h0_ref[1:2, nb * LANES:(nb + 1) * LANES], (SUBLANES, LANES))
                for nb in range(n_lane_blocks))
    hb_first = lax.fori_loop(0, n_chunks, backward_chunk, h0b, unroll=2)

    for nb in range(n_lane_blocks):
        lanes = slice(nb * LANES, (nb + 1) * LANES)
        hfin_ref[0:1, lanes] = hf_last[nb][0:1]
        hfin_ref[1:2, lanes] = hb_first[nb][0:1]

    def gate_chunk(c, carry):
        r = pl.multiple_of(c * tc, tc)
        for nb in range(n_lane_blocks):
            lanes = slice(nb * LANES, (nb + 1) * LANES)
            for m in range(0, steps, 2):
                rows = pl.ds(r + m * SUBLANES, BF16_ROWS)
                hsum = jnp.concatenate([hs_ref[nb, natural_rows(r, m), :], hs_ref[nb, natural_rows(r, m + 1), :]],
                                       axis=0)
                o_ref[rows, lanes] = (hsum * jax.nn.gelu(zg_ref[rows, lanes].astype(F32))).astype(o_ref.dtype)
        return carry

    lax.fori_loop(0, n_chunks, gate_chunk, 0)


def _rglru(z, conv_w, conv_b, wa, wx, ba, bx, lam, h0, *, n_seq, t, width, x_col, g_col, name):
    cw = _row_tile(width, 512)
    tc = _row_tile(t, 128)
    lb = cw // LANES
    assert wa.shape[-1] == LANES and x_col % cw == 0 and g_col % cw == 0
    vec_spec = pl.BlockSpec((2, cw), lambda s, c: (0, c))
    w_spec = pl.BlockSpec((2, lb, LANES, LANES), lambda s, c: (0, c, 0, 0))
    return pl.pallas_call(
        functools.partial(_lru_kernel, t=t, tc=tc),
        grid=(n_seq, width // cw),
        in_specs=[
            pl.BlockSpec((t, cw), lambda s, c: (s, x_col // cw + c)),
            pl.BlockSpec((t, cw), lambda s, c: (s, g_col // cw + c)),
            pl.BlockSpec((CONV_W, cw), lambda s, c: (0, c)),
            pl.BlockSpec((1, cw), lambda s, c: (0, c)),
            w_spec, w_spec, vec_spec, vec_spec, vec_spec,
            pl.BlockSpec((None, 2, cw), lambda s, c: (s, 0, c)),
        ],
        out_specs=[
            pl.BlockSpec((t, cw), lambda s, c: (s, c)),
            pl.BlockSpec((None, 2, cw), lambda s, c: (s, 0, c)),
        ],
        out_shape=[
            jax.ShapeDtypeStruct((n_seq * t, width), BF16),
            jax.ShapeDtypeStruct((n_seq, 2, width), F32),
        ],
        scratch_shapes=[
            pltpu.VMEM((t + 2 * SUBLANES, cw), F32),
            pltpu.VMEM((lb, t, LANES), F32),
            pltpu.VMEM((lb, t, LANES), F32),
        ],
        compiler_params=_params("parallel", "parallel"),
        name=name,
    )(z, z, conv_w, conv_b.reshape(1, width), wa, wx, ba, bx, lam, h0)


def _cmix_kernel(zu_ref, zv_ref, g_ref, ws_ref, bs_ref, o_ref, v_ref, *, n_groups):
    rows, width = zu_ref.shape
    gw = width // n_groups
    def norm_chunk(c, carry):
        r = pl.ds(pl.multiple_of(c * ROW_CHUNK, ROW_CHUNK), ROW_CHUNK)
        v_ref[r, :] = _rms(jax.nn.gelu(zv_ref[r, :].astype(F32)), g_ref[...]).astype(BF16)
        return carry

    lax.fori_loop(0, rows // ROW_CHUNK, norm_chunk, 0, unroll=NORM_UNROLL)
    for ch in range(rows // CHUNK):
        r = slice(ch * CHUNK, (ch + 1) * CHUNK)
        for g in range(n_groups):
            cols = slice(g * gw, (g + 1) * gw)
            mixed = jnp.dot(ws_ref[g], v_ref[r, cols], preferred_element_type=F32) + bs_ref[g]
            o_ref[r, cols] = (jax.nn.gelu(zu_ref[r, cols].astype(F32)) * mixed).astype(o_ref.dtype)


def _chunk_mix(z, cm_g, ws, bs, *, width, u_col, v_col, name):
    m = z.shape[0]
    n_groups = ws.shape[0]
    tr = _row_tile(m, 2 * CHUNK)
    assert u_col % width == 0 and v_col % width == 0
    return pl.pallas_call(
        functools.partial(_cmix_kernel, n_groups=n_groups),
        grid=(m // tr,),
        in_specs=[
            pl.BlockSpec((tr, width), lambda i: (i, u_col // width)),
            pl.BlockSpec((tr, width), lambda i: (i, v_col // width)),
            pl.BlockSpec((1, width), lambda i: (0, 0)),
            pl.BlockSpec(ws.shape, lambda i: (0, 0, 0)),
            pl.BlockSpec(bs.shape, lambda i: (0, 0, 0)),
        ],
        out_specs=pl.BlockSpec((tr, width), lambda i: (i, 0)),
        out_shape=jax.ShapeDtypeStruct((m, width), BF16),
        scratch_shapes=[pltpu.VMEM((tr, width), BF16)],
        compiler_params=_params("parallel"),
        name=name,
    )(z, z, cm_g.reshape(1, width), ws, bs)


def _rope_tables(n_tokens):
    rows = n_tokens // GRID_W
    pos_row = jnp.repeat(jnp.arange(rows), GRID_W).astype(F32)
    pos_col = (jnp.arange(n_tokens) % GRID_W).astype(F32)
    inv = ROPE_BASE ** (-jnp.arange(ROPE_FREQS, dtype=F32) / ROPE_FREQS)
    cr, sr = jnp.cos(pos_row[:, None] * inv), jnp.sin(pos_row[:, None] * inv)
    cc, sc = jnp.cos(pos_col[:, None] * inv), jnp.sin(pos_col[:, None] * inv)
    return (jnp.concatenate([cr, cr, cc, cc], axis=-1), jnp.concatenate([-sr, sr, -sc, sc], axis=-1))


def _layer(x, mod, p, tag, *, n_seq, t, rows_per_cond, rope=None, past_k=None, past_v=None, layer=None, h0=None):
    d = x.shape[1]
    kv_w = p["kv_w"]
    col_lx, col_lg, col_cu, col_cv, col_g, col_k = d, 2 * d, 3 * d, 4 * d, 5 * d, 8 * d
    col_v = col_k + kv_w

    tn = 2 * kv_w
    assert d % tn == 0
    q_blocks, rest_blocks = d // tn, 7 * d // tn

    def w_col_block(j):
        return jnp.where(j < q_blocks, j, jnp.where(j < q_blocks + rest_blocks, j + 1, q_blocks))

    li = p["layer"]
    z = _prenorm_matmul(x, p["g_pre_mix"], mod, p["w_in"], li, rows_per_cond=rows_per_cond,
                        shift_row=MOD_SHIFT1, scale_row=MOD_SCALE1, relu2=False, name=f"in_proj_{tag}",
                        tn=tn, w_col_block=w_col_block)
    attn = _attention(z, p["g_q"], p["g_k"], n_seq=n_seq, t=t, d=d, kv_w=kv_w, k_col=col_k, v_col=col_v,
                      rope=rope, past_k=past_k, past_v=past_v, layer=layer, name=f"attention_{tag}")
    if rope is None:
        attn_o, k_new, v_new = attn
    else:
        attn_o, k_new, v_new = attn, None, None
    lru_o, h_fin = _rglru(z, p["conv_w"], p["conv_b"], p["lru_wa"], p["lru_wx"], p["lru_ba"], p["lru_bx"],
                          p["lru_lam"], h0, n_seq=n_seq, t=t, width=d, x_col=col_lx, g_col=col_lg,
                          name=f"rglru_{tag}")
    cm_o = _chunk_mix(z, p["cm_g"], p["cm_ws"], p["cm_bs"], width=d, u_col=col_cu, v_col=col_cv,
                      name=f"chunk_mix_{tag}")
    merged = _merge((attn_o, lru_o, cm_o), (p["w_attn_out"], p["w_lru_out"], p["w_cm_out"]), li, z, col_g,
                    name=f"merge_{tag}")
    x = _matmul_norm_residual(merged, p["w_out"], li, p["g_post_mix"], mod, x, rows_per_cond=rows_per_cond,
                              gate_row=MOD_GATE1, name=f"out_proj_{tag}")
    f = _prenorm_matmul(x, p["g_pre_ff"], mod, p["w_ff1"], li, rows_per_cond=rows_per_cond,
                        shift_row=MOD_SHIFT2, scale_row=MOD_SCALE2, relu2=True, name=f"ff1_{tag}")
    x = _matmul_norm_residual(f, p["w_ff2"], li, p["g_post_ff"], mod, x, rows_per_cond=rows_per_cond,
                              gate_row=MOD_GATE2, name=f"ff2_{tag}")
    return x, k_new, v_new, h_fin


def kernel(x_prompt, x_sample, cache_k, cache_v, state_lru, c, c_ctx, w_mod, b_mod, g_pre_mix, g_post_mix, g_pre_ff, g_post_ff, w_in, g_q, g_k, w_attn_out, conv_w, conv_b, lru_wa, lru_ba, lru_wx, lru_bx, lru_lam, w_lru_out, cm_g, cm_ws, cm_bs, w_cm_out, w_out, w_ff1, w_ff2):
    batch, seq, d = x_prompt.shape
    dec_batch, dec_seq, _ = x_sample.shape
    depth = w_in.shape[0]
    in_w = w_in.shape[2]
    kv_w = (in_w - 8 * d) // 2
    n_kv = kv_w // HEAD_DIM
    past = cache_k.shape[2]
    n_groups = cm_ws.shape[1]

    n_cond = 1 + dec_batch
    cond = jnp.concatenate([c_ctx[None, :], c], axis=0)
    cond = jnp.pad(cond, ((0, -n_cond % SUBLANES), (0, 0)))
    mod = _modulation(cond, w_mod, b_mod)

    w_in_b = w_in.astype(BF16)
    w_attn_out_b, w_lru_out_b, w_cm_out_b = w_attn_out.astype(BF16), w_lru_out.astype(BF16), w_cm_out.astype(BF16)
    w_out_b, w_ff1_b, w_ff2_b = w_out.astype(BF16), w_ff1.astype(BF16), w_ff2.astype(BF16)
    lru_wa_b, lru_wx_b, cm_ws_b = lru_wa.astype(BF16), lru_wx.astype(BF16), cm_ws.astype(BF16)
    cm_bs_b = jnp.broadcast_to(cm_bs[:, :, :, None], cm_bs.shape + (d // n_groups,))

    rope = _rope_tables(dec_seq)
    past_k = cache_k.reshape(dec_batch, depth, past, kv_w)
    past_v = cache_v.reshape(dec_batch, depth, past, kv_w)
    h0_prompt = jnp.zeros((batch, 2, d), F32)

    y_p = x_prompt.reshape(batch * seq, d)
    y_s = x_sample.reshape(dec_batch * dec_seq, d)
    new_k, new_v, new_s = [], [], []
    for l in range(depth):
        p = {
            "kv_w": kv_w, "layer": l, "g_pre_mix": g_pre_mix[l], "g_post_mix": g_post_mix[l],
            "g_pre_ff": g_pre_ff[l], "g_post_ff": g_post_ff[l], "w_in": w_in_b, "g_q": g_q[l], "g_k": g_k[l],
            "w_attn_out": w_attn_out_b, "conv_w": conv_w[l], "conv_b": conv_b[l],
            "lru_wa": lru_wa_b[l], "lru_ba": lru_ba[l], "lru_wx": lru_wx_b[l], "lru_bx": lru_bx[l],
            "lru_lam": lru_lam[l], "w_lru_out": w_lru_out_b, "cm_g": cm_g[l], "cm_ws": cm_ws_b[l],
            "cm_bs": cm_bs_b[l], "w_cm_out": w_cm_out_b, "w_out": w_out_b, "w_ff1": w_ff1_b,
            "w_ff2": w_ff2_b,
        }
        y_p, k_l, v_l, s_l = _layer(y_p, mod[l, 0:1], p, f"ctx{l}", n_seq=batch, t=seq,
                                    rows_per_cond=batch * seq, h0=h0_prompt)
        new_k.append(k_l.reshape(batch, seq, n_kv, HEAD_DIM))
        new_v.append(v_l.reshape(batch, seq, n_kv, HEAD_DIM))
        new_s.append(s_l)
        y_s, _, _, _ = _layer(y_s, mod[l, 1:1 + dec_batch], p, f"lat{l}", n_seq=dec_batch, t=dec_seq,
                              rows_per_cond=dec_seq, rope=rope, past_k=past_k, past_v=past_v, layer=l,
                              h0=state_lru[:, l])
    return (y_p.reshape(batch, seq, d), y_s.reshape(dec_batch, dec_seq, d),
            jnp.stack(new_k, axis=1), jnp.stack(new_v, axis=1), jnp.stack(new_s, axis=1))
```

```python
import functools

import jax
import jax.numpy as jnp
from jax import lax
from jax.experimental import pallas as pl
from jax.experimental.pallas import tpu as pltpu

F32 = jnp.float32
BF16 = jnp.bfloat16

EPS = 1e-6
HEAD_DIM = 128
GRID_W = 64
ROPE_BASE = 10000.0
ROPE_FREQS = HEAD_DIM // 4
CONV_W = 4
CONV_PAD_L = 2
LRU_C = 8.0
CHUNK = 128
N_MOD = 6
N_BRANCH = 3
LOG2_E = 1.4426950408889634

LANES = 128
SUBLANES = 8
VMEM_LIMIT_BYTES = 56 * 1024 * 1024
BF16_ROWS = 2 * SUBLANES
ROW_CHUNK = BF16_ROWS
SCORE_BUFFERS = 3
PROB_BUFFERS = 2
NORM_UNROLL = 8
VREG_BUDGET_ELEMS = 32 * SUBLANES * LANES

MOD_SHIFT1, MOD_SCALE1, MOD_GATE1, MOD_SHIFT2, MOD_SCALE2, MOD_GATE2 = range(N_MOD)


def _params(*semantics):
    return pltpu.CompilerParams(dimension_semantics=semantics, vmem_limit_bytes=VMEM_LIMIT_BYTES)


def _rms(x, g):
    return x * lax.rsqrt(jnp.mean(x * x, axis=-1, keepdims=True) + EPS) * g


def _sigmoid(x):
    return 0.5 * jnp.tanh(0.5 * x) + 0.5


def _row_tile(m, cap):
    if m <= cap:
        return m
    t = cap - cap % LANES
    while m % t:
        t -= LANES
    assert t > 0
    return t


def _mod_kernel(c_ref, w_ref, b_ref, o_ref):
    c = c_ref[...]
    s = (c * jax.nn.sigmoid(c)).astype(BF16)
    o_ref[...] = jnp.dot(s, w_ref[...].astype(BF16), preferred_element_type=F32) + b_ref[...]


def _modulation(cond, w_mod, b_mod):
    nc, d = cond.shape
    depth, _, n = w_mod.shape
    tn = _row_tile(n, 1024)
    out = pl.pallas_call(
        _mod_kernel,
        grid=(depth, n // tn),
        in_specs=[
            pl.BlockSpec((nc, d), lambda l, j: (0, 0)),
            pl.BlockSpec((None, d, tn), lambda l, j: (l, 0, j)),
            pl.BlockSpec((None, 1, tn), lambda l, j: (l, 0, j)),
        ],
        out_specs=pl.BlockSpec((None, nc, tn), lambda l, j: (l, 0, j)),
        out_shape=jax.ShapeDtypeStruct((depth, nc, n), F32),
        compiler_params=_params("parallel", "parallel"),
        name="modulation",
    )(cond, w_mod, b_mod.reshape(depth, 1, n))
    return out.reshape(depth, nc, N_MOD, d)


def _prenorm_mm_kernel(x_ref, g_ref, mod_ref, w_ref, o_ref, h_ref, *, shift_row, scale_row, relu2,
                       rows_per_cond, n_tiles, n_slices):
    i, j = pl.program_id(0), pl.program_id(1)
    tm = x_ref.shape[0]

    def gain_shift(tile):
        cond = (tile * tm) // rows_per_cond
        return (g_ref[...] * (1.0 + mod_ref[cond, scale_row:scale_row + 1, :]),
                mod_ref[cond, shift_row:shift_row + 1, :])

    def normalise(rows, slot, gain, shift):
        x = x_ref[rows, :]
        inv = lax.rsqrt(jnp.mean(x * x, axis=-1, keepdims=True) + EPS)
        h_ref[slot, rows, :] = (x * inv * gain + shift).astype(BF16)

    @pl.when(jnp.logical_and(i == 0, j == 0))
    def _():
        gain, shift = gain_shift(0)

        def chunk(c, carry):
            normalise(pl.ds(pl.multiple_of(c * ROW_CHUNK, ROW_CHUNK), ROW_CHUNK), 0, gain, shift)
            return carry

        lax.fori_loop(0, tm // ROW_CHUNK, chunk, 0, unroll=NORM_UNROLL)

    slice_rows = tm // n_slices
    gain, shift = gain_shift(jnp.minimum(i + 1, n_tiles - 1))

    def normalise_slice(s):
        row0 = pl.multiple_of(s * slice_rows, slice_rows)
        for c in range(slice_rows // ROW_CHUNK):
            normalise(pl.ds(row0 + c * ROW_CHUNK, ROW_CHUNK), (i + 1) % 2, gain, shift)

    @pl.when(jnp.logical_and(i == 0, j == 1))
    def _():
        normalise_slice(0)

    normalise_slice(jnp.minimum(j, n_slices - 1))

    acc = jnp.dot(h_ref[i % 2], w_ref[...], preferred_element_type=F32)
    if relu2:
        acc = jnp.square(jnp.maximum(acc, 0.0))
    o_ref[...] = acc.astype(o_ref.dtype)


def _prenorm_matmul(x, g, mod, w, layer, *, rows_per_cond, shift_row, scale_row, relu2, name, tn=None,
                    w_col_block=lambda j: j):
    m, d = x.shape
    n = w.shape[2]
    tm = _row_tile(rows_per_cond, 1024)
    tn = tn or _row_tile(n, 1024)
    n_tiles, nj = m // tm, n // tn
    assert rows_per_cond % tm == 0 and n % tn == 0 and nj >= 2
    n_slices = max(s for s in range(1, nj + 1) if (tm // ROW_CHUNK) % s == 0)

    def x_tile(i, j):
        return jnp.where(jnp.logical_and(i == 0, j == 0), 0, jnp.minimum(i + 1, n_tiles - 1))

    return pl.pallas_call(
        functools.partial(_prenorm_mm_kernel, shift_row=shift_row, scale_row=scale_row, relu2=relu2,
                          rows_per_cond=rows_per_cond, n_tiles=n_tiles, n_slices=n_slices),
        grid=(n_tiles, nj),
        in_specs=[
            pl.BlockSpec((tm, d), lambda i, j: (x_tile(i, j), 0)),
            pl.BlockSpec((1, d), lambda i, j: (0, 0)),
            pl.BlockSpec(mod.shape, lambda i, j: (0, 0, 0)),
            pl.BlockSpec((None, d, tn), lambda i, j: (layer, 0, w_col_block(j))),
        ],
        out_specs=pl.BlockSpec((tm, tn), lambda i, j: (i, j)),
        out_shape=jax.ShapeDtypeStruct((m, n), BF16),
        scratch_shapes=[pltpu.VMEM((2, tm, d), BF16)],
        compiler_params=_params("arbitrary", "arbitrary"),
        name=name,
    )(x, g.reshape(1, d), mod, w)


def _mm_norm_res_kernel(a_ref, w_ref, g_ref, mod_ref, x_ref, o_ref, *acc_refs, gate_row, nk):
    def product():
        return jnp.dot(a_ref[...], w_ref[...], preferred_element_type=F32)

    def finish(y_ref):
        gain = mod_ref[gate_row:gate_row + 1, :] * g_ref[...]

        def chunk(c, carry):
            rows = pl.ds(pl.multiple_of(c * ROW_CHUNK, ROW_CHUNK), ROW_CHUNK)
            y = y_ref[rows, :]
            inv = lax.rsqrt(jnp.mean(y * y, axis=-1, keepdims=True) + EPS)
            o_ref[rows, :] = x_ref[rows, :] + y * inv * gain
            return carry

        lax.fori_loop(0, o_ref.shape[0] // ROW_CHUNK, chunk, 0, unroll=NORM_UNROLL)

    if nk == 1:
        o_ref[...] = product()
        finish(o_ref)
        return
    acc_ref, = acc_refs
    k = pl.program_id(1)

    @pl.when(k == 0)
    def _():
        acc_ref[...] = product()

    @pl.when(k > 0)
    def _():
        acc_ref[...] += product()

    @pl.when(k == nk - 1)
    def _():
        finish(acc_ref)


def _matmul_norm_residual(a, w, layer, g, mod, x, *, rows_per_cond, gate_row, name):
    m, kdim = a.shape
    d = w.shape[2]
    tm = _row_tile(rows_per_cond, 512)
    tk = _row_tile(kdim, 2048)
    nk = kdim // tk
    assert rows_per_cond % tm == 0
    return pl.pallas_call(
        functools.partial(_mm_norm_res_kernel, gate_row=gate_row, nk=nk),
        grid=(m // tm, nk),
        in_specs=[
            pl.BlockSpec((tm, tk), lambda i, k: (i, k)),
            pl.BlockSpec((None, tk, d), lambda i, k: (layer, k, 0)),
            pl.BlockSpec((1, d), lambda i, k: (0, 0)),
            pl.BlockSpec((None, N_MOD, d), lambda i, k: ((i * tm) // rows_per_cond, 0, 0)),
            pl.BlockSpec((tm, d), lambda i, k: (i, 0)),
        ],
        out_specs=pl.BlockSpec((tm, d), lambda i, k: (i, 0)),
        out_shape=jax.ShapeDtypeStruct((m, d), F32),
        scratch_shapes=[pltpu.VMEM((tm, d), F32)] if nk > 1 else [],
        compiler_params=_params("parallel", "arbitrary"),
        name=name,
    )(a, w, g.reshape(1, d), mod, x)


def _merge_kernel(a0_ref, a1_ref, a2_ref, w0_ref, w1_ref, w2_ref, g0_ref, g1_ref, g2_ref, o_ref):
    acc = _sigmoid(g0_ref[...].astype(F32)) * jnp.dot(a0_ref[...], w0_ref[...], preferred_element_type=F32)
    acc += _sigmoid(g1_ref[...].astype(F32)) * jnp.dot(a1_ref[...], w1_ref[...], preferred_element_type=F32)
    acc += _sigmoid(g2_ref[...].astype(F32)) * jnp.dot(a2_ref[...], w2_ref[...], preferred_element_type=F32)
    o_ref[...] = acc.astype(o_ref.dtype)


def _merge(branches, weights, layer, z, gate_col0, name):
    m, d = branches[0].shape
    tm = _row_tile(m, 512)
    tn = _row_tile(d, 1024)
    assert gate_col0 % tn == 0 and d % tn == 0
    a_spec = pl.BlockSpec((tm, d), lambda i, j: (i, 0))
    w_spec = pl.BlockSpec((None, d, tn), lambda i, j: (layer, 0, j))

    def gate_spec(b):
        off = (gate_col0 + b * d) // tn
        return pl.BlockSpec((tm, tn), lambda i, j: (i, off + j))

    return pl.pallas_call(
        _merge_kernel,
        grid=(m // tm, d // tn),
        in_specs=[a_spec] * N_BRANCH + [w_spec] * N_BRANCH + [gate_spec(b) for b in range(N_BRANCH)],
        out_specs=pl.BlockSpec((tm, tn), lambda i, j: (i, j)),
        out_shape=jax.ShapeDtypeStruct((m, d), BF16),
        compiler_params=_params("parallel", "arbitrary"),
        name=name,
    )(*branches, *weights, z, z, z)


def _rope(x, c, s):
    lane = lax.broadcasted_iota(jnp.int32, x.shape, 1)
    partner = jnp.where((lane % (2 * ROPE_FREQS)) < ROPE_FREQS,
                        pltpu.roll(x, HEAD_DIM - ROPE_FREQS, 1),
                        pltpu.roll(x, ROPE_FREQS, 1))
    return x * c + partner * s


def _attn_kernel(*refs, has_ctx, t, tq, q_group, unit_rows, softmax_rows):
    if has_ctx:
        (zq_ref, zk_ref, zv_ref, gq_ref, gk_ref, cq_ref, sq_ref, ck_ref, sk_ref, pk_ref, pv_ref,
         o_ref, kt_ref, v_ref, *buffers) = refs
    else:
        zq_ref, zk_ref, zv_ref, gq_ref, gk_ref, o_ref, ko_ref, vo_ref, kt_ref, v_ref, *buffers = refs
    s_refs, p_refs = buffers[:SCORE_BUFFERS], buffers[SCORE_BUFFERS:]

    @pl.when(pl.program_id(2) == 0)
    def _():
        k = _rms(zk_ref[...].astype(F32), gk_ref[...])
        v_ref[:, HEAD_DIM:] = jnp.ones((v_ref.shape[0], HEAD_DIM), BF16)
        if has_ctx:
            k = _rope(k, ck_ref[...], sk_ref[...])
            kt_ref[:, 0:t] = k.T.astype(BF16)
            kt_ref[:, t:] = pk_ref[...].T.astype(BF16)
            v_ref[0:t, 0:HEAD_DIM] = zv_ref[...]
            v_ref[t:, 0:HEAD_DIM] = pv_ref[...].astype(BF16)
        else:
            ko_ref[...] = k
            vo_ref[...] = zv_ref[...].astype(F32)
            kt_ref[...] = k.T.astype(BF16)
            v_ref[:, 0:HEAD_DIM] = zv_ref[...]

    q_scale = LOG2_E * HEAD_DIM ** -0.5

    units = [(g, r0) for r0 in range(0, tq, unit_rows) for g in range(q_group)]

    def scores(u):
        g, r0 = units[u]
        rows = slice(r0, r0 + unit_rows)
        q = _rms(zq_ref[rows, g * HEAD_DIM:(g + 1) * HEAD_DIM].astype(F32), gq_ref[...])
        if has_ctx:
            q = _rope(q, cq_ref[rows, :], sq_ref[rows, :])
        s_refs[u % SCORE_BUFFERS][...] = jnp.dot((q * q_scale).astype(BF16), kt_ref[...],
                                                 preferred_element_type=F32)

    def exponentials(u):
        s_ref, p_ref = s_refs[u % SCORE_BUFFERS], p_refs[u % PROB_BUFFERS]
        groups = [slice(r0, r0 + softmax_rows) for r0 in range(0, unit_rows, softmax_rows)]
        maxima = [jnp.max(s_ref[rows, :], axis=-1, keepdims=True) for rows in groups]
        for rows, m in zip(groups, maxima):
            p_ref[rows, :] = jnp.exp2(s_ref[rows, :] - m).astype(BF16)

    def weighted_values(u):
        g, r0 = units[u]
        o = jnp.dot(p_refs[u % PROB_BUFFERS][...], v_ref[...], preferred_element_type=F32)
        o_ref[r0:r0 + unit_rows, g * HEAD_DIM:(g + 1) * HEAD_DIM] = (
            o[:, :HEAD_DIM] * (1.0 / o[:, HEAD_DIM:])).astype(o_ref.dtype)

    lookahead = SCORE_BUFFERS - 1
    for u in range(min(lookahead, len(units))):
        scores(u)
    for u in range(len(units)):
        if u + lookahead < len(units):
            scores(u + lookahead)
        exponentials(u)
        weighted_values(u)


def _attention(z, g_q, g_k, *, n_seq, t, d, kv_w, k_col, v_col, rope=None, past_k=None, past_v=None, layer=None, name):
    has_ctx = rope is not None
    n_kv = kv_w // HEAD_DIM
    q_group = d // kv_w
    qw = q_group * HEAD_DIM
    past = past_k.shape[2] if has_ctx else 0
    tk = t + past
    tq = _row_tile(t, 512)
    nq = t // tq
    unit_rows = _row_tile(tq, 128)
    softmax_rows = min(unit_rows, max(BF16_ROWS, (VREG_BUDGET_ELEMS // tk) // BF16_ROWS * BF16_ROWS))
    assert unit_rows % softmax_rows == 0

    in_specs = [
        pl.BlockSpec((tq, qw), lambda b, h, i: (b * nq + i, h)),
        pl.BlockSpec((t, HEAD_DIM), lambda b, h, i: (b, k_col // HEAD_DIM + h)),
        pl.BlockSpec((t, HEAD_DIM), lambda b, h, i: (b, v_col // HEAD_DIM + h)),
        pl.BlockSpec((1, HEAD_DIM), lambda b, h, i: (0, 0)),
        pl.BlockSpec((1, HEAD_DIM), lambda b, h, i: (0, 0)),
    ]
    args = [z, z, z, g_q.reshape(1, HEAD_DIM), g_k.reshape(1, HEAD_DIM)]
    o_spec = pl.BlockSpec((tq, qw), lambda b, h, i: (b * nq + i, h))
    o_shape = jax.ShapeDtypeStruct((n_seq * t, d), BF16)
    if has_ctx:
        cos, sin = rope
        in_specs += [
            pl.BlockSpec((tq, HEAD_DIM), lambda b, h, i: (i, 0)),
            pl.BlockSpec((tq, HEAD_DIM), lambda b, h, i: (i, 0)),
            pl.BlockSpec((t, HEAD_DIM), lambda b, h, i: (0, 0)),
            pl.BlockSpec((t, HEAD_DIM), lambda b, h, i: (0, 0)),
            pl.BlockSpec((None, None, past, HEAD_DIM), lambda b, h, i: (b, layer, 0, h)),
            pl.BlockSpec((None, None, past, HEAD_DIM), lambda b, h, i: (b, layer, 0, h)),
        ]
        args += [cos, sin, cos, sin, past_k, past_v]
        out_specs = o_spec
        out_shape = o_shape
    else:
        kv_spec = pl.BlockSpec((None, t, HEAD_DIM), lambda b, h, i: (b, 0, h))
        kv_shape = jax.ShapeDtypeStruct((n_seq, t, kv_w), F32)
        out_specs = [o_spec, kv_spec, kv_spec]
        out_shape = [o_shape, kv_shape, kv_shape]

    return pl.pallas_call(
        functools.partial(_attn_kernel, has_ctx=has_ctx, t=t, tq=tq, q_group=q_group, unit_rows=unit_rows,
                          softmax_rows=softmax_rows),
        grid=(n_seq, n_kv, nq),
        in_specs=in_specs,
        out_specs=out_specs,
        out_shape=out_shape,
        scratch_shapes=[
            pltpu.VMEM((HEAD_DIM, tk), BF16),
            pltpu.VMEM((tk, 2 * HEAD_DIM), BF16),
            *[pltpu.VMEM((unit_rows, tk), F32)] * SCORE_BUFFERS,
            *[pltpu.VMEM((unit_rows, tk), BF16)] * PROB_BUFFERS,
        ],
        compiler_params=_params("parallel", "parallel", "arbitrary"),
        name=name,
    )(*args)


def _lru_kernel(zx_ref, zg_ref, cw_ref, cb_ref, wa_ref, wx_ref, ba_ref, bx_ref, lam_ref, h0_ref,
                o_ref, hfin_ref, xpad_ref, xc_ref, hs_ref, *, t, tc):
    cw = zx_ref.shape[1]
    n_lane_blocks = cw // LANES
    n_chunks = t // tc
    pad = SUBLANES
    steps = tc // SUBLANES
    assert steps % SUBLANES == 0

    def natural_rows(r, m):
        seg, j0 = divmod(m * SUBLANES, steps)
        return pl.ds(r + SUBLANES * j0 + seg, SUBLANES, stride=SUBLANES)

    xpad_ref[0:pad, :] = jnp.zeros((pad, cw), F32)
    xpad_ref[t + pad:t + 2 * pad, :] = jnp.zeros((pad, cw), F32)

    def copy_chunk(c, carry):
        r = pl.multiple_of(c * tc, tc)
        xpad_ref[pl.ds(r + pad, tc), :] = zx_ref[pl.ds(r, tc), :].astype(F32)
        return carry

    lax.fori_loop(0, n_chunks, copy_chunk, 0)

    def conv_chunk(c, carry):
        r = pl.multiple_of(c * tc, tc)
        xe = xpad_ref[pl.ds(r, tc + 2 * pad), :]
        n = tc + 2 * pad
        y = cb_ref[...] + pltpu.roll(xe, 2, 0)[pad:pad + tc] * cw_ref[0:1, :]
        y = y + pltpu.roll(xe, 1, 0)[pad:pad + tc] * cw_ref[1:2, :]
        y = y + xe[pad:pad + tc] * cw_ref[2:3, :]
        y = y + pltpu.roll(xe, n - 1, 0)[pad:pad + tc] * cw_ref[3:4, :]
        for nb in range(n_lane_blocks):
            for m in range(steps):
                xc_ref[nb, natural_rows(r, m), :] = y[m * SUBLANES:(m + 1) * SUBLANES, nb * LANES:(nb + 1) * LANES]
        return carry

    lax.fori_loop(0, n_chunks, conv_chunk, 0)

    sublane = lax.broadcasted_iota(jnp.int32, (SUBLANES, LANES), 0)

    half_k = [[(-0.5 * LRU_C * LOG2_E) * jax.nn.softplus(-lam_ref[direction:direction + 1, nb * LANES:(nb + 1) * LANES])
               for nb in range(n_lane_blocks)] for direction in range(2)]

    def gates(xn, direction, nb):
        lanes = slice(nb * LANES, (nb + 1) * LANES)
        xb = xn.astype(BF16)
        tanh_r = jnp.tanh(0.5 * (jnp.dot(xb, wa_ref[direction, nb], preferred_element_type=F32)
                                 + ba_ref[direction:direction + 1, lanes]))
        i = _sigmoid(jnp.dot(xb, wx_ref[direction, nb], preferred_element_type=F32)
                     + bx_ref[direction:direction + 1, lanes])
        a = jnp.exp2(half_k[direction][nb] * tanh_r + half_k[direction][nb])
        b = jnp.sqrt(jnp.maximum(1.0 - a * a, 0.0)) * (i * xn)
        return a, b

    def chunk_scan(a, b, carry, reverse):
        order = range(steps - 1, -1, -1) if reverse else range(steps)
        local, decay = [None] * steps, [None] * steps
        h, p = None, None
        for j in order:
            aj, bj = a[j * SUBLANES:(j + 1) * SUBLANES], b[j * SUBLANES:(j + 1) * SUBLANES]
            h = bj if h is None else aj * h + bj
            p = aj if p is None else aj * p
            local[j], decay[j] = h, p
        seg_h, seg_p = h, p
        for dist in (1, 2, 4):
            if reverse:
                edge, shift = sublane >= SUBLANES - dist, SUBLANES - dist
            else:
                edge, shift = sublane < dist, dist
            h_prev = jnp.where(edge, 0.0, pltpu.roll(seg_h, shift, 0))
            p_prev = jnp.where(edge, 1.0, pltpu.roll(seg_p, shift, 0))
            seg_h = seg_p * h_prev + seg_h
            seg_p = seg_p * p_prev
        seg_state = seg_h + seg_p * carry
        if reverse:
            entering = jnp.where(sublane == SUBLANES - 1, carry, pltpu.roll(seg_state, SUBLANES - 1, 0))
            leaving = jnp.broadcast_to(seg_state[0:1], (SUBLANES, LANES))
        else:
            entering = jnp.where(sublane == 0, carry, pltpu.roll(seg_state, 1, 0))
            leaving = jnp.broadcast_to(seg_state[SUBLANES - 1:SUBLANES], (SUBLANES, LANES))
        return [local[j] + decay[j] * entering for j in range(steps)], leaving

    def forward_chunk(c, carry):
        r = pl.multiple_of(c * tc, tc)
        new_carry = []
        for nb in range(n_lane_blocks):
            lanes = slice(nb * LANES, (nb + 1) * LANES)
            a, b = gates(xc_ref[nb, pl.ds(r, tc), :], 0, nb)
            states, leaving = chunk_scan(a, b, carry[nb], reverse=False)
            for j in range(steps):
                hs_ref[nb, pl.ds(r + j * SUBLANES, SUBLANES), :] = states[j]
            new_carry.append(leaving)
        return tuple(new_carry)

    h0f = tuple(jnp.broadcast_to(h0_ref[0:1, nb * LANES:(nb + 1) * LANES], (SUBLANES, LANES))
                for nb in range(n_lane_blocks))
    hf_last = lax.fori_loop(0, n_chunks, forward_chunk, h0f, unroll=2)

    def backward_chunk(c, carry):
        r = pl.multiple_of((n_chunks - 1 - c) * tc, tc)
        new_carry = []
        for nb in range(n_lane_blocks):
            lanes = slice(nb * LANES, (nb + 1) * LANES)
            a, b = gates(xc_ref[nb, pl.ds(r, tc), :], 1, nb)
            states, leaving = chunk_scan(a, b, carry[nb], reverse=True)
            for j in range(steps):
                hs_ref[nb, pl.ds(r + j * SUBLANES, SUBLANES), :] += states[j]
            new_carry.append(leaving)
        return tuple(new_carry)

    h0b = tuple(jnp.broadcast_to(h0_ref[1:2, nb * LANES:(nb + 1) * LANES], (SUBLANES, LANES))
                for nb in range(n_lane_blocks))
    hb_first = lax.fori_loop(0, n_chunks, backward_chunk, h0b, unroll=2)

    for nb in range(n_lane_blocks):
        lanes = slice(nb * LANES, (nb + 1) * LANES)
        hfin_ref[0:1, lanes] = hf_last[nb][0:1]
        hfin_ref[1:2, lanes] = hb_first[nb][0:1]

    def gate_chunk(c, carry):
        r = pl.multiple_of(c * tc, tc)
        for nb in range(n_lane_blocks):
            lanes = slice(nb * LANES, (nb + 1) * LANES)
            for m in range(0, steps, 2):
                rows = pl.ds(r + m * SUBLANES, BF16_ROWS)
                hsum = jnp.concatenate([hs_ref[nb, natural_rows(r, m), :], hs_ref[nb, natural_rows(r, m + 1), :]],
                                       axis=0)
                o_ref[rows, lanes] = (hsum * jax.nn.gelu(zg_ref[rows, lanes].astype(F32))).astype(o_ref.dtype)
        return carry

    lax.fori_loop(0, n_chunks, gate_chunk, 0)


def _rglru(z, conv_w, conv_b, wa, wx, ba, bx, lam, h0, *, n_seq, t, width, x_col, g_col, name):
    cw = _row_tile(width, 512)
    tc = _row_tile(t, 128)
    lb = cw // LANES
    assert wa.shape[-1] == LANES and x_col % cw == 0 and g_col % cw == 0
    vec_spec = pl.BlockSpec((2, cw), lambda s, c: (0, c))
    w_spec = pl.BlockSpec((2, lb, LANES, LANES), lambda s, c: (0, c, 0, 0))
    return pl.pallas_call(
        functools.partial(_lru_kernel, t=t, tc=tc),
        grid=(n_seq, width // cw),
        in_specs=[
            pl.BlockSpec((t, cw), lambda s, c: (s, x_col // cw + c)),
            pl.BlockSpec((t, cw), lambda s, c: (s, g_col // cw + c)),
            pl.BlockSpec((CONV_W, cw), lambda s, c: (0, c)),
            pl.BlockSpec((1, cw), lambda s, c: (0, c)),
            w_spec, w_spec, vec_spec, vec_spec, vec_spec,
            pl.BlockSpec((None, 2, cw), lambda s, c: (s, 0, c)),
        ],
        out_specs=[
            pl.BlockSpec((t, cw), lambda s, c: (s, c)),
            pl.BlockSpec((None, 2, cw), lambda s, c: (s, 0, c)),
        ],
        out_shape=[
            jax.ShapeDtypeStruct((n_seq * t, width), BF16),
            jax.ShapeDtypeStruct((n_seq, 2, width), F32),
        ],
        scratch_shapes=[
            pltpu.VMEM((t + 2 * SUBLANES, cw), F32),
            pltpu.VMEM((lb, t, LANES), F32),
            pltpu.VMEM((lb, t, LANES), F32),
        ],
        compiler_params=_params("parallel", "parallel"),
        name=name,
    )(z, z, conv_w, conv_b.reshape(1, width), wa, wx, ba, bx, lam, h0)


def _cmix_kernel(zu_ref, zv_ref, g_ref, ws_ref, bs_ref, o_ref, v_ref, *, n_groups):
    rows, width = zu_ref.shape
    gw = width // n_groups
    def norm_chunk(c, carry):
        r = pl.ds(pl.multiple_of(c * ROW_CHUNK, ROW_CHUNK), ROW_CHUNK)
        v_ref[r, :] = _rms(jax.nn.gelu(zv_ref[r, :].astype(F32)), g_ref[...]).astype(BF16)
        return carry

    lax.fori_loop(0, rows // ROW_CHUNK, norm_chunk, 0, unroll=NORM_UNROLL)
    for ch in range(rows // CHUNK):
        r = slice(ch * CHUNK, (ch + 1) * CHUNK)
        for g in range(n_groups):
            cols = slice(g * gw, (g + 1) * gw)
            mixed = jnp.dot(ws_ref[g], v_ref[r, cols], preferred_element_type=F32) + bs_ref[g]
            o_ref[r, cols] = (jax.nn.gelu(zu_ref[r, cols].astype(F32)) * mixed).astype(o_ref.dtype)


def _chunk_mix(z, cm_g, ws, bs, *, width, u_col, v_col, name):
    m = z.shape[0]
    n_groups = ws.shape[0]
    tr = _row_tile(m, 2 * CHUNK)
    assert u_col % width == 0 and v_col % width == 0
    return pl.pallas_call(
        functools.partial(_cmix_kernel, n_groups=n_groups),
        grid=(m // tr,),
        in_specs=[
            pl.BlockSpec((tr, width), lambda i: (i, u_col // width)),
            pl.BlockSpec((tr, width), lambda i: (i, v_col // width)),
            pl.BlockSpec((1, width), lambda i: (0, 0)),
            pl.BlockSpec(ws.shape, lambda i: (0, 0, 0)),
            pl.BlockSpec(bs.shape, lambda i: (0, 0, 0)),
        ],
        out_specs=pl.BlockSpec((tr, width), lambda i: (i, 0)),
        out_shape=jax.ShapeDtypeStruct((m, width), BF16),
        scratch_shapes=[pltpu.VMEM((tr, width), BF16)],
        compiler_params=_params("parallel"),
        name=name,
    )(z, z, cm_g.reshape(1, width), ws, bs)


def _rope_tables(n_tokens):
    rows = n_tokens // GRID_W
    pos_row = jnp.repeat(jnp.arange(rows), GRID_W).astype(F32)
    pos_col = (jnp.arange(n_tokens) % GRID_W).astype(F32)
    inv = ROPE_BASE ** (-jnp.arange(ROPE_FREQS, dtype=F32) / ROPE_FREQS)
    cr, sr = jnp.cos(pos_row[:, None] * inv), jnp.sin(pos_row[:, None] * inv)
    cc, sc = jnp.cos(pos_col[:, None] * inv), jnp.sin(pos_col[:, None] * inv)
    return (jnp.concatenate([cr, cr, cc, cc], axis=-1), jnp.concatenate([-sr, sr, -sc, sc], axis=-1))


def _layer(x, mod, p, tag, *, n_seq, t, rows_per_cond, rope=None, past_k=None, past_v=None, layer=None, h0=None):
    d = x.shape[1]
    kv_w = p["kv_w"]
    col_lx, col_lg, col_cu, col_cv, col_g, col_k = d, 2 * d, 3 * d, 4 * d, 5 * d, 8 * d
    col_v = col_k + kv_w

    tn = 2 * kv_w
    assert d % tn == 0
    q_blocks, rest_blocks = d // tn, 7 * d // tn

    def w_col_block(j):
        return jnp.where(j < q_blocks, j, jnp.where(j < q_blocks + rest_blocks, j + 1, q_blocks))

    li = p["layer"]
    z = _prenorm_matmul(x, p["g_pre_mix"], mod, p["w_in"], li, rows_per_cond=rows_per_cond,
                        shift_row=MOD_SHIFT1, scale_row=MOD_SCALE1, relu2=False, name=f"in_proj_{tag}",
                        tn=tn, w_col_block=w_col_block)
    attn = _attention(z, p["g_q"], p["g_k"], n_seq=n_seq, t=t, d=d, kv_w=kv_w, k_col=col_k, v_col=col_v,
                      rope=rope, past_k=past_k, past_v=past_v, layer=layer, name=f"attention_{tag}")
    if rope is None:
        attn_o, k_new, v_new = attn
    else:
        attn_o, k_new, v_new = attn, None, None
    lru_o, h_fin = _rglru(z, p["conv_w"], p["conv_b"], p["lru_wa"], p["lru_wx"], p["lru_ba"], p["lru_bx"],
                          p["lru_lam"], h0, n_seq=n_seq, t=t, width=d, x_col=col_lx, g_col=col_lg,
                          name=f"rglru_{tag}")
    cm_o = _chunk_mix(z, p["cm_g"], p["cm_ws"], p["cm_bs"], width=d, u_col=col_cu, v_col=col_cv,
                      name=f"chunk_mix_{tag}")
    merged = _merge((attn_o, lru_o, cm_o), (p["w_attn_out"], p["w_lru_out"], p["w_cm_out"]), li, z, col_g,
                    name=f"merge_{tag}")
    x = _matmul_norm_residual(merged, p["w_out"], li, p["g_post_mix"], mod, x, rows_per_cond=rows_per_cond,
                              gate_row=MOD_GATE1, name=f"out_proj_{tag}")
    f = _prenorm_matmul(x, p["g_pre_ff"], mod, p["w_ff1"], li, rows_per_cond=rows_per_cond,
                        shift_row=MOD_SHIFT2, scale_row=MOD_SCALE2, relu2=True, name=f"ff1_{tag}")
    x = _matmul_norm_residual(f, p["w_ff2"], li, p["g_post_ff"], mod, x, rows_per_cond=rows_per_cond,
                              gate_row=MOD_GATE2, name=f"ff2_{tag}")
    return x, k_new, v_new, h_fin


def kernel(x_prompt, x_sample, cache_k, cache_v, state_lru, c, c_ctx, w_mod, b_mod, g_pre_mix, g_post_mix, g_pre_ff, g_post_ff, w_in, g_q, g_k, w_attn_out, conv_w, conv_b, lru_wa, lru_ba, lru_wx, lru_bx, lru_lam, w_lru_out, cm_g, cm_ws, cm_bs, w_cm_out, w_out, w_ff1, w_ff2):
    batch, seq, d = x_prompt.shape
    dec_batch, dec_seq, _ = x_sample.shape
    depth = w_in.shape[0]
    in_w = w_in.shape[2]
    kv_w = (in_w - 8 * d) // 2
    n_kv = kv_w // HEAD_DIM
    past = cache_k.shape[2]
    n_groups = cm_ws.shape[1]

    n_cond = 1 + dec_batch
    cond = jnp.concatenate([c_ctx[None, :], c], axis=0)
    cond = jnp.pad(cond, ((0, -n_cond % SUBLANES), (0, 0)))
    mod = _modulation(cond, w_mod, b_mod)

    w_in_b = w_in.astype(BF16)
    w_attn_out_b, w_lru_out_b, w_cm_out_b = w_attn_out.astype(BF16), w_lru_out.astype(BF16), w_cm_out.astype(BF16)
    w_out_b, w_ff1_b, w_ff2_b = w_out.astype(BF16), w_ff1.astype(BF16), w_ff2.astype(BF16)
    lru_wa_b, lru_wx_b, cm_ws_b = lru_wa.astype(BF16), lru_wx.astype(BF16), cm_ws.astype(BF16)
    cm_bs_b = jnp.broadcast_to(cm_bs[:, :, :, None], cm_bs.shape + (d // n_groups,))

    rope = _rope_tables(dec_seq)
    past_k = cache_k.reshape(dec_batch, depth, past, kv_w)
    past_v = cache_v.reshape(dec_batch, depth, past, kv_w)
    h0_prompt = jnp.zeros((batch, 2, d), F32)

    y_p = x_prompt.reshape(batch * seq, d)
    y_s = x_sample.reshape(dec_batch * dec_seq, d)
    new_k, new_v, new_s = [], [], []
    for l in range(depth):
        p = {
            "kv_w": kv_w, "layer": l, "g_pre_mix": g_pre_mix[l], "g_post_mix": g_post_mix[l],
            "g_pre_ff": g_pre_ff[l], "g_post_ff": g_post_ff[l], "w_in": w_in_b, "g_q": g_q[l], "g_k": g_k[l],
            "w_attn_out": w_attn_out_b, "conv_w": conv_w[l], "conv_b": conv_b[l],
            "lru_wa": lru_wa_b[l], "lru_ba": lru_ba[l], "lru_wx": lru_wx_b[l], "lru_bx": lru_bx[l],
            "lru_lam": lru_lam[l], "w_lru_out": w_lru_out_b, "cm_g": cm_g[l], "cm_ws": cm_ws_b[l],
            "cm_bs": cm_bs_b[l], "w_cm_out": w_cm_out_b, "w_out": w_out_b, "w_ff1": w_ff1_b,
            "w_ff2": w_ff2_b,
        }
        y_p, k_l, v_l, s_l = _layer(y_p, mod[l, 0:1], p, f"ctx{l}", n_seq=batch, t=seq,
                                    rows_per_cond=batch * seq, h0=h0_prompt)
        new_k.append(k_l.reshape(batch, seq, n_kv, HEAD_DIM))
        new_v.append(v_l.reshape(batch, seq, n_kv, HEAD_DIM))
        new_s.append(s_l)
        y_s, _, _, _ = _layer(y_s, mod[l, 1:1 + dec_batch], p, f"lat{l}", n_seq=dec_batch, t=dec_seq,
                              rows_per_cond=dec_seq, rope=rope, past_k=past_k, past_v=past_v, layer=l,
                              h0=state_lru[:, l])
    return (y_p.reshape(batch, seq, d), y_s.reshape(dec_batch, dec_seq, d),
            jnp.stack(new_k, axis=1), jnp.stack(new_v, axis=1), jnp.stack(new_s, axis=1))
```

```python
import functools

import jax
import jax.numpy as jnp
from jax import lax
from jax.experimental import pallas as pl
from jax.experimental.pallas import tpu as pltpu

F32 = jnp.float32
BF16 = jnp.bfloat16

EPS = 1e-6
HEAD_DIM = 128
GRID_W = 64
ROPE_BASE = 10000.0
ROPE_FREQS = HEAD_DIM // 4
CONV_W = 4
CONV_PAD_L = 2
LRU_C = 8.0
CHUNK = 128
N_MOD = 6
N_BRANCH = 3
LOG2_E = 1.4426950408889634

LANES = 128
SUBLANES = 8
VMEM_LIMIT_BYTES = 56 * 1024 * 1024
BF16_ROWS = 2 * SUBLANES
ROW_CHUNK = BF16_ROWS
SCORE_BUFFERS = 3
PROB_BUFFERS = 2
NORM_UNROLL = 8
VREG_BUDGET_ELEMS = 32 * SUBLANES * LANES

MOD_SHIFT1, MOD_SCALE1, MOD_GATE1, MOD_SHIFT2, MOD_SCALE2, MOD_GATE2 = range(N_MOD)


def _params(*semantics):
    return pltpu.CompilerParams(dimension_semantics=semantics, vmem_limit_bytes=VMEM_LIMIT_BYTES)


def _rms(x, g):
    return x * lax.rsqrt(jnp.mean(x * x, axis=-1, keepdims=True) + EPS) * g


def _sigmoid(x):
    return 0.5 * jnp.tanh(0.5 * x) + 0.5


def _row_tile(m, cap):
    if m <= cap:
        return m
    t = cap - cap % LANES
    while m % t:
        t -= LANES
    assert t > 0
    return t


def _mod_kernel(c_ref, w_ref, b_ref, o_ref):
    c = c_ref[...]
    s = (c * jax.nn.sigmoid(c)).astype(BF16)
    o_ref[...] = jnp.dot(s, w_ref[...].astype(BF16), preferred_element_type=F32) + b_ref[...]


def _modulation(cond, w_mod, b_mod):
    nc, d = cond.shape
    depth, _, n = w_mod.shape
    tn = _row_tile(n, 1024)
    out = pl.pallas_call(
        _mod_kernel,
        grid=(depth, n // tn),
        in_specs=[
            pl.BlockSpec((nc, d), lambda l, j: (0, 0)),
            pl.BlockSpec((None, d, tn), lambda l, j: (l, 0, j)),
            pl.BlockSpec((None, 1, tn), lambda l, j: (l, 0, j)),
        ],
        out_specs=pl.BlockSpec((None, nc, tn), lambda l, j: (l, 0, j)),
        out_shape=jax.ShapeDtypeStruct((depth, nc, n), F32),
        compiler_params=_params("parallel", "parallel"),
        name="modulation",
    )(cond, w_mod, b_mod.reshape(depth, 1, n))
    return out.reshape(depth, nc, N_MOD, d)


def _prenorm_mm_kernel(x_ref, g_ref, mod_ref, w_ref, o_ref, h_ref, *, shift_row, scale_row, relu2,
                       rows_per_cond, n_tiles, n_slices):
    i, j = pl.program_id(0), pl.program_id(1)
    tm = x_ref.shape[0]

    def gain_shift(tile):
        cond = (tile * tm) // rows_per_cond
        return (g_ref[...] * (1.0 + mod_ref[cond, scale_row:scale_row + 1, :]),
                mod_ref[cond, shift_row:shift_row + 1, :])

    def normalise(rows, slot, gain, shift):
        x = x_ref[rows, :]
        inv = lax.rsqrt(jnp.mean(x * x, axis=-1, keepdims=True) + EPS)
        h_ref[slot, rows, :] = (x * inv * gain + shift).astype(BF16)

    @pl.when(jnp.logical_and(i == 0, j == 0))
    def _():
        gain, shift = gain_shift(0)

        def chunk(c, carry):
            normalise(pl.ds(pl.multiple_of(c * ROW_CHUNK, ROW_CHUNK), ROW_CHUNK), 0, gain, shift)
            return carry

        lax.fori_loop(0, tm // ROW_CHUNK, chunk, 0, unroll=NORM_UNROLL)

    slice_rows = tm // n_slices
    gain, shift = gain_shift(jnp.minimum(i + 1, n_tiles - 1))

    def normalise_slice(s):
        row0 = pl.multiple_of(s * slice_rows, slice_rows)
        for c in range(slice_rows // ROW_CHUNK):
            normalise(pl.ds(row0 + c * ROW_CHUNK, ROW_CHUNK), (i + 1) % 2, gain, shift)

    @pl.when(jnp.logical_and(i == 0, j == 1))
    def _():
        normalise_slice(0)

    normalise_slice(jnp.minimum(j, n_slices - 1))

    acc = jnp.dot(h_ref[i % 2], w_ref[...], preferred_element_type=F32)
    if relu2:
        acc = jnp.square(jnp.maximum(acc, 0.0))
    o_ref[...] = acc.astype(o_ref.dtype)


def _prenorm_matmul(x, g, mod, w, layer, *, rows_per_cond, shift_row, scale_row, relu2, name, tn=None,
                    w_col_block=lambda j: j):
    m, d = x.shape
    n = w.shape[2]
    tm = _row_tile(rows_per_cond, 1024)
    tn = tn or _row_tile(n, 1024)
    n_tiles, nj = m // tm, n // tn
    assert rows_per_cond % tm == 0 and n % tn == 0 and nj >= 2
    n_slices = max(s for s in range(1, nj + 1) if (tm // ROW_CHUNK) % s == 0)

    def x_tile(i, j):
        return jnp.where(jnp.logical_and(i == 0, j == 0), 0, jnp.minimum(i + 1, n_tiles - 1))

    return pl.pallas_call(
        functools.partial(_prenorm_mm_kernel, shift_row=shift_row, scale_row=scale_row, relu2=relu2,
                          rows_per_cond=rows_per_cond, n_tiles=n_tiles, n_slices=n_slices),
        grid=(n_tiles, nj),
        in_specs=[
            pl.BlockSpec((tm, d), lambda i, j: (x_tile(i, j), 0)),
            pl.BlockSpec((1, d), lambda i, j: (0, 0)),
            pl.BlockSpec(mod.shape, lambda i, j: (0, 0, 0)),
            pl.BlockSpec((None, d, tn), lambda i, j: (layer, 0, w_col_block(j))),
        ],
        out_specs=pl.BlockSpec((tm, tn), lambda i, j: (i, j)),
        out_shape=jax.ShapeDtypeStruct((m, n), BF16),
        scratch_shapes=[pltpu.VMEM((2, tm, d), BF16)],
        compiler_params=_params("arbitrary", "arbitrary"),
        name=name,
    )(x, g.reshape(1, d), mod, w)


def _mm_norm_res_kernel(a_ref, w_ref, g_ref, mod_ref, x_ref, o_ref, acc_ref, *, gate_row, nk):
    def product():
        return jnp.dot(a_ref[...], w_ref[...], preferred_element_type=F32)

    def finish():
        gain = mod_ref[gate_row:gate_row + 1, :] * g_ref[...]

        def chunk(c, carry):
            rows = pl.ds(pl.multiple_of(c * ROW_CHUNK, ROW_CHUNK), ROW_CHUNK)
            y = acc_ref[rows, :]
            inv = lax.rsqrt(jnp.mean(y * y, axis=-1, keepdims=True) + EPS)
            o_ref[rows, :] = x_ref[rows, :] + y * inv * gain
            return carry

        lax.fori_loop(0, o_ref.shape[0] // ROW_CHUNK, chunk, 0, unroll=NORM_UNROLL)

    if nk == 1:
        acc_ref[...] = product()
        finish()
        return
    k = pl.program_id(1)

    @pl.when(k == 0)
    def _():
        acc_ref[...] = product()

    @pl.when(k > 0)
    def _():
        acc_ref[...] += product()

    @pl.when(k == nk - 1)
    def _():
        finish()


def _matmul_norm_residual(a, w, layer, g, mod, x, *, rows_per_cond, gate_row, name):
    m, kdim = a.shape
    d = w.shape[2]
    tk = _row_tile(kdim, 2048)
    nk = kdim // tk
    tm = _row_tile(rows_per_cond, 512)
    assert rows_per_cond % tm == 0
    return pl.pallas_call(
        functools.partial(_mm_norm_res_kernel, gate_row=gate_row, nk=nk),
        grid=(m // tm, nk),
        in_specs=[
            pl.BlockSpec((tm, tk), lambda i, k: (i, k)),
            pl.BlockSpec((None, tk, d), lambda i, k: (layer, k, 0)),
            pl.BlockSpec((1, d), lambda i, k: (0, 0)),
            pl.BlockSpec((None, N_MOD, d), lambda i, k: ((i * tm) // rows_per_cond, 0, 0)),
            pl.BlockSpec((tm, d), lambda i, k: (i, 0)),
        ],
        out_specs=pl.BlockSpec((tm, d), lambda i, k: (i, 0)),
        out_shape=jax.ShapeDtypeStruct((m, d), F32),
        scratch_shapes=[pltpu.VMEM((tm, d), F32)],
        compiler_params=_params("parallel", "arbitrary"),
        name=name,
    )(a, w, g.reshape(1, d), mod, x)


def _merge_kernel(a0_ref, a1_ref, a2_ref, w0_ref, w1_ref, w2_ref, g0_ref, g1_ref, g2_ref, o_ref):
    acc = _sigmoid(g0_ref[...].astype(F32)) * jnp.dot(a0_ref[...], w0_ref[...], preferred_element_type=F32)
    acc += _sigmoid(g1_ref[...].astype(F32)) * jnp.dot(a1_ref[...], w1_ref[...], preferred_element_type=F32)
    acc += _sigmoid(g2_ref[...].astype(F32)) * jnp.dot(a2_ref[...], w2_ref[...], preferred_element_type=F32)
    o_ref[...] = acc.astype(o_ref.dtype)


def _merge(branches, weights, layer, z, gate_col0, name):
    m, d = branches[0].shape
    tm = _row_tile(m, 512)
    tn = _row_tile(d, 1024)
    assert gate_col0 % tn == 0 and d % tn == 0
    a_spec = pl.BlockSpec((tm, d), lambda i, j: (i, 0))
    w_spec = pl.BlockSpec((None, d, tn), lambda i, j: (layer, 0, j))

    def gate_spec(b):
        off = (gate_col0 + b * d) // tn
        return pl.BlockSpec((tm, tn), lambda i, j: (i, off + j))

    return pl.pallas_call(
        _merge_kernel,
        grid=(m // tm, d // tn),
        in_specs=[a_spec] * N_BRANCH + [w_spec] * N_BRANCH + [gate_spec(b) for b in range(N_BRANCH)],
        out_specs=pl.BlockSpec((tm, tn), lambda i, j: (i, j)),
        out_shape=jax.ShapeDtypeStruct((m, d), BF16),
        compiler_params=_params("parallel", "arbitrary"),
        name=name,
    )(*branches, *weights, z, z, z)


def _rope(x, c, s):
    lane = lax.broadcasted_iota(jnp.int32, x.shape, 1)
    partner = jnp.where((lane % (2 * ROPE_FREQS)) < ROPE_FREQS,
                        pltpu.roll(x, HEAD_DIM - ROPE_FREQS, 1),
                        pltpu.roll(x, ROPE_FREQS, 1))
    return x * c + partner * s


def _attn_kernel(*refs, has_ctx, t, tq, q_group, unit_rows, softmax_rows):
    if has_ctx:
        (zq_ref, zk_ref, zv_ref, gq_ref, gk_ref, cq_ref, sq_ref, ck_ref, sk_ref, pk_ref, pv_ref,
         o_ref, kt_ref, v_ref, *buffers) = refs
    else:
        zq_ref, zk_ref, zv_ref, gq_ref, gk_ref, o_ref, ko_ref, vo_ref, kt_ref, v_ref, *buffers = refs
    s_refs, p_refs = buffers[:SCORE_BUFFERS], buffers[SCORE_BUFFERS:]

    @pl.when(pl.program_id(2) == 0)
    def _():
        k = _rms(zk_ref[...].astype(F32), gk_ref[...])
        v_ref[:, HEAD_DIM:] = jnp.ones((v_ref.shape[0], HEAD_DIM), BF16)
        if has_ctx:
            k = _rope(k, ck_ref[...], sk_ref[...])
            kt_ref[:, 0:t] = k.T.astype(BF16)
            kt_ref[:, t:] = pk_ref[...].T.astype(BF16)
            v_ref[0:t, 0:HEAD_DIM] = zv_ref[...]
            v_ref[t:, 0:HEAD_DIM] = pv_ref[...].astype(BF16)
        else:
            ko_ref[...] = k
            vo_ref[...] = zv_ref[...].astype(F32)
            kt_ref[...] = k.T.astype(BF16)
            v_ref[:, 0:HEAD_DIM] = zv_ref[...]

    q_scale = LOG2_E * HEAD_DIM ** -0.5

    units = [(g, r0) for r0 in range(0, tq, unit_rows) for g in range(q_group)]

    def scores(u):
        g, r0 = units[u]
        rows = slice(r0, r0 + unit_rows)
        q = _rms(zq_ref[rows, g * HEAD_DIM:(g + 1) * HEAD_DIM].astype(F32), gq_ref[...])
        if has_ctx:
            q = _rope(q, cq_ref[rows, :], sq_ref[rows, :])
        s_refs[u % SCORE_BUFFERS][...] = jnp.dot((q * q_scale).astype(BF16), kt_ref[...],
                                                 preferred_element_type=F32)

    def exponentials(u):
        s_ref, p_ref = s_refs[u % SCORE_BUFFERS], p_refs[u % PROB_BUFFERS]
        groups = [slice(r0, r0 + softmax_rows) for r0 in range(0, unit_rows, softmax_rows)]
        maxima = [jnp.max(s_ref[rows, :], axis=-1, keepdims=True) for rows in groups]
        for rows, m in zip(groups, maxima):
            p_ref[rows, :] = jnp.exp2(s_ref[rows, :] - m).astype(BF16)

    def weighted_values(u):
        g, r0 = units[u]
        o = jnp.dot(p_refs[u % PROB_BUFFERS][...], v_ref[...], preferred_element_type=F32)
        o_ref[r0:r0 + unit_rows, g * HEAD_DIM:(g + 1) * HEAD_DIM] = (
            o[:, :HEAD_DIM] * (1.0 / o[:, HEAD_DIM:])).astype(o_ref.dtype)

    lookahead = SCORE_BUFFERS - 1
    for u in range(min(lookahead, len(units))):
        scores(u)
    for u in range(len(units)):
        if u + lookahead < len(units):
            scores(u + lookahead)
        exponentials(u)
        weighted_values(u)


def _attention(z, g_q, g_k, *, n_seq, t, d, kv_w, k_col, v_col, rope=None, past_k=None, past_v=None, layer=None, name):
    has_ctx = rope is not None
    n_kv = kv_w // HEAD_DIM
    q_group = d // kv_w
    qw = q_group * HEAD_DIM
    past = past_k.shape[2] if has_ctx else 0
    tk = t + past
    tq = _row_tile(t, 512)
    nq = t // tq
    unit_rows = _row_tile(tq, 128)
    softmax_rows = min(unit_rows, max(BF16_ROWS, (VREG_BUDGET_ELEMS // tk) // BF16_ROWS * BF16_ROWS))
    assert unit_rows % softmax_rows == 0

    in_specs = [
        pl.BlockSpec((tq, qw), lambda b, h, i: (b * nq + i, h)),
        pl.BlockSpec((t, HEAD_DIM), lambda b, h, i: (b, k_col // HEAD_DIM + h)),
        pl.BlockSpec((t, HEAD_DIM), lambda b, h, i: (b, v_col // HEAD_DIM + h)),
        pl.BlockSpec((1, HEAD_DIM), lambda b, h, i: (0, 0)),
        pl.BlockSpec((1, HEAD_DIM), lambda b, h, i: (0, 0)),
    ]
    args = [z, z, z, g_q.reshape(1, HEAD_DIM), g_k.reshape(1, HEAD_DIM)]
    o_spec = pl.BlockSpec((tq, qw), lambda b, h, i: (b * nq + i, h))
    o_shape = jax.ShapeDtypeStruct((n_seq * t, d), BF16)
    if has_ctx:
        cos, sin = rope
        in_specs += [
            pl.BlockSpec((tq, HEAD_DIM), lambda b, h, i: (i, 0)),
            pl.BlockSpec((tq, HEAD_DIM), lambda b, h, i: (i, 0)),
            pl.BlockSpec((t, HEAD_DIM), lambda b, h, i: (0, 0)),
            pl.BlockSpec((t, HEAD_DIM), lambda b, h, i: (0, 0)),
            pl.BlockSpec((None, None, past, HEAD_DIM), lambda b, h, i: (b, layer, 0, h)),
            pl.BlockSpec((None, None, past, HEAD_DIM), lambda b, h, i: (b, layer, 0, h)),
        ]
        args += [cos, sin, cos, sin, past_k, past_v]
        out_specs = o_spec
        out_shape = o_shape
    else:
        kv_spec = pl.BlockSpec((None, t, HEAD_DIM), lambda b, h, i: (b, 0, h))
        kv_shape = jax.ShapeDtypeStruct((n_seq, t, kv_w), F32)
        out_specs = [o_spec, kv_spec, kv_spec]
        out_shape = [o_shape, kv_shape, kv_shape]

    return pl.pallas_call(
        functools.partial(_attn_kernel, has_ctx=has_ctx, t=t, tq=tq, q_group=q_group, unit_rows=unit_rows,
                          softmax_rows=softmax_rows),
        grid=(n_seq, n_kv, nq),
        in_specs=in_specs,
        out_specs=out_specs,
        out_shape=out_shape,
        scratch_shapes=[
            pltpu.VMEM((HEAD_DIM, tk), BF16),
            pltpu.VMEM((tk, 2 * HEAD_DIM), BF16),
            *[pltpu.VMEM((unit_rows, tk), F32)] * SCORE_BUFFERS,
            *[pltpu.VMEM((unit_rows, tk), BF16)] * PROB_BUFFERS,
        ],
        compiler_params=_params("parallel", "parallel", "arbitrary"),
        name=name,
    )(*args)


def _lru_kernel(zx_ref, zg_ref, cw_ref, cb_ref, wa_ref, wx_ref, ba_ref, bx_ref, lam_ref, h0_ref,
                o_ref, hfin_ref, xpad_ref, xc_ref, hs_ref, *, t, tc):
    cw = zx_ref.shape[1]
    n_lane_blocks = cw // LANES
    n_chunks = t // tc
    pad = SUBLANES
    steps = tc // SUBLANES
    assert steps % SUBLANES == 0

    def natural_rows(r, m):
        seg, j0 = divmod(m * SUBLANES, steps)
        return pl.ds(r + SUBLANES * j0 + seg, SUBLANES, stride=SUBLANES)

    xpad_ref[0:pad, :] = jnp.zeros((pad, cw), F32)
    xpad_ref[t + pad:t + 2 * pad, :] = jnp.zeros((pad, cw), F32)

    def copy_chunk(c, carry):
        r = pl.multiple_of(c * tc, tc)
        xpad_ref[pl.ds(r + pad, tc), :] = zx_ref[pl.ds(r, tc), :].astype(F32)
        return carry

    lax.fori_loop(0, n_chunks, copy_chunk, 0)

    def conv_chunk(c, carry):
        r = pl.multiple_of(c * tc, tc)
        xe = xpad_ref[pl.ds(r, tc + 2 * pad), :]
        n = tc + 2 * pad
        y = cb_ref[...] + pltpu.roll(xe, 2, 0)[pad:pad + tc] * cw_ref[0:1, :]
        y = y + pltpu.roll(xe, 1, 0)[pad:pad + tc] * cw_ref[1:2, :]
        y = y + xe[pad:pad + tc] * cw_ref[2:3, :]
        y = y + pltpu.roll(xe, n - 1, 0)[pad:pad + tc] * cw_ref[3:4, :]
        for nb in range(n_lane_blocks):
            for m in range(steps):
                xc_ref[nb, natural_rows(r, m), :] = y[m * SUBLANES:(m + 1) * SUBLANES, nb * LANES:(nb + 1) * LANES]
        return carry

    lax.fori_loop(0, n_chunks, conv_chunk, 0)

    sublane = lax.broadcasted_iota(jnp.int32, (SUBLANES, LANES), 0)

    half_k = [[(-0.5 * LRU_C * LOG2_E) * jax.nn.softplus(-lam_ref[direction:direction + 1, nb * LANES:(nb + 1) * LANES])
               for nb in range(n_lane_blocks)] for direction in range(2)]

    half_ba = [[0.5 * ba_ref[direction:direction + 1, nb * LANES:(nb + 1) * LANES]
                for nb in range(n_lane_blocks)] for direction in range(2)]
    half_bx = [[0.5 * bx_ref[direction:direction + 1, nb * LANES:(nb + 1) * LANES]
                for nb in range(n_lane_blocks)] for direction in range(2)]

    def gates(xn, direction, nb):
        xb = xn.astype(BF16)
        tanh_r = jnp.tanh(jnp.dot(xb, wa_ref[direction, nb], preferred_element_type=F32) + half_ba[direction][nb])
        tanh_i = jnp.tanh(jnp.dot(xb, wx_ref[direction, nb], preferred_element_type=F32) + half_bx[direction][nb])
        a = jnp.exp2(half_k[direction][nb] * tanh_r + half_k[direction][nb])
        scale = jnp.exp2((0.5 * LOG2_E) * jnp.log(jnp.maximum(1.0 - a * a, 0.0)))
        b = scale * ((0.5 * tanh_i + 0.5) * xn)
        return a, b

    def chunk_scan(a, b, carry, reverse):
        order = range(steps - 1, -1, -1) if reverse else range(steps)
        local, decay = [None] * steps, [None] * steps
        h, p = None, None
        for j in order:
            aj, bj = a[j * SUBLANES:(j + 1) * SUBLANES], b[j * SUBLANES:(j + 1) * SUBLANES]
            h = bj if h is None else aj * h + bj
            p = aj if p is None else aj * p
            local[j], decay[j] = h, p
        seg_h, seg_p = h, p
        for dist in (1, 2, 4):
            if reverse:
                edge, shift = sublane >= SUBLANES - dist, SUBLANES - dist
            else:
                edge, shift = sublane < dist, dist
            h_prev = jnp.where(edge, 0.0, pltpu.roll(seg_h, shift, 0))
            p_prev = jnp.where(edge, 1.0, pltpu.roll(seg_p, shift, 0))
            seg_h = seg_p * h_prev + seg_h
            seg_p = seg_p * p_prev
        seg_state = seg_h + seg_p * carry
        if reverse:
            entering = jnp.where(sublane == SUBLANES - 1, carry, pltpu.roll(seg_state, SUBLANES - 1, 0))
            leaving = jnp.broadcast_to(seg_state[0:1], (SUBLANES, LANES))
        else:
            entering = jnp.where(sublane == 0, carry, pltpu.roll(seg_state, 1, 0))
            leaving = jnp.broadcast_to(seg_state[SUBLANES - 1:SUBLANES], (SUBLANES, LANES))
        return [local[j] + decay[j] * entering for j in range(steps)], leaving

    def forward_chunk(c, carry):
        r = pl.multiple_of(c * tc, tc)
        new_carry = []
        for nb in range(n_lane_blocks):
            lanes = slice(nb * LANES, (nb + 1) * LANES)
            a, b = gates(xc_ref[nb, pl.ds(r, tc), :], 0, nb)
            states, leaving = chunk_scan(a, b, carry[nb], reverse=False)
            for j in range(steps):
                hs_ref[nb, pl.ds(r + j * SUBLANES, SUBLANES), :] = states[j]
            new_carry.append(leaving)
        return tuple(new_carry)

    h0f = tuple(jnp.broadcast_to(h0_ref[0:1, nb * LANES:(nb + 1) * LANES], (SUBLANES, LANES))
                for nb in range(n_lane_blocks))
    hf_last = lax.fori_loop(0, n_chunks, forward_chunk, h0f, unroll=2)

    def backward_chunk(c, carry):
        r = pl.multiple_of((n_chunks - 1 - c) * tc, tc)
        new_carry = []
        for nb in range(n_lane_blocks):
            lanes = slice(nb * LANES, (nb + 1) * LANES)
            a, b = gates(xc_ref[nb, pl.ds(r, tc), :], 1, nb)
            states, leaving = chunk_scan(a, b, carry[nb], reverse=True)
            for j in range(steps):
                hs_ref[nb, pl.ds(r + j * SUBLANES, SUBLANES), :] += states[j]
            new_carry.append(leaving)
        return tuple(new_carry)

    h0b = tuple(jnp.broadcast_to(h0_ref[1:2, nb * LANES:(nb + 1) * LANES], (SUBLANES, LANES))
                for nb in range(n_lane_blocks))
    hb_first = lax.fori_loop(0, n_chunks, backward_chunk, h0b, unroll=2)

    for nb in range(n_lane_blocks):
        lanes = slice(nb * LANES, (nb + 1) * LANES)
        hfin_ref[0:1, lanes] = hf_last[nb][0:1]
        hfin_ref[1:2, lanes] = hb_first[nb][0:1]

    def gate_chunk(c, carry):
        r = pl.multiple_of(c * tc, tc)
        for nb in range(n_lane_blocks):
            lanes = slice(nb * LANES, (nb + 1) * LANES)
            for m in range(0, steps, 2):
                rows = pl.ds(r + m * SUBLANES, BF16_ROWS)
                hsum = jnp.concatenate([hs_ref[nb, natural_rows(r, m), :], hs_ref[nb, natural_rows(r, m + 1), :]],
                                       axis=0)
                o_ref[rows, lanes] = (hsum * jax.nn.gelu(zg_ref[rows, lanes].astype(F32))).astype(o_ref.dtype)
        return carry

    lax.fori_loop(0, n_chunks, gate_chunk, 0)


def _rglru(z, conv_w, conv_b, wa, wx, ba, bx, lam, h0, *, n_seq, t, width, x_col, g_col, name):
    cw = _row_tile(width, 512)
    tc = _row_tile(t, 128)
    lb = cw // LANES
    assert wa.shape[-1] == LANES and x_col % cw == 0 and g_col % cw == 0
    vec_spec = pl.BlockSpec((2, cw), lambda s, c: (0, c))
    w_spec = pl.BlockSpec((2, lb, LANES, LANES), lambda s, c: (0, c, 0, 0))
    return pl.pallas_call(
        functools.partial(_lru_kernel, t=t, tc=tc),
        grid=(n_seq, width // cw),
        in_specs=[
            pl.BlockSpec((t, cw), lambda s, c: (s, x_col // cw + c)),
            pl.BlockSpec((t, cw), lambda s, c: (s, g_col // cw + c)),
            pl.BlockSpec((CONV_W, cw), lambda s, c: (0, c)),
            pl.BlockSpec((1, cw), lambda s, c: (0, c)),
            w_spec, w_spec, vec_spec, vec_spec, vec_spec,
            pl.BlockSpec((None, 2, cw), lambda s, c: (s, 0, c)),
        ],
        out_specs=[
            pl.BlockSpec((t, cw), lambda s, c: (s, c)),
            pl.BlockSpec((None, 2, cw), lambda s, c: (s, 0, c)),
        ],
        out_shape=[
            jax.ShapeDtypeStruct((n_seq * t, width), BF16),
            jax.ShapeDtypeStruct((n_seq, 2, width), F32),
        ],
        scratch_shapes=[
            pltpu.VMEM((t + 2 * SUBLANES, cw), F32),
            pltpu.VMEM((lb, t, LANES), F32),
            pltpu.VMEM((lb, t, LANES), F32),
        ],
        compiler_params=_params("parallel", "parallel"),
        name=name,
    )(z, z, conv_w, conv_b.reshape(1, width), wa, wx, ba, bx, lam, h0)


def _cmix_kernel(zu_ref, zv_ref, g_ref, ws_ref, bs_ref, o_ref, v_ref, *, n_groups):
    rows, width = zu_ref.shape
    gw = width // n_groups
    def norm_chunk(c, carry):
        r = pl.ds(pl.multiple_of(c * ROW_CHUNK, ROW_CHUNK), ROW_CHUNK)
        v_ref[r, :] = _rms(jax.nn.gelu(zv_ref[r, :].astype(F32)), g_ref[...]).astype(BF16)
        return carry

    lax.fori_loop(0, rows // ROW_CHUNK, norm_chunk, 0, unroll=NORM_UNROLL)
    for ch in range(rows // CHUNK):
        r = slice(ch * CHUNK, (ch + 1) * CHUNK)
        for g in range(n_groups):
            cols = slice(g * gw, (g + 1) * gw)
            mixed = jnp.dot(ws_ref[g], v_ref[r, cols], preferred_element_type=F32) + bs_ref[g]
            o_ref[r, cols] = (jax.nn.gelu(zu_ref[r, cols].astype(F32)) * mixed).astype(o_ref.dtype)


def _chunk_mix(z, cm_g, ws, bs, *, width, u_col, v_col, name):
    m = z.shape[0]
    n_groups = ws.shape[0]
    tr = _row_tile(m, 2 * CHUNK)
    assert u_col % width == 0 and v_col % width == 0
    return pl.pallas_call(
        functools.partial(_cmix_kernel, n_groups=n_groups),
        grid=(m // tr,),
        in_specs=[
            pl.BlockSpec((tr, width), lambda i: (i, u_col // width)),
            pl.BlockSpec((tr, width), lambda i: (i, v_col // width)),
            pl.BlockSpec((1, width), lambda i: (0, 0)),
            pl.BlockSpec(ws.shape, lambda i: (0, 0, 0)),
            pl.BlockSpec(bs.shape, lambda i: (0, 0, 0)),
        ],
        out_specs=pl.BlockSpec((tr, width), lambda i: (i, 0)),
        out_shape=jax.ShapeDtypeStruct((m, width), BF16),
        scratch_shapes=[pltpu.VMEM((tr, width), BF16)],
        compiler_params=_params("parallel"),
        name=name,
    )(z, z, cm_g.reshape(1, width), ws, bs)


def _rope_tables(n_tokens):
    rows = n_tokens // GRID_W
    pos_row = jnp.repeat(jnp.arange(rows), GRID_W).astype(F32)
    pos_col = (jnp.arange(n_tokens) % GRID_W).astype(F32)
    inv = ROPE_BASE ** (-jnp.arange(ROPE_FREQS, dtype=F32) / ROPE_FREQS)
    cr, sr = jnp.cos(pos_row[:, None] * inv), jnp.sin(pos_row[:, None] * inv)
    cc, sc = jnp.cos(pos_col[:, None] * inv), jnp.sin(pos_col[:, None] * inv)
    return (jnp.concatenate([cr, cr, cc, cc], axis=-1), jnp.concatenate([-sr, sr, -sc, sc], axis=-1))


def _layer(x, mod, p, tag, *, n_seq, t, rows_per_cond, rope=None, past_k=None, past_v=None, layer=None, h0=None):
    d = x.shape[1]
    kv_w = p["kv_w"]
    col_lx, col_lg, col_cu, col_cv, col_g, col_k = d, 2 * d, 3 * d, 4 * d, 5 * d, 8 * d
    col_v = col_k + kv_w

    tn = 2 * kv_w
    assert d % tn == 0
    q_blocks, rest_blocks = d // tn, 7 * d // tn

    def w_col_block(j):
        return jnp.where(j < q_blocks, j, jnp.where(j < q_blocks + rest_blocks, j + 1, q_blocks))

    li = p["layer"]
    z = _prenorm_matmul(x, p["g_pre_mix"], mod, p["w_in"], li, rows_per_cond=rows_per_cond,
                        shift_row=MOD_SHIFT1, scale_row=MOD_SCALE1, relu2=False, name=f"in_proj_{tag}",
                        tn=tn, w_col_block=w_col_block)
    attn = _attention(z, p["g_q"], p["g_k"], n_seq=n_seq, t=t, d=d, kv_w=kv_w, k_col=col_k, v_col=col_v,
                      rope=rope, past_k=past_k, past_v=past_v, layer=layer, name=f"attention_{tag}")
    if rope is None:
        attn_o, k_new, v_new = attn
    else:
        attn_o, k_new, v_new = attn, None, None
    lru_o, h_fin = _rglru(z, p["conv_w"], p["conv_b"], p["lru_wa"], p["lru_wx"], p["lru_ba"], p["lru_bx"],
                          p["lru_lam"], h0, n_seq=n_seq, t=t, width=d, x_col=col_lx, g_col=col_lg,
                          name=f"rglru_{tag}")
    cm_o = _chunk_mix(z, p["cm_g"], p["cm_ws"], p["cm_bs"], width=d, u_col=col_cu, v_col=col_cv,
                      name=f"chunk_mix_{tag}")
    merged = _merge((attn_o, lru_o, cm_o), (p["w_attn_out"], p["w_lru_out"], p["w_cm_out"]), li, z, col_g,
                    name=f"merge_{tag}")
    x = _matmul_norm_residual(merged, p["w_out"], li, p["g_post_mix"], mod, x, rows_per_cond=rows_per_cond,
                              gate_row=MOD_GATE1, name=f"out_proj_{tag}")
    f = _prenorm_matmul(x, p["g_pre_ff"], mod, p["w_ff1"], li, rows_per_cond=rows_per_cond,
                        shift_row=MOD_SHIFT2, scale_row=MOD_SCALE2, relu2=True, name=f"ff1_{tag}")
    x = _matmul_norm_residual(f, p["w_ff2"], li, p["g_post_ff"], mod, x, rows_per_cond=rows_per_cond,
                              gate_row=MOD_GATE2, name=f"ff2_{tag}")
    return x, k_new, v_new, h_fin


def kernel(x_prompt, x_sample, cache_k, cache_v, state_lru, c, c_ctx, w_mod, b_mod, g_pre_mix, g_post_mix, g_pre_ff, g_post_ff, w_in, g_q, g_k, w_attn_out, conv_w, conv_b, lru_wa, lru_ba, lru_wx, lru_bx, lru_lam, w_lru_out, cm_g, cm_ws, cm_bs, w_cm_out, w_out, w_ff1, w_ff2):
    batch, seq, d = x_prompt.shape
    dec_batch, dec_seq, _ = x_sample.shape
    depth = w_in.shape[0]
    in_w = w_in.shape[2]
    kv_w = (in_w - 8 * d) // 2
    n_kv = kv_w // HEAD_DIM
    past = cache_k.shape[2]
    n_groups = cm_ws.shape[1]

    n_cond = 1 + dec_batch
    cond = jnp.concatenate([c_ctx[None, :], c], axis=0)
    cond = jnp.pad(cond, ((0, -n_cond % SUBLANES), (0, 0)))
    mod = _modulation(cond, w_mod, b_mod)

    w_in_b = w_in.astype(BF16)
    w_attn_out_b, w_lru_out_b, w_cm_out_b = w_attn_out.astype(BF16), w_lru_out.astype(BF16), w_cm_out.astype(BF16)
    w_out_b, w_ff1_b, w_ff2_b = w_out.astype(BF16), w_ff1.astype(BF16), w_ff2.astype(BF16)
    lru_wa_b, lru_wx_b, cm_ws_b = (0.5 * lru_wa).astype(BF16), (0.5 * lru_wx).astype(BF16), cm_ws.astype(BF16)
    cm_bs_b = jnp.broadcast_to(cm_bs[:, :, :, None], cm_bs.shape + (d // n_groups,))

    rope = _rope_tables(dec_seq)
    past_k = cache_k.reshape(dec_batch, depth, past, kv_w)
    past_v = cache_v.reshape(dec_batch, depth, past, kv_w)
    h0_prompt = jnp.zeros((batch, 2, d), F32)

    y_p = x_prompt.reshape(batch * seq, d)
    y_s = x_sample.reshape(dec_batch * dec_seq, d)
    new_k, new_v, new_s = [], [], []
    for l in range(depth):
        p = {
            "kv_w": kv_w, "layer": l, "g_pre_mix": g_pre_mix[l], "g_post_mix": g_post_mix[l],
            "g_pre_ff": g_pre_ff[l], "g_post_ff": g_post_ff[l], "w_in": w_in_b, "g_q": g_q[l], "g_k": g_k[l],
            "w_attn_out": w_attn_out_b, "conv_w": conv_w[l], "conv_b": conv_b[l],
            "lru_wa": lru_wa_b[l], "lru_ba": lru_ba[l], "lru_wx": lru_wx_b[l], "lru_bx": lru_bx[l],
            "lru_lam": lru_lam[l], "w_lru_out": w_lru_out_b, "cm_g": cm_g[l], "cm_ws": cm_ws_b[l],
            "cm_bs": cm_bs_b[l], "w_cm_out": w_cm_out_b, "w_out": w_out_b, "w_ff1": w_ff1_b,
            "w_ff2": w_ff2_b,
        }
        y_p, k_l, v_l, s_l = _layer(y_p, mod[l, 0:1], p, f"ctx{l}", n_seq=batch, t=seq,
                                    rows_per_cond=batch * seq, h0=h0_prompt)
        new_k.append(k_l.reshape(batch, seq, n_kv, HEAD_DIM))
        new_v.append(v_l.reshape(batch, seq, n_kv, HEAD_DIM))
        new_s.append(s_l)
        y_s, _, _, _ = _layer(y_s, mod[l, 1:1 + dec_batch], p, f"lat{l}", n_seq=dec_batch, t=dec_seq,
                              rows_per_cond=dec_seq, rope=rope, past_k=past_k, past_v=past_v, layer=l,
                              h0=state_lru[:, l])
    return (y_p.reshape(batch, seq, d), y_s.reshape(dec_batch, dec_seq, d),
            jnp.stack(new_k, axis=1), jnp.stack(new_v, axis=1), jnp.stack(new_s, axis=1))
```

```python
import functools

import jax
import jax.numpy as jnp
from jax import lax
from jax.experimental import pallas as pl
from jax.experimental.pallas import tpu as pltpu

F32 = jnp.float32
BF16 = jnp.bfloat16

EPS = 1e-6
HEAD_DIM = 128
GRID_W = 64
ROPE_BASE = 10000.0
ROPE_FREQS = HEAD_DIM // 4
CONV_W = 4
CONV_PAD_L = 2
LRU_C = 8.0
CHUNK = 128
N_MOD = 6
N_BRANCH = 3
LOG2_E = 1.4426950408889634

LANES = 128
SUBLANES = 8
VMEM_LIMIT_BYTES = 56 * 1024 * 1024
BF16_ROWS = 2 * SUBLANES
ROW_CHUNK = BF16_ROWS
SCORE_BUFFERS = 3
PROB_BUFFERS = 2
NORM_UNROLL = 8
VREG_BUDGET_ELEMS = 32 * SUBLANES * LANES

MOD_SHIFT1, MOD_SCALE1, MOD_GATE1, MOD_SHIFT2, MOD_SCALE2, MOD_GATE2 = range(N_MOD)


def _params(*semantics):
    return pltpu.CompilerParams(dimension_semantics=semantics, vmem_limit_bytes=VMEM_LIMIT_BYTES)


def _rms(x, g):
    return x * lax.rsqrt(jnp.mean(x * x, axis=-1, keepdims=True) + EPS) * g


def _sigmoid(x):
    return 0.5 * jnp.tanh(0.5 * x) + 0.5


def _row_tile(m, cap):
    if m <= cap:
        return m
    t = cap - cap % LANES
    while m % t:
        t -= LANES
    assert t > 0
    return t


def _mod_kernel(c_ref, w_ref, b_ref, o_ref):
    c = c_ref[...]
    s = (c * jax.nn.sigmoid(c)).astype(BF16)
    o_ref[...] = jnp.dot(s, w_ref[...].astype(BF16), preferred_element_type=F32) + b_ref[...]


def _modulation(cond, w_mod, b_mod):
    nc, d = cond.shape
    depth, _, n = w_mod.shape
    tn = _row_tile(n, 1024)
    out = pl.pallas_call(
        _mod_kernel,
        grid=(depth, n // tn),
        in_specs=[
            pl.BlockSpec((nc, d), lambda l, j: (0, 0)),
            pl.BlockSpec((None, d, tn), lambda l, j: (l, 0, j)),
            pl.BlockSpec((None, 1, tn), lambda l, j: (l, 0, j)),
        ],
        out_specs=pl.BlockSpec((None, nc, tn), lambda l, j: (l, 0, j)),
        out_shape=jax.ShapeDtypeStruct((depth, nc, n), F32),
        compiler_params=_params("parallel", "parallel"),
        name="modulation",
    )(cond, w_mod, b_mod.reshape(depth, 1, n))
    return out.reshape(depth, nc, N_MOD, d)


def _prenorm_mm_kernel(x_ref, g_ref, mod_ref, w_ref, o_ref, h_ref, *, shift_row, scale_row, relu2,
                       rows_per_cond, n_tiles, n_slices):
    i, j = pl.program_id(0), pl.program_id(1)
    tm = x_ref.shape[0]

    def gain_shift(tile):
        cond = (tile * tm) // rows_per_cond
        return (g_ref[...] * (1.0 + mod_ref[cond, scale_row:scale_row + 1, :]),
                mod_ref[cond, shift_row:shift_row + 1, :])

    def normalise(rows, slot, gain, shift):
        x = x_ref[rows, :]
        inv = lax.rsqrt(jnp.mean(x * x, axis=-1, keepdims=True) + EPS)
        h_ref[slot, rows, :] = (x * inv * gain + shift).astype(BF16)

    @pl.when(jnp.logical_and(i == 0, j == 0))
    def _():
        gain, shift = gain_shift(0)

        def chunk(c, carry):
            normalise(pl.ds(pl.multiple_of(c * ROW_CHUNK, ROW_CHUNK), ROW_CHUNK), 0, gain, shift)
            return carry

        lax.fori_loop(0, tm // ROW_CHUNK, chunk, 0, unroll=NORM_UNROLL)

    slice_rows = tm // n_slices
    gain, shift = gain_shift(jnp.minimum(i + 1, n_tiles - 1))

    def normalise_slice(s):
        row0 = pl.multiple_of(s * slice_rows, slice_rows)
        for c in range(slice_rows // ROW_CHUNK):
            normalise(pl.ds(row0 + c * ROW_CHUNK, ROW_CHUNK), (i + 1) % 2, gain, shift)

    @pl.when(jnp.logical_and(i == 0, j == 1))
    def _():
        normalise_slice(0)

    normalise_slice(jnp.minimum(j, n_slices - 1))

    acc = jnp.dot(h_ref[i % 2], w_ref[...], preferred_element_type=F32)
    if relu2:
        acc = jnp.square(jnp.maximum(acc, 0.0))
    o_ref[...] = acc.astype(o_ref.dtype)


def _prenorm_matmul(x, g, mod, w, layer, *, rows_per_cond, shift_row, scale_row, relu2, name, tn=None,
                    w_col_block=lambda j: j):
    m, d = x.shape
    n = w.shape[2]
    tm = _row_tile(rows_per_cond, 1024)
    tn = tn or _row_tile(n, 1024)
    n_tiles, nj = m // tm, n // tn
    assert rows_per_cond % tm == 0 and n % tn == 0 and nj >= 2
    n_slices = max(s for s in range(1, nj + 1) if (tm // ROW_CHUNK) % s == 0)

    def x_tile(i, j):
        return jnp.where(jnp.logical_and(i == 0, j == 0), 0, jnp.minimum(i + 1, n_tiles - 1))

    return pl.pallas_call(
        functools.partial(_prenorm_mm_kernel, shift_row=shift_row, scale_row=scale_row, relu2=relu2,
                          rows_per_cond=rows_per_cond, n_tiles=n_tiles, n_slices=n_slices),
        grid=(n_tiles, nj),
        in_specs=[
            pl.BlockSpec((tm, d), lambda i, j: (x_tile(i, j), 0)),
            pl.BlockSpec((1, d), lambda i, j: (0, 0)),
            pl.BlockSpec(mod.shape, lambda i, j: (0, 0, 0)),
            pl.BlockSpec((None, d, tn), lambda i, j: (layer, 0, w_col_block(j))),
        ],
        out_specs=pl.BlockSpec((tm, tn), lambda i, j: (i, j)),
        out_shape=jax.ShapeDtypeStruct((m, n), BF16),
        scratch_shapes=[pltpu.VMEM((2, tm, d), BF16)],
        compiler_params=_params("arbitrary", "arbitrary"),
        name=name,
    )(x, g.reshape(1, d), mod, w)


def _mm_norm_res_kernel(a_ref, w_ref, g_ref, mod_ref, x_ref, o_ref, acc_ref, *, gate_row, nk):
    def product():
        return jnp.dot(a_ref[...], w_ref[...], preferred_element_type=F32)

    def finish():
        gain = mod_ref[gate_row:gate_row + 1, :] * g_ref[...]

        def chunk(c, carry):
            rows = pl.ds(pl.multiple_of(c * ROW_CHUNK, ROW_CHUNK), ROW_CHUNK)
            y = acc_ref[rows, :]
            inv = lax.rsqrt(jnp.mean(y * y, axis=-1, keepdims=True) + EPS)
            o_ref[rows, :] = x_ref[rows, :] + y * inv * gain
            return carry

        lax.fori_loop(0, o_ref.shape[0] // ROW_CHUNK, chunk, 0, unroll=NORM_UNROLL)

    if nk == 1:
        acc_ref[...] = product()
        finish()
        return
    k = pl.program_id(1)

    @pl.when(k == 0)
    def _():
        acc_ref[...] = product()

    @pl.when(k > 0)
    def _():
        acc_ref[...] += product()

    @pl.when(k == nk - 1)
    def _():
        finish()


def _matmul_norm_residual(a, w, layer, g, mod, x, *, rows_per_cond, gate_row, name):
    m, kdim = a.shape
    d = w.shape[2]
    tk = _row_tile(kdim, 2048)
    nk = kdim // tk
    tm = _row_tile(rows_per_cond, 512)
    assert rows_per_cond % tm == 0
    return pl.pallas_call(
        functools.partial(_mm_norm_res_kernel, gate_row=gate_row, nk=nk),
        grid=(m // tm, nk),
        in_specs=[
            pl.BlockSpec((tm, tk), lambda i, k: (i, k)),
            pl.BlockSpec((None, tk, d), lambda i, k: (layer, k, 0)),
            pl.BlockSpec((1, d), lambda i, k: (0, 0)),
            pl.BlockSpec((None, N_MOD, d), lambda i, k: ((i * tm) // rows_per_cond, 0, 0)),
            pl.BlockSpec((tm, d), lambda i, k: (i, 0)),
        ],
        out_specs=pl.BlockSpec((tm, d), lambda i, k: (i, 0)),
        out_shape=jax.ShapeDtypeStruct((m, d), F32),
        scratch_shapes=[pltpu.VMEM((tm, d), F32)],
        compiler_params=_params("parallel", "arbitrary"),
        name=name,
    )(a, w, g.reshape(1, d), mod, x)


def _merge_kernel(a0_ref, a1_ref, a2_ref, w0_ref, w1_ref, w2_ref, g0_ref, g1_ref, g2_ref, o_ref):
    acc = _sigmoid(g0_ref[...].astype(F32)) * jnp.dot(a0_ref[...], w0_ref[...], preferred_element_type=F32)
    acc += _sigmoid(g1_ref[...].astype(F32)) * jnp.dot(a1_ref[...], w1_ref[...], preferred_element_type=F32)
    acc += _sigmoid(g2_ref[...].astype(F32)) * jnp.dot(a2_ref[...], w2_ref[...], preferred_element_type=F32)
    o_ref[...] = acc.astype(o_ref.dtype)


def _merge(branches, weights, layer, z, gate_col0, name):
    m, d = branches[0].shape
    tm = _row_tile(m, 1024)
    tn = _row_tile(d, 512)
    assert gate_col0 % tn == 0 and d % tn == 0
    a_spec = pl.BlockSpec((tm, d), lambda i, j: (i, 0))
    w_spec = pl.BlockSpec((None, d, tn), lambda i, j: (layer, 0, j))

    def gate_spec(b):
        off = (gate_col0 + b * d) // tn
        return pl.BlockSpec((tm, tn), lambda i, j: (i, off + j))

    return pl.pallas_call(
        _merge_kernel,
        grid=(m // tm, d // tn),
        in_specs=[a_spec] * N_BRANCH + [w_spec] * N_BRANCH + [gate_spec(b) for b in range(N_BRANCH)],
        out_specs=pl.BlockSpec((tm, tn), lambda i, j: (i, j)),
        out_shape=jax.ShapeDtypeStruct((m, d), BF16),
        compiler_params=_params("parallel", "arbitrary"),
        name=name,
    )(*branches, *weights, z, z, z)


def _rope(x, c, s):
    lane = lax.broadcasted_iota(jnp.int32, x.shape, 1)
    partner = jnp.where((lane % (2 * ROPE_FREQS)) < ROPE_FREQS,
                        pltpu.roll(x, HEAD_DIM - ROPE_FREQS, 1),
                        pltpu.roll(x, ROPE_FREQS, 1))
    return x * c + partner * s


def _attn_kernel(*refs, has_ctx, t, tq, q_group, unit_rows, softmax_rows):
    if has_ctx:
        (zq_ref, zk_ref, zv_ref, gq_ref, gk_ref, cq_ref, sq_ref, ck_ref, sk_ref, pk_ref, pv_ref,
         o_ref, kt_ref, v_ref, *buffers) = refs
    else:
        zq_ref, zk_ref, zv_ref, gq_ref, gk_ref, o_ref, ko_ref, vo_ref, kt_ref, v_ref, *buffers = refs
    s_refs, p_refs = buffers[:SCORE_BUFFERS], buffers[SCORE_BUFFERS:]

    @pl.when(pl.program_id(2) == 0)
    def _():
        k = _rms(zk_ref[...].astype(F32), gk_ref[...])
        v_ref[:, HEAD_DIM:] = jnp.ones((v_ref.shape[0], HEAD_DIM), BF16)
        if has_ctx:
            k = _rope(k, ck_ref[...], sk_ref[...])
            kt_ref[:, 0:t] = k.T.astype(BF16)
            kt_ref[:, t:] = pk_ref[...].T.astype(BF16)
            v_ref[0:t, 0:HEAD_DIM] = zv_ref[...]
            v_ref[t:, 0:HEAD_DIM] = pv_ref[...].astype(BF16)
        else:
            ko_ref[...] = k
            vo_ref[...] = zv_ref[...].astype(F32)
            kt_ref[...] = k.T.astype(BF16)
            v_ref[:, 0:HEAD_DIM] = zv_ref[...]

    q_scale = LOG2_E * HEAD_DIM ** -0.5

    units = [(g, r0) for r0 in range(0, tq, unit_rows) for g in range(q_group)]

    def scores(u):
        g, r0 = units[u]
        rows = slice(r0, r0 + unit_rows)
        q = _rms(zq_ref[rows, g * HEAD_DIM:(g + 1) * HEAD_DIM].astype(F32), gq_ref[...])
        if has_ctx:
            q = _rope(q, cq_ref[rows, :], sq_ref[rows, :])
        s_refs[u % SCORE_BUFFERS][...] = jnp.dot((q * q_scale).astype(BF16), kt_ref[...],
                                                 preferred_element_type=F32)

    def exponentials(u):
        s_ref, p_ref = s_refs[u % SCORE_BUFFERS], p_refs[u % PROB_BUFFERS]
        groups = [slice(r0, r0 + softmax_rows) for r0 in range(0, unit_rows, softmax_rows)]
        maxima = [jnp.max(s_ref[rows, :], axis=-1, keepdims=True) for rows in groups]
        for rows, m in zip(groups, maxima):
            p_ref[rows, :] = jnp.exp2(s_ref[rows, :] - m).astype(BF16)

    def weighted_values(u):
        g, r0 = units[u]
        o = jnp.dot(p_refs[u % PROB_BUFFERS][...], v_ref[...], preferred_element_type=F32)
        o_ref[r0:r0 + unit_rows, g * HEAD_DIM:(g + 1) * HEAD_DIM] = (
            o[:, :HEAD_DIM] * (1.0 / o[:, HEAD_DIM:])).astype(o_ref.dtype)

    lookahead = SCORE_BUFFERS - 1
    for u in range(min(lookahead, len(units))):
        scores(u)
    for u in range(len(units)):
        if u + lookahead < len(units):
            scores(u + lookahead)
        exponentials(u)
        weighted_values(u)


def _attention(z, g_q, g_k, *, n_seq, t, d, kv_w, k_col, v_col, rope=None, past_k=None, past_v=None, layer=None, name):
    has_ctx = rope is not None
    n_kv = kv_w // HEAD_DIM
    q_group = d // kv_w
    qw = q_group * HEAD_DIM
    past = past_k.shape[2] if has_ctx else 0
    tk = t + past
    tq = _row_tile(t, 512)
    nq = t // tq
    unit_rows = _row_tile(tq, 128)
    softmax_rows = min(unit_rows, max(BF16_ROWS, (VREG_BUDGET_ELEMS // tk) // BF16_ROWS * BF16_ROWS))
    assert unit_rows % softmax_rows == 0

    in_specs = [
        pl.BlockSpec((tq, qw), lambda b, h, i: (b * nq + i, h)),
        pl.BlockSpec((t, HEAD_DIM), lambda b, h, i: (b, k_col // HEAD_DIM + h)),
        pl.BlockSpec((t, HEAD_DIM), lambda b, h, i: (b, v_col // HEAD_DIM + h)),
        pl.BlockSpec((1, HEAD_DIM), lambda b, h, i: (0, 0)),
        pl.BlockSpec((1, HEAD_DIM), lambda b, h, i: (0, 0)),
    ]
    args = [z, z, z, g_q.reshape(1, HEAD_DIM), g_k.reshape(1, HEAD_DIM)]
    o_spec = pl.BlockSpec((tq, qw), lambda b, h, i: (b * nq + i, h))
    o_shape = jax.ShapeDtypeStruct((n_seq * t, d), BF16)
    if has_ctx:
        cos, sin = rope
        in_specs += [
            pl.BlockSpec((tq, HEAD_DIM), lambda b, h, i: (i, 0)),
            pl.BlockSpec((tq, HEAD_DIM), lambda b, h, i: (i, 0)),
            pl.BlockSpec((t, HEAD_DIM), lambda b, h, i: (0, 0)),
            pl.BlockSpec((t, HEAD_DIM), lambda b, h, i: (0, 0)),
            pl.BlockSpec((None, None, past, HEAD_DIM), lambda b, h, i: (b, layer, 0, h)),
            pl.BlockSpec((None, None, past, HEAD_DIM), lambda b, h, i: (b, layer, 0, h)),
        ]
        args += [cos, sin, cos, sin, past_k, past_v]
        out_specs = o_spec
        out_shape = o_shape
    else:
        kv_spec = pl.BlockSpec((None, t, HEAD_DIM), lambda b, h, i: (b, 0, h))
        kv_shape = jax.ShapeDtypeStruct((n_seq, t, kv_w), F32)
        out_specs = [o_spec, kv_spec, kv_spec]
        out_shape = [o_shape, kv_shape, kv_shape]

    return pl.pallas_call(
        functools.partial(_attn_kernel, has_ctx=has_ctx, t=t, tq=tq, q_group=q_group, unit_rows=unit_rows,
                          softmax_rows=softmax_rows),
        grid=(n_seq, n_kv, nq),
        in_specs=in_specs,
        out_specs=out_specs,
        out_shape=out_shape,
        scratch_shapes=[
            pltpu.VMEM((HEAD_DIM, tk), BF16),
            pltpu.VMEM((tk, 2 * HEAD_DIM), BF16),
            *[pltpu.VMEM((unit_rows, tk), F32)] * SCORE_BUFFERS,
            *[pltpu.VMEM((unit_rows, tk), BF16)] * PROB_BUFFERS,
        ],
        compiler_params=_params("parallel", "parallel", "arbitrary"),
        name=name,
    )(*args)


def _lru_kernel(zx_ref, zg_ref, cw_ref, cb_ref, wa_ref, wx_ref, ba_ref, bx_ref, lam_ref, h0_ref,
                o_ref, hfin_ref, xpad_ref, xc_ref, hs_ref, *, t, tc):
    cw = zx_ref.shape[1]
    n_lane_blocks = cw // LANES
    n_chunks = t // tc
    pad = SUBLANES
    steps = tc // SUBLANES
    assert steps % SUBLANES == 0

    def natural_rows(r, m):
        seg, j0 = divmod(m * SUBLANES, steps)
        return pl.ds(r + SUBLANES * j0 + seg, SUBLANES, stride=SUBLANES)

    xpad_ref[0:pad, :] = jnp.zeros((pad, cw), F32)
    xpad_ref[t + pad:t + 2 * pad, :] = jnp.zeros((pad, cw), F32)

    def copy_chunk(c, carry):
        r = pl.multiple_of(c * tc, tc)
        xpad_ref[pl.ds(r + pad, tc), :] = zx_ref[pl.ds(r, tc), :].astype(F32)
        return carry

    lax.fori_loop(0, n_chunks, copy_chunk, 0)

    def conv_chunk(c, carry):
        r = pl.multiple_of(c * tc, tc)
        xe = xpad_ref[pl.ds(r, tc + 2 * pad), :]
        n = tc + 2 * pad
        y = cb_ref[...] + pltpu.roll(xe, 2, 0)[pad:pad + tc] * cw_ref[0:1, :]
        y = y + pltpu.roll(xe, 1, 0)[pad:pad + tc] * cw_ref[1:2, :]
        y = y + xe[pad:pad + tc] * cw_ref[2:3, :]
        y = y + pltpu.roll(xe, n - 1, 0)[pad:pad + tc] * cw_ref[3:4, :]
        for nb in range(n_lane_blocks):
            for m in range(steps):
                xc_ref[nb, natural_rows(r, m), :] = y[m * SUBLANES:(m + 1) * SUBLANES, nb * LANES:(nb + 1) * LANES]
        return carry

    lax.fori_loop(0, n_chunks, conv_chunk, 0)

    sublane = lax.broadcasted_iota(jnp.int32, (SUBLANES, LANES), 0)

    half_k = [[(-0.5 * LRU_C * LOG2_E) * jax.nn.softplus(-lam_ref[direction:direction + 1, nb * LANES:(nb + 1) * LANES])
               for nb in range(n_lane_blocks)] for direction in range(2)]

    half_ba = [[0.5 * ba_ref[direction:direction + 1, nb * LANES:(nb + 1) * LANES]
                for nb in range(n_lane_blocks)] for direction in range(2)]
    half_bx = [[0.5 * bx_ref[direction:direction + 1, nb * LANES:(nb + 1) * LANES]
                for nb in range(n_lane_blocks)] for direction in range(2)]

    def gates(xn, direction, nb):
        xb = xn.astype(BF16)
        tanh_r = jnp.tanh(jnp.dot(xb, wa_ref[direction, nb], preferred_element_type=F32) + half_ba[direction][nb])
        tanh_i = jnp.tanh(jnp.dot(xb, wx_ref[direction, nb], preferred_element_type=F32) + half_bx[direction][nb])
        a = jnp.exp2(half_k[direction][nb] * tanh_r + half_k[direction][nb])
        scale = jnp.exp2((0.5 * LOG2_E) * jnp.log(jnp.maximum(1.0 - a * a, 0.0)))
        b = scale * ((0.5 * tanh_i + 0.5) * xn)
        return a, b

    def chunk_scan(a, b, carry, reverse):
        order = range(steps - 1, -1, -1) if reverse else range(steps)
        local, decay = [None] * steps, [None] * steps
        h, p = None, None
        for j in order:
            aj, bj = a[j * SUBLANES:(j + 1) * SUBLANES], b[j * SUBLANES:(j + 1) * SUBLANES]
            h = bj if h is None else aj * h + bj
            p = aj if p is None else aj * p
            local[j], decay[j] = h, p
        seg_h, seg_p = h, p
        for dist in (1, 2, 4):
            if reverse:
                edge, shift = sublane >= SUBLANES - dist, SUBLANES - dist
            else:
                edge, shift = sublane < dist, dist
            h_prev = jnp.where(edge, 0.0, pltpu.roll(seg_h, shift, 0))
            p_prev = jnp.where(edge, 1.0, pltpu.roll(seg_p, shift, 0))
            seg_h = seg_p * h_prev + seg_h
            seg_p = seg_p * p_prev
        seg_state = seg_h + seg_p * carry
        if reverse:
            entering = jnp.where(sublane == SUBLANES - 1, carry, pltpu.roll(seg_state, SUBLANES - 1, 0))
            leaving = jnp.broadcast_to(seg_state[0:1], (SUBLANES, LANES))
        else:
            entering = jnp.where(sublane == 0, carry, pltpu.roll(seg_state, 1, 0))
            leaving = jnp.broadcast_to(seg_state[SUBLANES - 1:SUBLANES], (SUBLANES, LANES))
        return [local[j] + decay[j] * entering for j in range(steps)], leaving

    def forward_chunk(c, carry):
        r = pl.multiple_of(c * tc, tc)
        new_carry = []
        for nb in range(n_lane_blocks):
            lanes = slice(nb * LANES, (nb + 1) * LANES)
            a, b = gates(xc_ref[nb, pl.ds(r, tc), :], 0, nb)
            states, leaving = chunk_scan(a, b, carry[nb], reverse=False)
            for j in range(steps):
                hs_ref[nb, pl.ds(r + j * SUBLANES, SUBLANES), :] = states[j]
            new_carry.append(leaving)
        return tuple(new_carry)

    h0f = tuple(jnp.broadcast_to(h0_ref[0:1, nb * LANES:(nb + 1) * LANES], (SUBLANES, LANES))
                for nb in range(n_lane_blocks))
    hf_last = lax.fori_loop(0, n_chunks, forward_chunk, h0f, unroll=2)

    def backward_chunk(c, carry):
        r = pl.multiple_of((n_chunks - 1 - c) * tc, tc)
        new_carry = []
        for nb in range(n_lane_blocks):
            lanes = slice(nb * LANES, (nb + 1) * LANES)
            a, b = gates(xc_ref[nb, pl.ds(r, tc), :], 1, nb)
            states, leaving = chunk_scan(a, b, carry[nb], reverse=True)
            for j in range(steps):
                hs_ref[nb, pl.ds(r + j * SUBLANES, SUBLANES), :] += states[j]
            new_carry.append(leaving)
        return tuple(new_carry)

    h0b = tuple(jnp.broadcast_to(h0_ref[1:2, nb * LANES:(nb + 1) * LANES], (SUBLANES, LANES))
                for nb in range(n_lane_blocks))
    hb_first = lax.fori_loop(0, n_chunks, backward_chunk, h0b, unroll=2)

    for nb in range(n_lane_blocks):
        lanes = slice(nb * LANES, (nb + 1) * LANES)
        hfin_ref[0:1, lanes] = hf_last[nb][0:1]
        hfin_ref[1:2, lanes] = hb_first[nb][0:1]

    def gate_chunk(c, carry):
        r = pl.multiple_of(c * tc, tc)
        for nb in range(n_lane_blocks):
            lanes = slice(nb * LANES, (nb + 1) * LANES)
            for m in range(0, steps, 2):
                rows = pl.ds(r + m * SUBLANES, BF16_ROWS)
                hsum = jnp.concatenate([hs_ref[nb, natural_rows(r, m), :], hs_ref[nb, natural_rows(r, m + 1), :]],
                                       axis=0)
                o_ref[rows, lanes] = (hsum * jax.nn.gelu(zg_ref[rows, lanes].astype(F32))).astype(o_ref.dtype)
        return carry

    lax.fori_loop(0, n_chunks, gate_chunk, 0)


def _rglru(z, conv_w, conv_b, wa, wx, ba, bx, lam, h0, *, n_seq, t, width, x_col, g_col, name):
    cw = _row_tile(width, 1024)
    tc = _row_tile(t, 128)
    lb = cw // LANES
    assert wa.shape[-1] == LANES and x_col % cw == 0 and g_col % cw == 0
    vec_spec = pl.BlockSpec((2, cw), lambda s, c: (0, c))
    w_spec = pl.BlockSpec((2, lb, LANES, LANES), lambda s, c: (0, c, 0, 0))
    return pl.pallas_call(
        functools.partial(_lru_kernel, t=t, tc=tc),
        grid=(n_seq, width // cw),
        in_specs=[
            pl.BlockSpec((t, cw), lambda s, c: (s, x_col // cw + c)),
            pl.BlockSpec((t, cw), lambda s, c: (s, g_col // cw + c)),
            pl.BlockSpec((CONV_W, cw), lambda s, c: (0, c)),
            pl.BlockSpec((1, cw), lambda s, c: (0, c)),
            w_spec, w_spec, vec_spec, vec_spec, vec_spec,
            pl.BlockSpec((None, 2, cw), lambda s, c: (s, 0, c)),
        ],
        out_specs=[
            pl.BlockSpec((t, cw), lambda s, c: (s, c)),
            pl.BlockSpec((None, 2, cw), lambda s, c: (s, 0, c)),
        ],
        out_shape=[
            jax.ShapeDtypeStruct((n_seq * t, width), BF16),
            jax.ShapeDtypeStruct((n_seq, 2, width), F32),
        ],
        scratch_shapes=[
            pltpu.VMEM((t + 2 * SUBLANES, cw), F32),
            pltpu.VMEM((lb, t, LANES), F32),
            pltpu.VMEM((lb, t, LANES), F32),
        ],
        compiler_params=_params("parallel", "parallel"),
        name=name,
    )(z, z, conv_w, conv_b.reshape(1, width), wa, wx, ba, bx, lam, h0)


def _cmix_kernel(zu_ref, zv_ref, g_ref, ws_ref, bs_ref, o_ref, v_ref, *, n_groups):
    rows, width = zu_ref.shape
    gw = width // n_groups
    def norm_chunk(c, carry):
        r = pl.ds(pl.multiple_of(c * ROW_CHUNK, ROW_CHUNK), ROW_CHUNK)
        v_ref[r, :] = _rms(jax.nn.gelu(zv_ref[r, :].astype(F32)), g_ref[...]).astype(BF16)
        return carry

    lax.fori_loop(0, rows // ROW_CHUNK, norm_chunk, 0, unroll=NORM_UNROLL)
    for ch in range(rows // CHUNK):
        r = slice(ch * CHUNK, (ch + 1) * CHUNK)
        for g in range(n_groups):
            cols = slice(g * gw, (g + 1) * gw)
            mixed = jnp.dot(ws_ref[g], v_ref[r, cols], preferred_element_type=F32) + bs_ref[g]
            o_ref[r, cols] = (jax.nn.gelu(zu_ref[r, cols].astype(F32)) * mixed).astype(o_ref.dtype)


def _chunk_mix(z, cm_g, ws, bs, *, width, u_col, v_col, name):
    m = z.shape[0]
    n_groups = ws.shape[0]
    tr = _row_tile(m, 2 * CHUNK)
    assert u_col % width == 0 and v_col % width == 0
    return pl.pallas_call(
        functools.partial(_cmix_kernel, n_groups=n_groups),
        grid=(m // tr,),
        in_specs=[
            pl.BlockSpec((tr, width), lambda i: (i, u_col // width)),
            pl.BlockSpec((tr, width), lambda i: (i, v_col // width)),
            pl.BlockSpec((1, width), lambda i: (0, 0)),
            pl.BlockSpec(ws.shape, lambda i: (0, 0, 0)),
            pl.BlockSpec(bs.shape, lambda i: (0, 0, 0)),
        ],
        out_specs=pl.BlockSpec((tr, width), lambda i: (i, 0)),
        out_shape=jax.ShapeDtypeStruct((m, width), BF16),
        scratch_shapes=[pltpu.VMEM((tr, width), BF16)],
        compiler_params=_params("parallel"),
        name=name,
    )(z, z, cm_g.reshape(1, width), ws, bs)


def _rope_tables(n_tokens):
    rows = n_tokens // GRID_W
    pos_row = jnp.repeat(jnp.arange(rows), GRID_W).astype(F32)
    pos_col = (jnp.arange(n_tokens) % GRID_W).astype(F32)
    inv = ROPE_BASE ** (-jnp.arange(ROPE_FREQS, dtype=F32) / ROPE_FREQS)
    cr, sr = jnp.cos(pos_row[:, None] * inv), jnp.sin(pos_row[:, None] * inv)
    cc, sc = jnp.cos(pos_col[:, None] * inv), jnp.sin(pos_col[:, None] * inv)
    return (jnp.concatenate([cr, cr, cc, cc], axis=-1), jnp.concatenate([-sr, sr, -sc, sc], axis=-1))


def _layer(x, mod, p, tag, *, n_seq, t, rows_per_cond, rope=None, past_k=None, past_v=None, layer=None, h0=None):
    d = x.shape[1]
    kv_w = p["kv_w"]
    col_lx, col_lg, col_cu, col_cv, col_g, col_k = d, 2 * d, 3 * d, 4 * d, 5 * d, 8 * d
    col_v = col_k + kv_w

    tn = 2 * kv_w
    assert d % tn == 0
    q_blocks, rest_blocks = d // tn, 7 * d // tn

    def w_col_block(j):
        return jnp.where(j < q_blocks, j, jnp.where(j < q_blocks + rest_blocks, j + 1, q_blocks))

    li = p["layer"]
    z = _prenorm_matmul(x, p["g_pre_mix"], mod, p["w_in"], li, rows_per_cond=rows_per_cond,
                        shift_row=MOD_SHIFT1, scale_row=MOD_SCALE1, relu2=False, name=f"in_proj_{tag}",
                        tn=tn, w_col_block=w_col_block)
    attn = _attention(z, p["g_q"], p["g_k"], n_seq=n_seq, t=t, d=d, kv_w=kv_w, k_col=col_k, v_col=col_v,
                      rope=rope, past_k=past_k, past_v=past_v, layer=layer, name=f"attention_{tag}")
    if rope is None:
        attn_o, k_new, v_new = attn
    else:
        attn_o, k_new, v_new = attn, None, None
    lru_o, h_fin = _rglru(z, p["conv_w"], p["conv_b"], p["lru_wa"], p["lru_wx"], p["lru_ba"], p["lru_bx"],
                          p["lru_lam"], h0, n_seq=n_seq, t=t, width=d, x_col=col_lx, g_col=col_lg,
                          name=f"rglru_{tag}")
    cm_o = _chunk_mix(z, p["cm_g"], p["cm_ws"], p["cm_bs"], width=d, u_col=col_cu, v_col=col_cv,
                      name=f"chunk_mix_{tag}")
    merged = _merge((attn_o, lru_o, cm_o), (p["w_attn_out"], p["w_lru_out"], p["w_cm_out"]), li, z, col_g,
                    name=f"merge_{tag}")
    x = _matmul_norm_residual(merged, p["w_out"], li, p["g_post_mix"], mod, x, rows_per_cond=rows_per_cond,
                              gate_row=MOD_GATE1, name=f"out_proj_{tag}")
    f = _prenorm_matmul(x, p["g_pre_ff"], mod, p["w_ff1"], li, rows_per_cond=rows_per_cond,
                        shift_row=MOD_SHIFT2, scale_row=MOD_SCALE2, relu2=True, name=f"ff1_{tag}")
    x = _matmul_norm_residual(f, p["w_ff2"], li, p["g_post_ff"], mod, x, rows_per_cond=rows_per_cond,
                              gate_row=MOD_GATE2, name=f"ff2_{tag}")
    return x, k_new, v_new, h_fin


def kernel(x_prompt, x_sample, cache_k, cache_v, state_lru, c, c_ctx, w_mod, b_mod, g_pre_mix, g_post_mix, g_pre_ff, g_post_ff, w_in, g_q, g_k, w_attn_out, conv_w, conv_b, lru_wa, lru_ba, lru_wx, lru_bx, lru_lam, w_lru_out, cm_g, cm_ws, cm_bs, w_cm_out, w_out, w_ff1, w_ff2):
    batch, seq, d = x_prompt.shape
    dec_batch, dec_seq, _ = x_sample.shape
    depth = w_in.shape[0]
    in_w = w_in.shape[2]
    kv_w = (in_w - 8 * d) // 2
    n_kv = kv_w // HEAD_DIM
    past = cache_k.shape[2]
    n_groups = cm_ws.shape[1]

    n_cond = 1 + dec_batch
    cond = jnp.concatenate([c_ctx[None, :], c], axis=0)
    cond = jnp.pad(cond, ((0, -n_cond % SUBLANES), (0, 0)))
    mod = _modulation(cond, w_mod, b_mod)

    w_in_b = w_in.astype(BF16)
    w_attn_out_b, w_lru_out_b, w_cm_out_b = w_attn_out.astype(BF16), w_lru_out.astype(BF16), w_cm_out.astype(BF16)
    w_out_b, w_ff1_b, w_ff2_b = w_out.astype(BF16), w_ff1.astype(BF16), w_ff2.astype(BF16)
    lru_wa_b, lru_wx_b, cm_ws_b = (0.5 * lru_wa).astype(BF16), (0.5 * lru_wx).astype(BF16), cm_ws.astype(BF16)
    cm_bs_b = jnp.broadcast_to(cm_bs[:, :, :, None], cm_bs.shape + (d // n_groups,))

    rope = _rope_tables(dec_seq)
    past_k = cache_k.reshape(dec_batch, depth, past, kv_w)
    past_v = cache_v.reshape(dec_batch, depth, past, kv_w)
    h0_prompt = jnp.zeros((batch, 2, d), F32)

    y_p = x_prompt.reshape(batch * seq, d)
    y_s = x_sample.reshape(dec_batch * dec_seq, d)
    new_k, new_v, new_s = [], [], []
    for l in range(depth):
        p = {
            "kv_w": kv_w, "layer": l, "g_pre_mix": g_pre_mix[l], "g_post_mix": g_post_mix[l],
            "g_pre_ff": g_pre_ff[l], "g_post_ff": g_post_ff[l], "w_in": w_in_b, "g_q": g_q[l], "g_k": g_k[l],
            "w_attn_out": w_attn_out_b, "conv_w": conv_w[l], "conv_b": conv_b[l],
            "lru_wa": lru_wa_b[l], "lru_ba": lru_ba[l], "lru_wx": lru_wx_b[l], "lru_bx": lru_bx[l],
            "lru_lam": lru_lam[l], "w_lru_out": w_lru_out_b, "cm_g": cm_g[l], "cm_ws": cm_ws_b[l],
            "cm_bs": cm_bs_b[l], "w_cm_out": w_cm_out_b, "w_out": w_out_b, "w_ff1": w_ff1_b,
            "w_ff2": w_ff2_b,
        }
        y_p, k_l, v_l, s_l = _layer(y_p, mod[l, 0:1], p, f"ctx{l}", n_seq=batch, t=seq,
                                    rows_per_cond=batch * seq, h0=h0_prompt)
        new_k.append(k_l.reshape(batch, seq, n_kv, HEAD_DIM))
        new_v.append(v_l.reshape(batch, seq, n_kv, HEAD_DIM))
        new_s.append(s_l)
        y_s, _, _, _ = _layer(y_s, mod[l, 1:1 + dec_batch], p, f"lat{l}", n_seq=dec_batch, t=dec_seq,
                              rows_per_cond=dec_seq, rope=rope, past_k=past_k, past_v=past_v, layer=l,
                              h0=state_lru[:, l])
    return (y_p.reshape(batch, seq, d), y_s.reshape(dec_batch, dec_seq, d),
            jnp.stack(new_k, axis=1), jnp.stack(new_v, axis=1), jnp.stack(new_s, axis=1))
```

```python
import functools

import jax
import jax.numpy as jnp
from jax import lax
from jax.experimental import pallas as pl
from jax.experimental.pallas import tpu as pltpu

F32 = jnp.float32
BF16 = jnp.bfloat16

EPS = 1e-6
HEAD_DIM = 128
GRID_W = 64
ROPE_BASE = 10000.0
ROPE_FREQS = HEAD_DIM // 4
CONV_W = 4
CONV_PAD_L = 2
LRU_C = 8.0
CHUNK = 128
N_MOD = 6
N_BRANCH = 3
LOG2_E = 1.4426950408889634

LANES = 128
SUBLANES = 8
VMEM_LIMIT_BYTES = 56 * 1024 * 1024
BF16_ROWS = 2 * SUBLANES
ROW_CHUNK = BF16_ROWS
SCORE_BUFFERS = 3
PROB_BUFFERS = 2
SCAN_UNROLL = 4
NORM_UNROLL = 8
VREG_BUDGET_ELEMS = 32 * SUBLANES * LANES

MOD_SHIFT1, MOD_SCALE1, MOD_GATE1, MOD_SHIFT2, MOD_SCALE2, MOD_GATE2 = range(N_MOD)


def _params(*semantics):
    return pltpu.CompilerParams(dimension_semantics=semantics, vmem_limit_bytes=VMEM_LIMIT_BYTES)


def _rms(x, g):
    return x * lax.rsqrt(jnp.mean(x * x, axis=-1, keepdims=True) + EPS) * g


def _sigmoid(x):
    return 0.5 * jnp.tanh(0.5 * x) + 0.5


def _row_tile(m, cap):
    if m <= cap:
        return m
    t = cap - cap % LANES
    while m % t:
        t -= LANES
    assert t > 0
    return t


def _mod_kernel(c_ref, w_ref, b_ref, o_ref):
    c = c_ref[...]
    s = (c * jax.nn.sigmoid(c)).astype(BF16)
    o_ref[...] = jnp.dot(s, w_ref[...].astype(BF16), preferred_element_type=F32) + b_ref[...]


def _modulation(cond, w_mod, b_mod):
    nc, d = cond.shape
    depth, _, n = w_mod.shape
    tn = _row_tile(n, 1024)
    out = pl.pallas_call(
        _mod_kernel,
        grid=(depth, n // tn),
        in_specs=[
            pl.BlockSpec((nc, d), lambda l, j: (0, 0)),
            pl.BlockSpec((None, d, tn), lambda l, j: (l, 0, j)),
            pl.BlockSpec((None, 1, tn), lambda l, j: (l, 0, j)),
        ],
        out_specs=pl.BlockSpec((None, nc, tn), lambda l, j: (l, 0, j)),
        out_shape=jax.ShapeDtypeStruct((depth, nc, n), F32),
        compiler_params=_params("parallel", "parallel"),
        name="modulation",
    )(cond, w_mod, b_mod.reshape(depth, 1, n))
    return out.reshape(depth, nc, N_MOD, d)


def _prenorm_mm_kernel(x_ref, g_ref, mod_ref, w_ref, o_ref, h_ref, *, shift_row, scale_row, relu2,
                       rows_per_cond, n_tiles, n_slices):
    i, j = pl.program_id(0), pl.program_id(1)
    tm = x_ref.shape[0]

    def gain_shift(tile):
        cond = (tile * tm) // rows_per_cond
        return (g_ref[...] * (1.0 + mod_ref[cond, scale_row:scale_row + 1, :]),
                mod_ref[cond, shift_row:shift_row + 1, :])

    def normalise(rows, slot, gain, shift):
        x = x_ref[rows, :]
        inv = lax.rsqrt(jnp.mean(x * x, axis=-1, keepdims=True) + EPS)
        h_ref[slot, rows, :] = (x * inv * gain + shift).astype(BF16)

    @pl.when(jnp.logical_and(i == 0, j == 0))
    def _():
        gain, shift = gain_shift(0)

        def chunk(c, carry):
            normalise(pl.ds(pl.multiple_of(c * ROW_CHUNK, ROW_CHUNK), ROW_CHUNK), 0, gain, shift)
            return carry

        lax.fori_loop(0, tm // ROW_CHUNK, chunk, 0, unroll=NORM_UNROLL)

    slice_rows = tm // n_slices
    gain, shift = gain_shift(jnp.minimum(i + 1, n_tiles - 1))

    def normalise_slice(s):
        row0 = pl.multiple_of(s * slice_rows, slice_rows)
        for c in range(slice_rows // ROW_CHUNK):
            normalise(pl.ds(row0 + c * ROW_CHUNK, ROW_CHUNK), (i + 1) % 2, gain, shift)

    @pl.when(jnp.logical_and(i == 0, j == 1))
    def _():
        normalise_slice(0)

    normalise_slice(jnp.minimum(j, n_slices - 1))

    acc = jnp.dot(h_ref[i % 2], w_ref[...], preferred_element_type=F32)
    if relu2:
        acc = jnp.square(jnp.maximum(acc, 0.0))
    o_ref[...] = acc.astype(o_ref.dtype)


def _prenorm_matmul(x, g, mod, w, layer, *, rows_per_cond, shift_row, scale_row, relu2, name, tn=None,
                    w_col_block=lambda j: j):
    m, d = x.shape
    n = w.shape[2]
    tm = _row_tile(rows_per_cond, 1024)
    tn = tn or _row_tile(n, 1024)
    n_tiles, nj = m // tm, n // tn
    assert rows_per_cond % tm == 0 and n % tn == 0 and nj >= 2
    n_slices = max(s for s in range(1, nj + 1) if (tm // ROW_CHUNK) % s == 0)

    def x_tile(i, j):
        return jnp.where(jnp.logical_and(i == 0, j == 0), 0, jnp.minimum(i + 1, n_tiles - 1))

    return pl.pallas_call(
        functools.partial(_prenorm_mm_kernel, shift_row=shift_row, scale_row=scale_row, relu2=relu2,
                          rows_per_cond=rows_per_cond, n_tiles=n_tiles, n_slices=n_slices),
        grid=(n_tiles, nj),
        in_specs=[
            pl.BlockSpec((tm, d), lambda i, j: (x_tile(i, j), 0)),
            pl.BlockSpec((1, d), lambda i, j: (0, 0)),
            pl.BlockSpec(mod.shape, lambda i, j: (0, 0, 0)),
            pl.BlockSpec((None, d, tn), lambda i, j: (layer, 0, w_col_block(j))),
        ],
        out_specs=pl.BlockSpec((tm, tn), lambda i, j: (i, j)),
        out_shape=jax.ShapeDtypeStruct((m, n), BF16),
        scratch_shapes=[pltpu.VMEM((2, tm, d), BF16)],
        compiler_params=_params("arbitrary", "arbitrary"),
        name=name,
    )(x, g.reshape(1, d), mod, w)


def _mm_norm_res_kernel(a_ref, w_ref, g_ref, mod_ref, x_ref, o_ref, acc_ref, *, gate_row, nk):
    def product():
        return jnp.dot(a_ref[...], w_ref[...], preferred_element_type=F32)

    def finish():
        gain = mod_ref[gate_row:gate_row + 1, :] * g_ref[...]

        def chunk(c, carry):
            rows = pl.ds(pl.multiple_of(c * ROW_CHUNK, ROW_CHUNK), ROW_CHUNK)
            y = acc_ref[rows, :]
            inv = lax.rsqrt(jnp.mean(y * y, axis=-1, keepdims=True) + EPS)
            o_ref[rows, :] = x_ref[rows, :] + y * inv * gain
            return carry

        lax.fori_loop(0, o_ref.shape[0] // ROW_CHUNK, chunk, 0, unroll=NORM_UNROLL)

    if nk == 1:
        acc_ref[...] = product()
        finish()
        return
    k = pl.program_id(1)

    @pl.when(k == 0)
    def _():
        acc_ref[...] = product()

    @pl.when(k > 0)
    def _():
        acc_ref[...] += product()

    @pl.when(k == nk - 1)
    def _():
        finish()


def _matmul_norm_residual(a, w, layer, g, mod, x, *, rows_per_cond, gate_row, name):
    m, kdim = a.shape
    d = w.shape[2]
    tk = _row_tile(kdim, 2048)
    nk = kdim // tk
    tm = _row_tile(rows_per_cond, 512)
    assert rows_per_cond % tm == 0
    return pl.pallas_call(
        functools.partial(_mm_norm_res_kernel, gate_row=gate_row, nk=nk),
        grid=(m // tm, nk),
        in_specs=[
            pl.BlockSpec((tm, tk), lambda i, k: (i, k)),
            pl.BlockSpec((None, tk, d), lambda i, k: (layer, k, 0)),
            pl.BlockSpec((1, d), lambda i, k: (0, 0)),
            pl.BlockSpec((None, N_MOD, d), lambda i, k: ((i * tm) // rows_per_cond, 0, 0)),
            pl.BlockSpec((tm, d), lambda i, k: (i, 0)),
        ],
        out_specs=pl.BlockSpec((tm, d), lambda i, k: (i, 0)),
        out_shape=jax.ShapeDtypeStruct((m, d), F32),
        scratch_shapes=[pltpu.VMEM((tm, d), F32)],
        compiler_params=_params("parallel", "arbitrary"),
        name=name,
    )(a, w, g.reshape(1, d), mod, x)


def _merge_kernel(a0_ref, a1_ref, a2_ref, w0_ref, w1_ref, w2_ref, g0_ref, g1_ref, g2_ref, o_ref):
    acc = _sigmoid(g0_ref[...].astype(F32)) * jnp.dot(a0_ref[...], w0_ref[...], preferred_element_type=F32)
    acc += _sigmoid(g1_ref[...].astype(F32)) * jnp.dot(a1_ref[...], w1_ref[...], preferred_element_type=F32)
    acc += _sigmoid(g2_ref[...].astype(F32)) * jnp.dot(a2_ref[...], w2_ref[...], preferred_element_type=F32)
    o_ref[...] = acc.astype(o_ref.dtype)


def _merge(branches, weights, layer, z, gate_col0, name):
    m, d = branches[0].shape
    tm = _row_tile(m, 1024)
    tn = _row_tile(d, 512)
    assert gate_col0 % tn == 0 and d % tn == 0
    a_spec = pl.BlockSpec((tm, d), lambda i, j: (i, 0))
    w_spec = pl.BlockSpec((None, d, tn), lambda i, j: (layer, 0, j))

    def gate_spec(b):
        off = (gate_col0 + b * d) // tn
        return pl.BlockSpec((tm, tn), lambda i, j: (i, off + j))

    return pl.pallas_call(
        _merge_kernel,
        grid=(m // tm, d // tn),
        in_specs=[a_spec] * N_BRANCH + [w_spec] * N_BRANCH + [gate_spec(b) for b in range(N_BRANCH)],
        out_specs=pl.BlockSpec((tm, tn), lambda i, j: (i, j)),
        out_shape=jax.ShapeDtypeStruct((m, d), BF16),
        compiler_params=_params("parallel", "arbitrary"),
        name=name,
    )(*branches, *weights, z, z, z)


def _rope(x, c, s):
    lane = lax.broadcasted_iota(jnp.int32, x.shape, 1)
    partner = jnp.where((lane % (2 * ROPE_FREQS)) < ROPE_FREQS,
                        pltpu.roll(x, HEAD_DIM - ROPE_FREQS, 1),
                        pltpu.roll(x, ROPE_FREQS, 1))
    return x * c + partner * s


def _attn_kernel(*refs, has_ctx, t, tq, q_group, unit_rows, softmax_rows):
    if has_ctx:
        (zq_ref, zk_ref, zv_ref, gq_ref, gk_ref, cq_ref, sq_ref, ck_ref, sk_ref, pk_ref, pv_ref,
         o_ref, kt_ref, v_ref, *buffers) = refs
    else:
        zq_ref, zk_ref, zv_ref, gq_ref, gk_ref, o_ref, ko_ref, vo_ref, kt_ref, v_ref, *buffers = refs
    s_refs, p_refs = buffers[:SCORE_BUFFERS], buffers[SCORE_BUFFERS:]

    @pl.when(pl.program_id(2) == 0)
    def _():
        k = _rms(zk_ref[...].astype(F32), gk_ref[...])
        v_ref[:, HEAD_DIM:] = jnp.ones((v_ref.shape[0], HEAD_DIM), BF16)
        if has_ctx:
            k = _rope(k, ck_ref[...], sk_ref[...])
            kt_ref[:, 0:t] = k.T.astype(BF16)
            kt_ref[:, t:] = pk_ref[...].T.astype(BF16)
            v_ref[0:t, 0:HEAD_DIM] = zv_ref[...]
            v_ref[t:, 0:HEAD_DIM] = pv_ref[...].astype(BF16)
        else:
            ko_ref[...] = k
            vo_ref[...] = zv_ref[...].astype(F32)
            kt_ref[...] = k.T.astype(BF16)
            v_ref[:, 0:HEAD_DIM] = zv_ref[...]

    q_scale = LOG2_E * HEAD_DIM ** -0.5

    units = [(g, r0) for r0 in range(0, tq, unit_rows) for g in range(q_group)]

    def scores(u):
        g, r0 = units[u]
        rows = slice(r0, r0 + unit_rows)
        q = _rms(zq_ref[rows, g * HEAD_DIM:(g + 1) * HEAD_DIM].astype(F32), gq_ref[...])
        if has_ctx:
            q = _rope(q, cq_ref[rows, :], sq_ref[rows, :])
        s_refs[u % SCORE_BUFFERS][...] = jnp.dot((q * q_scale).astype(BF16), kt_ref[...],
                                                 preferred_element_type=F32)

    def exponentials(u):
        s_ref, p_ref = s_refs[u % SCORE_BUFFERS], p_refs[u % PROB_BUFFERS]
        groups = [slice(r0, r0 + softmax_rows) for r0 in range(0, unit_rows, softmax_rows)]
        maxima = [jnp.max(s_ref[rows, :], axis=-1, keepdims=True) for rows in groups]
        for rows, m in zip(groups, maxima):
            p_ref[rows, :] = jnp.exp2(s_ref[rows, :] - m).astype(BF16)

    def weighted_values(u):
        g, r0 = units[u]
        o = jnp.dot(p_refs[u % PROB_BUFFERS][...], v_ref[...], preferred_element_type=F32)
        o_ref[r0:r0 + unit_rows, g * HEAD_DIM:(g + 1) * HEAD_DIM] = (
            o[:, :HEAD_DIM] * (1.0 / o[:, HEAD_DIM:])).astype(o_ref.dtype)

    lookahead = SCORE_BUFFERS - 1
    for u in range(min(lookahead, len(units))):
        scores(u)
    for u in range(len(units)):
        if u + lookahead < len(units):
            scores(u + lookahead)
        exponentials(u)
        weighted_values(u)


def _attention(z, g_q, g_k, *, n_seq, t, d, kv_w, k_col, v_col, rope=None, past_k=None, past_v=None, layer=None, name):
    has_ctx = rope is not None
    n_kv = kv_w // HEAD_DIM
    q_group = d // kv_w
    qw = q_group * HEAD_DIM
    past = past_k.shape[2] if has_ctx else 0
    tk = t + past
    tq = _row_tile(t, 512)
    nq = t // tq
    unit_rows = _row_tile(tq, 128)
    softmax_rows = min(unit_rows, max(BF16_ROWS, (VREG_BUDGET_ELEMS // tk) // BF16_ROWS * BF16_ROWS))
    assert unit_rows % softmax_rows == 0

    in_specs = [
        pl.BlockSpec((tq, qw), lambda b, h, i: (b * nq + i, h)),
        pl.BlockSpec((t, HEAD_DIM), lambda b, h, i: (b, k_col // HEAD_DIM + h)),
        pl.BlockSpec((t, HEAD_DIM), lambda b, h, i: (b, v_col // HEAD_DIM + h)),
        pl.BlockSpec((1, HEAD_DIM), lambda b, h, i: (0, 0)),
        pl.BlockSpec((1, HEAD_DIM), lambda b, h, i: (0, 0)),
    ]
    args = [z, z, z, g_q.reshape(1, HEAD_DIM), g_k.reshape(1, HEAD_DIM)]
    o_spec = pl.BlockSpec((tq, qw), lambda b, h, i: (b * nq + i, h))
    o_shape = jax.ShapeDtypeStruct((n_seq * t, d), BF16)
    if has_ctx:
        cos, sin = rope
        in_specs += [
            pl.BlockSpec((tq, HEAD_DIM), lambda b, h, i: (i, 0)),
            pl.BlockSpec((tq, HEAD_DIM), lambda b, h, i: (i, 0)),
            pl.BlockSpec((t, HEAD_DIM), lambda b, h, i: (0, 0)),
            pl.BlockSpec((t, HEAD_DIM), lambda b, h, i: (0, 0)),
            pl.BlockSpec((None, None, past, HEAD_DIM), lambda b, h, i: (b, layer, 0, h)),
            pl.BlockSpec((None, None, past, HEAD_DIM), lambda b, h, i: (b, layer, 0, h)),
        ]
        args += [cos, sin, cos, sin, past_k, past_v]
        out_specs = o_spec
        out_shape = o_shape
    else:
        kv_spec = pl.BlockSpec((None, t, HEAD_DIM), lambda b, h, i: (b, 0, h))
        kv_shape = jax.ShapeDtypeStruct((n_seq, t, kv_w), F32)
        out_specs = [o_spec, kv_spec, kv_spec]
        out_shape = [o_shape, kv_shape, kv_shape]

    return pl.pallas_call(
        functools.partial(_attn_kernel, has_ctx=has_ctx, t=t, tq=tq, q_group=q_group, unit_rows=unit_rows,
                          softmax_rows=softmax_rows),
        grid=(n_seq, n_kv, nq),
        in_specs=in_specs,
        out_specs=out_specs,
        out_shape=out_shape,
        scratch_shapes=[
            pltpu.VMEM((HEAD_DIM, tk), BF16),
            pltpu.VMEM((tk, 2 * HEAD_DIM), BF16),
            *[pltpu.VMEM((unit_rows, tk), F32)] * SCORE_BUFFERS,
            *[pltpu.VMEM((unit_rows, tk), BF16)] * PROB_BUFFERS,
        ],
        compiler_params=_params("parallel", "parallel", "arbitrary"),
        name=name,
    )(*args)


def _lru_kernel(zx_ref, zg_ref, cw_ref, cb_ref, wa_ref, wx_ref, ba_ref, bx_ref, lam_ref, h0_ref,
                o_ref, hfin_ref, xpad_ref, xc_ref, hs_ref, *, t, tc):
    cw = zx_ref.shape[1]
    n_lane_blocks = cw // LANES
    n_chunks = t // tc
    pad = SUBLANES
    steps = tc // SUBLANES
    assert steps % SUBLANES == 0

    def natural_rows(r, m):
        seg, j0 = divmod(m * SUBLANES, steps)
        return pl.ds(r + SUBLANES * j0 + seg, SUBLANES, stride=SUBLANES)

    xpad_ref[0:pad, :] = jnp.zeros((pad, cw), F32)
    xpad_ref[t + pad:t + 2 * pad, :] = jnp.zeros((pad, cw), F32)

    def copy_chunk(c, carry):
        r = pl.multiple_of(c * tc, tc)
        xpad_ref[pl.ds(r + pad, tc), :] = zx_ref[pl.ds(r, tc), :].astype(F32)
        return carry

    lax.fori_loop(0, n_chunks, copy_chunk, 0)

    def conv_chunk(c, carry):
        r = pl.multiple_of(c * tc, tc)
        xe = xpad_ref[pl.ds(r, tc + 2 * pad), :]
        n = tc + 2 * pad
        y = cb_ref[...]
        for tap in range(CONV_W):
            shifted = pltpu.roll(xe, (CONV_PAD_L - tap) % n, 0) if tap != CONV_PAD_L else xe
            y = y + shifted[pad:pad + tc] * cw_ref[tap:tap + 1, :]
        for nb in range(n_lane_blocks):
            for m in range(steps):
                xc_ref[nb, natural_rows(r, m), :] = y[m * SUBLANES:(m + 1) * SUBLANES, nb * LANES:(nb + 1) * LANES]
        return carry

    lax.fori_loop(0, n_chunks, conv_chunk, 0)

    sublane = lax.broadcasted_iota(jnp.int32, (SUBLANES, LANES), 0)

    half_k = [[(-0.5 * LRU_C * LOG2_E) * jax.nn.softplus(-lam_ref[direction:direction + 1, nb * LANES:(nb + 1) * LANES])
               for nb in range(n_lane_blocks)] for direction in range(2)]

    half_ba = [[0.5 * ba_ref[direction:direction + 1, nb * LANES:(nb + 1) * LANES]
                for nb in range(n_lane_blocks)] for direction in range(2)]
    half_bx = [[0.5 * bx_ref[direction:direction + 1, nb * LANES:(nb + 1) * LANES]
                for nb in range(n_lane_blocks)] for direction in range(2)]

    def gates(xn, direction, nb):
        xb = xn.astype(BF16)
        tanh_r = jnp.tanh(jnp.dot(xb, wa_ref[direction, nb], preferred_element_type=F32) + half_ba[direction][nb])
        tanh_i = jnp.tanh(jnp.dot(xb, wx_ref[direction, nb], preferred_element_type=F32) + half_bx[direction][nb])
        a = jnp.exp2(half_k[direction][nb] * tanh_r + half_k[direction][nb])
        scale = jnp.exp2((0.5 * LOG2_E) * jnp.log(jnp.maximum(1.0 - a * a, 0.0)))
        b = scale * ((0.5 * tanh_i + 0.5) * xn)
        return a, b

    def chunk_scan(a, b, carry, reverse):
        order = range(steps - 1, -1, -1) if reverse else range(steps)
        local, decay = [None] * steps, [None] * steps
        h, p = None, None
        for j in order:
            aj, bj = a[j * SUBLANES:(j + 1) * SUBLANES], b[j * SUBLANES:(j + 1) * SUBLANES]
            h = bj if h is None else aj * h + bj
            p = aj if p is None else aj * p
            local[j], decay[j] = h, p
        seg_h, seg_p = h, p
        for dist in (1, 2, 4):
            if reverse:
                edge, shift = sublane >= SUBLANES - dist, SUBLANES - dist
            else:
                edge, shift = sublane < dist, dist
            h_prev = jnp.where(edge, 0.0, pltpu.roll(seg_h, shift, 0))
            p_prev = jnp.where(edge, 1.0, pltpu.roll(seg_p, shift, 0))
            seg_h = seg_p * h_prev + seg_h
            seg_p = seg_p * p_prev
        seg_state = seg_h + seg_p * carry
        if reverse:
            entering = jnp.where(sublane == SUBLANES - 1, carry, pltpu.roll(seg_state, SUBLANES - 1, 0))
            leaving = jnp.broadcast_to(seg_state[0:1], (SUBLANES, LANES))
        else:
            entering = jnp.where(sublane == 0, carry, pltpu.roll(seg_state, 1, 0))
            leaving = jnp.broadcast_to(seg_state[SUBLANES - 1:SUBLANES], (SUBLANES, LANES))
        return [local[j] + decay[j] * entering for j in range(steps)], leaving

    def forward_chunk(c, carry):
        r = pl.multiple_of(c * tc, tc)
        new_carry = []
        for nb in range(n_lane_blocks):
            lanes = slice(nb * LANES, (nb + 1) * LANES)
            a, b = gates(xc_ref[nb, pl.ds(r, tc), :], 0, nb)
            states, leaving = chunk_scan(a, b, carry[nb], reverse=False)
            for j in range(steps):
                hs_ref[nb, pl.ds(r + j * SUBLANES, SUBLANES), :] = states[j]
            new_carry.append(leaving)
        return tuple(new_carry)

    h0f = tuple(jnp.broadcast_to(h0_ref[0:1, nb * LANES:(nb + 1) * LANES], (SUBLANES, LANES))
                for nb in range(n_lane_blocks))
    scan_unroll = min(SCAN_UNROLL, n_chunks)
    hf_last = lax.fori_loop(0, n_chunks, forward_chunk, h0f, unroll=scan_unroll)

    def backward_chunk(c, carry):
        r = pl.multiple_of((n_chunks - 1 - c) * tc, tc)
        new_carry = []
        for nb in range(n_lane_blocks):
            lanes = slice(nb * LANES, (nb + 1) * LANES)
            a, b = gates(xc_ref[nb, pl.ds(r, tc), :], 1, nb)
            states, leaving = chunk_scan(a, b, carry[nb], reverse=True)
            for j in range(steps):
                hs_ref[nb, pl.ds(r + j * SUBLANES, SUBLANES), :] += states[j]
            new_carry.append(leaving)
        return tuple(new_carry)

    h0b = tuple(jnp.broadcast_to(h0_ref[1:2, nb * LANES:(nb + 1) * LANES], (SUBLANES, LANES))
                for nb in range(n_lane_blocks))
    hb_first = lax.fori_loop(0, n_chunks, backward_chunk, h0b, unroll=scan_unroll)

    for nb in range(n_lane_blocks):
        lanes = slice(nb * LANES, (nb + 1) * LANES)
        hfin_ref[0:1, lanes] = hf_last[nb][0:1]
        hfin_ref[1:2, lanes] = hb_first[nb][0:1]

    def gate_chunk(c, carry):
        r = pl.multiple_of(c * tc, tc)
        for nb in range(n_lane_blocks):
            lanes = slice(nb * LANES, (nb + 1) * LANES)
            for m in range(0, steps, 2):
                rows = pl.ds(r + m * SUBLANES, BF16_ROWS)
                hsum = jnp.concatenate([hs_ref[nb, natural_rows(r, m), :], hs_ref[nb, natural_rows(r, m + 1), :]],
                                       axis=0)
                o_ref[rows, lanes] = (hsum * jax.nn.gelu(zg_ref[rows, lanes].astype(F32))).astype(o_ref.dtype)
        return carry

    lax.fori_loop(0, n_chunks, gate_chunk, 0)


def _rglru(z, conv_w, conv_b, wa, wx, ba, bx, lam, h0, *, n_seq, t, width, x_col, g_col, name):
    cw = _row_tile(width, 1024)
    tc = _row_tile(t, 128)
    lb = cw // LANES
    assert wa.shape[-1] == LANES and x_col % cw == 0 and g_col % cw == 0
    vec_spec = pl.BlockSpec((2, cw), lambda s, c: (0, c))
    w_spec = pl.BlockSpec((2, lb, LANES, LANES), lambda s, c: (0, c, 0, 0))
    return pl.pallas_call(
        functools.partial(_lru_kernel, t=t, tc=tc),
        grid=(n_seq, width // cw),
        in_specs=[
            pl.BlockSpec((t, cw), lambda s, c: (s, x_col // cw + c)),
            pl.BlockSpec((t, cw), lambda s, c: (s, g_col // cw + c)),
            pl.BlockSpec((CONV_W, cw), lambda s, c: (0, c)),
            pl.BlockSpec((1, cw), lambda s, c: (0, c)),
            w_spec, w_spec, vec_spec, vec_spec, vec_spec,
            pl.BlockSpec((None, 2, cw), lambda s, c: (s, 0, c)),
        ],
        out_specs=[
            pl.BlockSpec((t, cw), lambda s, c: (s, c)),
            pl.BlockSpec((None, 2, cw), lambda s, c: (s, 0, c)),
        ],
        out_shape=[
            jax.ShapeDtypeStruct((n_seq * t, width), BF16),
            jax.ShapeDtypeStruct((n_seq, 2, width), F32),
        ],
        scratch_shapes=[
            pltpu.VMEM((t + 2 * SUBLANES, cw), F32),
            pltpu.VMEM((lb, t, LANES), F32),
            pltpu.VMEM((lb, t, LANES), F32),
        ],
        compiler_params=_params("parallel", "parallel"),
        name=name,
    )(z, z, conv_w, conv_b.reshape(1, width), wa, wx, ba, bx, lam, h0)


def _cmix_kernel(zu_ref, zv_ref, g_ref, ws_ref, bs_ref, o_ref, v_ref, *, n_groups):
    rows, width = zu_ref.shape
    gw = width // n_groups
    def norm_chunk(c, carry):
        r = pl.ds(pl.multiple_of(c * ROW_CHUNK, ROW_CHUNK), ROW_CHUNK)
        v_ref[r, :] = _rms(jax.nn.gelu(zv_ref[r, :].astype(F32)), g_ref[...]).astype(BF16)
        return carry

    lax.fori_loop(0, rows // ROW_CHUNK, norm_chunk, 0, unroll=NORM_UNROLL)
    for ch in range(rows // CHUNK):
        r = slice(ch * CHUNK, (ch + 1) * CHUNK)
        for g in range(n_groups):
            cols = slice(g * gw, (g + 1) * gw)
            mixed = jnp.dot(ws_ref[g], v_ref[r, cols], preferred_element_type=F32) + bs_ref[g]
            o_ref[r, cols] = (jax.nn.gelu(zu_ref[r, cols].astype(F32)) * mixed).astype(o_ref.dtype)


def _chunk_mix(z, cm_g, ws, bs, *, width, u_col, v_col, name):
    m = z.shape[0]
    n_groups = ws.shape[0]
    tr = _row_tile(m, 2 * CHUNK)
    assert u_col % width == 0 and v_col % width == 0
    return pl.pallas_call(
        functools.partial(_cmix_kernel, n_groups=n_groups),
        grid=(m // tr,),
        in_specs=[
            pl.BlockSpec((tr, width), lambda i: (i, u_col // width)),
            pl.BlockSpec((tr, width), lambda i: (i, v_col // width)),
            pl.BlockSpec((1, width), lambda i: (0, 0)),
            pl.BlockSpec(ws.shape, lambda i: (0, 0, 0)),
            pl.BlockSpec(bs.shape, lambda i: (0, 0, 0)),
        ],
        out_specs=pl.BlockSpec((tr, width), lambda i: (i, 0)),
        out_shape=jax.ShapeDtypeStruct((m, width), BF16),
        scratch_shapes=[pltpu.VMEM((tr, width), BF16)],
        compiler_params=_params("parallel"),
        name=name,
    )(z, z, cm_g.reshape(1, width), ws, bs)


def _rope_tables(n_tokens):
    rows = n_tokens // GRID_W
    pos_row = jnp.repeat(jnp.arange(rows), GRID_W).astype(F32)
    pos_col = (jnp.arange(n_tokens) % GRID_W).astype(F32)
    inv = ROPE_BASE ** (-jnp.arange(ROPE_FREQS, dtype=F32) / ROPE_FREQS)
    cr, sr = jnp.cos(pos_row[:, None] * inv), jnp.sin(pos_row[:, None] * inv)
    cc, sc = jnp.cos(pos_col[:, None] * inv), jnp.sin(pos_col[:, None] * inv)
    return (jnp.concatenate([cr, cr, cc, cc], axis=-1), jnp.concatenate([-sr, sr, -sc, sc], axis=-1))


def _layer(x, mod, p, tag, *, n_seq, t, rows_per_cond, rope=None, past_k=None, past_v=None, layer=None, h0=None):
    d = x.shape[1]
    kv_w = p["kv_w"]
    col_lx, col_lg, col_cu, col_cv, col_g, col_k = d, 2 * d, 3 * d, 4 * d, 5 * d, 8 * d
    col_v = col_k + kv_w

    tn = 2 * kv_w
    assert d % tn == 0
    q_blocks, rest_blocks = d // tn, 7 * d // tn

    def w_col_block(j):
        return jnp.where(j < q_blocks, j, jnp.where(j < q_blocks + rest_blocks, j + 1, q_blocks))

    li = p["layer"]
    z = _prenorm_matmul(x, p["g_pre_mix"], mod, p["w_in"], li, rows_per_cond=rows_per_cond,
                        shift_row=MOD_SHIFT1, scale_row=MOD_SCALE1, relu2=False, name=f"in_proj_{tag}",
                        tn=tn, w_col_block=w_col_block)
    attn = _attention(z, p["g_q"], p["g_k"], n_seq=n_seq, t=t, d=d, kv_w=kv_w, k_col=col_k, v_col=col_v,
                      rope=rope, past_k=past_k, past_v=past_v, layer=layer, name=f"attention_{tag}")
    if rope is None:
        attn_o, k_new, v_new = attn
    else:
        attn_o, k_new, v_new = attn, None, None
    lru_o, h_fin = _rglru(z, p["conv_w"], p["conv_b"], p["lru_wa"], p["lru_wx"], p["lru_ba"], p["lru_bx"],
                          p["lru_lam"], h0, n_seq=n_seq, t=t, width=d, x_col=col_lx, g_col=col_lg,
                          name=f"rglru_{tag}")
    cm_o = _chunk_mix(z, p["cm_g"], p["cm_ws"], p["cm_bs"], width=d, u_col=col_cu, v_col=col_cv,
                      name=f"chunk_mix_{tag}")
    merged = _merge((attn_o, lru_o, cm_o), (p["w_attn_out"], p["w_lru_out"], p["w_cm_out"]), li, z, col_g,
                    name=f"merge_{tag}")
    x = _matmul_norm_residual(merged, p["w_out"], li, p["g_post_mix"], mod, x, rows_per_cond=rows_per_cond,
                              gate_row=MOD_GATE1, name=f"out_proj_{tag}")
    f = _prenorm_matmul(x, p["g_pre_ff"], mod, p["w_ff1"], li, rows_per_cond=rows_per_cond,
                        shift_row=MOD_SHIFT2, scale_row=MOD_SCALE2, relu2=True, name=f"ff1_{tag}")
    x = _matmul_norm_residual(f, p["w_ff2"], li, p["g_post_ff"], mod, x, rows_per_cond=rows_per_cond,
                              gate_row=MOD_GATE2, name=f"ff2_{tag}")
    return x, k_new, v_new, h_fin


def kernel(x_prompt, x_sample, cache_k, cache_v, state_lru, c, c_ctx, w_mod, b_mod, g_pre_mix, g_post_mix, g_pre_ff, g_post_ff, w_in, g_q, g_k, w_attn_out, conv_w, conv_b, lru_wa, lru_ba, lru_wx, lru_bx, lru_lam, w_lru_out, cm_g, cm_ws, cm_bs, w_cm_out, w_out, w_ff1, w_ff2):
    batch, seq, d = x_prompt.shape
    dec_batch, dec_seq, _ = x_sample.shape
    depth = w_in.shape[0]
    in_w = w_in.shape[2]
    kv_w = (in_w - 8 * d) // 2
    n_kv = kv_w // HEAD_DIM
    past = cache_k.shape[2]
    n_groups = cm_ws.shape[1]

    n_cond = 1 + dec_batch
    cond = jnp.concatenate([c_ctx[None, :], c], axis=0)
    cond = jnp.pad(cond, ((0, -n_cond % SUBLANES), (0, 0)))
    mod = _modulation(cond, w_mod, b_mod)

    w_in_b = w_in.astype(BF16)
    w_attn_out_b, w_lru_out_b, w_cm_out_b = w_attn_out.astype(BF16), w_lru_out.astype(BF16), w_cm_out.astype(BF16)
    w_out_b, w_ff1_b, w_ff2_b = w_out.astype(BF16), w_ff1.astype(BF16), w_ff2.astype(BF16)
    lru_wa_b, lru_wx_b, cm_ws_b = (0.5 * lru_wa).astype(BF16), (0.5 * lru_wx).astype(BF16), cm_ws.astype(BF16)
    cm_bs_b = jnp.broadcast_to(cm_bs[:, :, :, None], cm_bs.shape + (d // n_groups,))

    rope = _rope_tables(dec_seq)
    past_k = cache_k.reshape(dec_batch, depth, past, kv_w)
    past_v = cache_v.reshape(dec_batch, depth, past, kv_w)
    h0_prompt = jnp.zeros((batch, 2, d), F32)

    y_p = x_prompt.reshape(batch * seq, d)
    y_s = x_sample.reshape(dec_batch * dec_seq, d)
    new_k, new_v, new_s = [], [], []
    for l in range(depth):
        p = {
            "kv_w": kv_w, "layer": l, "g_pre_mix": g_pre_mix[l], "g_post_mix": g_post_mix[l],
            "g_pre_ff": g_pre_ff[l], "g_post_ff": g_post_ff[l], "w_in": w_in_b, "g_q": g_q[l], "g_k": g_k[l],
            "w_attn_out": w_attn_out_b, "conv_w": conv_w[l], "conv_b": conv_b[l],
            "lru_wa": lru_wa_b[l], "lru_ba": lru_ba[l], "lru_wx": lru_wx_b[l], "lru_bx": lru_bx[l],
            "lru_lam": lru_lam[l], "w_lru_out": w_lru_out_b, "cm_g": cm_g[l], "cm_ws": cm_ws_b[l],
            "cm_bs": cm_bs_b[l], "w_cm_out": w_cm_out_b, "w_out": w_out_b, "w_ff1": w_ff1_b,
            "w_ff2": w_ff2_b,
        }
        y_p, k_l, v_l, s_l = _layer(y_p, mod[l, 0:1], p, f"ctx{l}", n_seq=batch, t=seq,
                                    rows_per_cond=batch * seq, h0=h0_prompt)
        new_k.append(k_l.reshape(batch, seq, n_kv, HEAD_DIM))
        new_v.append(v_l.reshape(batch, seq, n_kv, HEAD_DIM))
        new_s.append(s_l)
        y_s, _, _, _ = _layer(y_s, mod[l, 1:1 + dec_batch], p, f"lat{l}", n_seq=dec_batch, t=dec_seq,
                              rows_per_cond=dec_seq, rope=rope, past_k=past_k, past_v=past_v, layer=l,
                              h0=state_lru[:, l])
    return (y_p.reshape(batch, seq, d), y_s.reshape(dec_batch, dec_seq, d),
            jnp.stack(new_k, axis=1), jnp.stack(new_v, axis=1), jnp.stack(new_s, axis=1))
```

```python
import functools

import jax
import jax.numpy as jnp
from jax import lax
from jax.experimental import pallas as pl
from jax.experimental.pallas import tpu as pltpu

F32 = jnp.float32
BF16 = jnp.bfloat16

EPS = 1e-6
HEAD_DIM = 128
GRID_W = 64
ROPE_BASE = 10000.0
ROPE_FREQS = HEAD_DIM // 4
CONV_W = 4
CONV_PAD_L = 2
LRU_C = 8.0
CHUNK = 128
N_MOD = 6
N_BRANCH = 3
LOG2_E = 1.4426950408889634

LANES = 128
SUBLANES = 8
VMEM_LIMIT_BYTES = 56 * 1024 * 1024
BF16_ROWS = 2 * SUBLANES
ROW_CHUNK = BF16_ROWS
SCORE_BUFFERS = 3
PROB_BUFFERS = 2
SCAN_UNROLL = 4
NORM_UNROLL = 8
VREG_BUDGET_ELEMS = 32 * SUBLANES * LANES

MOD_SHIFT1, MOD_SCALE1, MOD_GATE1, MOD_SHIFT2, MOD_SCALE2, MOD_GATE2 = range(N_MOD)


def _params(*semantics):
    return pltpu.CompilerParams(dimension_semantics=semantics, vmem_limit_bytes=VMEM_LIMIT_BYTES)


def _rms(x, g):
    return x * lax.rsqrt(jnp.mean(x * x, axis=-1, keepdims=True) + EPS) * g


def _sigmoid(x):
    return 0.5 * jnp.tanh(0.5 * x) + 0.5


def _row_tile(m, cap):
    if m <= cap:
        return m
    t = cap - cap % LANES
    while m % t:
        t -= LANES
    assert t > 0
    return t


def _mod_kernel(c_ref, w_ref, b_ref, o_ref):
    c = c_ref[...]
    s = (c * jax.nn.sigmoid(c)).astype(BF16)
    o_ref[...] = jnp.dot(s, w_ref[...].astype(BF16), preferred_element_type=F32) + b_ref[...]


def _modulation(cond, w_mod, b_mod):
    nc, d = cond.shape
    depth, _, n = w_mod.shape
    tn = _row_tile(n, 1024)
    out = pl.pallas_call(
        _mod_kernel,
        grid=(depth, n // tn),
        in_specs=[
            pl.BlockSpec((nc, d), lambda l, j: (0, 0)),
            pl.BlockSpec((None, d, tn), lambda l, j: (l, 0, j)),
            pl.BlockSpec((None, 1, tn), lambda l, j: (l, 0, j)),
        ],
        out_specs=pl.BlockSpec((None, nc, tn), lambda l, j: (l, 0, j)),
        out_shape=jax.ShapeDtypeStruct((depth, nc, n), F32),
        compiler_params=_params("parallel", "parallel"),
        name="modulation",
    )(cond, w_mod, b_mod.reshape(depth, 1, n))
    return out.reshape(depth, nc, N_MOD, d)


def _prenorm_mm_kernel(x_ref, g_ref, mod_ref, w_ref, o_ref, h_ref, *, shift_row, scale_row, relu2,
                       rows_per_cond, n_tiles, n_slices):
    i, j = pl.program_id(0), pl.program_id(1)
    tm = x_ref.shape[0]

    def gain_shift(tile):
        cond = (tile * tm) // rows_per_cond
        return (g_ref[...] * (1.0 + mod_ref[cond, scale_row:scale_row + 1, :]),
                mod_ref[cond, shift_row:shift_row + 1, :])

    def normalise(rows, slot, gain, shift):
        x = x_ref[rows, :]
        inv = lax.rsqrt(jnp.mean(x * x, axis=-1, keepdims=True) + EPS)
        h_ref[slot, rows, :] = (x * inv * gain + shift).astype(BF16)

    @pl.when(jnp.logical_and(i == 0, j == 0))
    def _():
        gain, shift = gain_shift(0)

        def chunk(c, carry):
            normalise(pl.ds(pl.multiple_of(c * ROW_CHUNK, ROW_CHUNK), ROW_CHUNK), 0, gain, shift)
            return carry

        lax.fori_loop(0, tm // ROW_CHUNK, chunk, 0, unroll=NORM_UNROLL)

    slice_rows = tm // n_slices
    gain, shift = gain_shift(jnp.minimum(i + 1, n_tiles - 1))

    def normalise_slice(s):
        row0 = pl.multiple_of(s * slice_rows, slice_rows)
        for c in range(slice_rows // ROW_CHUNK):
            normalise(pl.ds(row0 + c * ROW_CHUNK, ROW_CHUNK), (i + 1) % 2, gain, shift)

    @pl.when(jnp.logical_and(i == 0, j == 1))
    def _():
        normalise_slice(0)

    normalise_slice(jnp.minimum(j, n_slices - 1))

    acc = jnp.dot(h_ref[i % 2], w_ref[...], preferred_element_type=F32)
    if relu2:
        acc = jnp.square(jnp.maximum(acc, 0.0))
    o_ref[...] = acc.astype(o_ref.dtype)


def _prenorm_matmul(x, g, mod, w, layer, *, rows_per_cond, shift_row, scale_row, relu2, name, tn=None,
                    w_col_block=lambda j: j):
    m, d = x.shape
    n = w.shape[2]
    tm = _row_tile(rows_per_cond, 1024)
    tn = tn or _row_tile(n, 1024)
    n_tiles, nj = m // tm, n // tn
    assert rows_per_cond % tm == 0 and n % tn == 0 and nj >= 2
    n_slices = max(s for s in range(1, nj + 1) if (tm // ROW_CHUNK) % s == 0)

    def x_tile(i, j):
        return jnp.where(jnp.logical_and(i == 0, j == 0), 0, jnp.minimum(i + 1, n_tiles - 1))

    return pl.pallas_call(
        functools.partial(_prenorm_mm_kernel, shift_row=shift_row, scale_row=scale_row, relu2=relu2,
                          rows_per_cond=rows_per_cond, n_tiles=n_tiles, n_slices=n_slices),
        grid=(n_tiles, nj),
        in_specs=[
            pl.BlockSpec((tm, d), lambda i, j: (x_tile(i, j), 0)),
            pl.BlockSpec((1, d), lambda i, j: (0, 0)),
            pl.BlockSpec(mod.shape, lambda i, j: (0, 0, 0)),
            pl.BlockSpec((None, d, tn), lambda i, j: (layer, 0, w_col_block(j))),
        ],
        out_specs=pl.BlockSpec((tm, tn), lambda i, j: (i, j)),
        out_shape=jax.ShapeDtypeStruct((m, n), BF16),
        scratch_shapes=[pltpu.VMEM((2, tm, d), BF16)],
        compiler_params=_params("arbitrary", "arbitrary"),
        name=name,
    )(x, g.reshape(1, d), mod, w)


def _mm_norm_res_kernel(a_ref, w_ref, g_ref, mod_ref, x_ref, o_ref, acc_ref, *, gate_row, nk):
    def product():
        return jnp.dot(a_ref[...], w_ref[...], preferred_element_type=F32)

    def finish():
        gain = mod_ref[gate_row:gate_row + 1, :] * g_ref[...]

        def chunk(c, carry):
            rows = pl.ds(pl.multiple_of(c * ROW_CHUNK, ROW_CHUNK), ROW_CHUNK)
            y = acc_ref[rows, :]
            inv = lax.rsqrt(jnp.mean(y * y, axis=-1, keepdims=True) + EPS)
            o_ref[rows, :] = x_ref[rows, :] + y * inv * gain
            return carry

        lax.fori_loop(0, o_ref.shape[0] // ROW_CHUNK, chunk, 0, unroll=NORM_UNROLL)

    if nk == 1:
        acc_ref[...] = product()
        finish()
        return
    k = pl.program_id(1)

    @pl.when(k == 0)
    def _():
        acc_ref[...] = product()

    @pl.when(k > 0)
    def _():
        acc_ref[...] += product()

    @pl.when(k == nk - 1)
    def _():
        finish()


def _matmul_norm_residual(a, w, layer, g, mod, x, *, rows_per_cond, gate_row, name):
    m, kdim = a.shape
    d = w.shape[2]
    tk = _row_tile(kdim, 2048)
    nk = kdim // tk
    tm = _row_tile(rows_per_cond, 512)
    assert rows_per_cond % tm == 0
    return pl.pallas_call(
        functools.partial(_mm_norm_res_kernel, gate_row=gate_row, nk=nk),
        grid=(m // tm, nk),
        in_specs=[
            pl.BlockSpec((tm, tk), lambda i, k: (i, k)),
            pl.BlockSpec((None, tk, d), lambda i, k: (layer, k, 0)),
            pl.BlockSpec((1, d), lambda i, k: (0, 0)),
            pl.BlockSpec((None, N_MOD, d), lambda i, k: ((i * tm) // rows_per_cond, 0, 0)),
            pl.BlockSpec((tm, d), lambda i, k: (i, 0)),
        ],
        out_specs=pl.BlockSpec((tm, d), lambda i, k: (i, 0)),
        out_shape=jax.ShapeDtypeStruct((m, d), F32),
        scratch_shapes=[pltpu.VMEM((tm, d), F32)],
        compiler_params=_params("parallel", "arbitrary"),
        name=name,
    )(a, w, g.reshape(1, d), mod, x)


def _merge_kernel(a0_ref, a1_ref, a2_ref, w0_ref, w1_ref, w2_ref, g0_ref, g1_ref, g2_ref, o_ref):
    acc = _sigmoid(g0_ref[...].astype(F32)) * jnp.dot(a0_ref[...], w0_ref[...], preferred_element_type=F32)
    acc += _sigmoid(g1_ref[...].astype(F32)) * jnp.dot(a1_ref[...], w1_ref[...], preferred_element_type=F32)
    acc += _sigmoid(g2_ref[...].astype(F32)) * jnp.dot(a2_ref[...], w2_ref[...], preferred_element_type=F32)
    o_ref[...] = acc.astype(o_ref.dtype)


def _merge(branches, weights, layer, z, gate_col0, name):
    m, d = branches[0].shape
    tm = _row_tile(m, 1024)
    tn = _row_tile(d, 512)
    assert gate_col0 % tn == 0 and d % tn == 0
    a_spec = pl.BlockSpec((tm, d), lambda i, j: (i, 0))
    w_spec = pl.BlockSpec((None, d, tn), lambda i, j: (layer, 0, j))

    def gate_spec(b):
        off = (gate_col0 + b * d) // tn
        return pl.BlockSpec((tm, tn), lambda i, j: (i, off + j))

    return pl.pallas_call(
        _merge_kernel,
        grid=(m // tm, d // tn),
        in_specs=[a_spec] * N_BRANCH + [w_spec] * N_BRANCH + [gate_spec(b) for b in range(N_BRANCH)],
        out_specs=pl.BlockSpec((tm, tn), lambda i, j: (i, j)),
        out_shape=jax.ShapeDtypeStruct((m, d), BF16),
        compiler_params=_params("parallel", "arbitrary"),
        name=name,
    )(*branches, *weights, z, z, z)


def _rope(x, c, s):
    lane = lax.broadcasted_iota(jnp.int32, x.shape, 1)
    partner = jnp.where((lane % (2 * ROPE_FREQS)) < ROPE_FREQS,
                        pltpu.roll(x, HEAD_DIM - ROPE_FREQS, 1),
                        pltpu.roll(x, ROPE_FREQS, 1))
    return x * c + partner * s


def _attn_kernel(*refs, has_ctx, t, tq, q_group, unit_rows, softmax_rows):
    if has_ctx:
        (zq_ref, zk_ref, zv_ref, gq_ref, gk_ref, cq_ref, sq_ref, ck_ref, sk_ref, pk_ref, pv_ref,
         o_ref, kt_ref, v_ref, *buffers) = refs
    else:
        zq_ref, zk_ref, zv_ref, gq_ref, gk_ref, o_ref, ko_ref, vo_ref, kt_ref, v_ref, *buffers = refs
    s_refs, p_refs = buffers[:SCORE_BUFFERS], buffers[SCORE_BUFFERS:]

    @pl.when(pl.program_id(2) == 0)
    def _():
        k = _rms(zk_ref[...].astype(F32), gk_ref[...])
        v_ref[:, HEAD_DIM:] = jnp.ones((v_ref.shape[0], HEAD_DIM), BF16)
        if has_ctx:
            k = _rope(k, ck_ref[...], sk_ref[...])
            kt_ref[:, 0:t] = k.T.astype(BF16)
            kt_ref[:, t:] = pk_ref[...].T.astype(BF16)
            v_ref[0:t, 0:HEAD_DIM] = zv_ref[...]
            v_ref[t:, 0:HEAD_DIM] = pv_ref[...].astype(BF16)
        else:
            ko_ref[...] = k
            vo_ref[...] = zv_ref[...].astype(F32)
            kt_ref[...] = k.T.astype(BF16)
            v_ref[:, 0:HEAD_DIM] = zv_ref[...]

    q_scale = LOG2_E * HEAD_DIM ** -0.5

    units = [(g, r0) for r0 in range(0, tq, unit_rows) for g in range(q_group)]

    def scores(u):
        g, r0 = units[u]
        rows = slice(r0, r0 + unit_rows)
        q = _rms(zq_ref[rows, g * HEAD_DIM:(g + 1) * HEAD_DIM].astype(F32), gq_ref[...])
        if has_ctx:
            q = _rope(q, cq_ref[rows, :], sq_ref[rows, :])
        s_refs[u % SCORE_BUFFERS][...] = jnp.dot((q * q_scale).astype(BF16), kt_ref[...],
                                                 preferred_element_type=F32)

    def exponentials(u):
        s_ref, p_ref = s_refs[u % SCORE_BUFFERS], p_refs[u % PROB_BUFFERS]
        groups = [slice(r0, r0 + softmax_rows) for r0 in range(0, unit_rows, softmax_rows)]
        maxima = [jnp.max(s_ref[rows, :], axis=-1, keepdims=True) for rows in groups]
        for rows, m in zip(groups, maxima):
            p_ref[rows, :] = jnp.exp2(s_ref[rows, :] - m).astype(BF16)

    def weighted_values(u):
        g, r0 = units[u]
        o = jnp.dot(p_refs[u % PROB_BUFFERS][...], v_ref[...], preferred_element_type=F32)
        o_ref[r0:r0 + unit_rows, g * HEAD_DIM:(g + 1) * HEAD_DIM] = (
            o[:, :HEAD_DIM] * (1.0 / o[:, HEAD_DIM:])).astype(o_ref.dtype)

    lookahead = SCORE_BUFFERS - 1
    for u in range(min(lookahead, len(units))):
        scores(u)
    for u in range(len(units)):
        if u + lookahead < len(units):
            scores(u + lookahead)
        exponentials(u)
        weighted_values(u)


def _attention(z, g_q, g_k, *, n_seq, t, d, kv_w, k_col, v_col, rope=None, past_k=None, past_v=None, layer=None, name):
    has_ctx = rope is not None
    n_kv = kv_w // HEAD_DIM
    q_group = d // kv_w
    qw = q_group * HEAD_DIM
    past = past_k.shape[2] if has_ctx else 0
    tk = t + past
    tq = _row_tile(t, 512)
    nq = t // tq
    unit_rows = _row_tile(tq, 128)
    softmax_rows = min(unit_rows, max(BF16_ROWS, (VREG_BUDGET_ELEMS // tk) // BF16_ROWS * BF16_ROWS))
    assert unit_rows % softmax_rows == 0

    in_specs = [
        pl.BlockSpec((tq, qw), lambda b, h, i: (b * nq + i, h)),
        pl.BlockSpec((t, HEAD_DIM), lambda b, h, i: (b, k_col // HEAD_DIM + h)),
        pl.BlockSpec((t, HEAD_DIM), lambda b, h, i: (b, v_col // HEAD_DIM + h)),
        pl.BlockSpec((1, HEAD_DIM), lambda b, h, i: (0, 0)),
        pl.BlockSpec((1, HEAD_DIM), lambda b, h, i: (0, 0)),
    ]
    args = [z, z, z, g_q.reshape(1, HEAD_DIM), g_k.reshape(1, HEAD_DIM)]
    o_spec = pl.BlockSpec((tq, qw), lambda b, h, i: (b * nq + i, h))
    o_shape = jax.ShapeDtypeStruct((n_seq * t, d), BF16)
    if has_ctx:
        cos, sin = rope
        in_specs += [
            pl.BlockSpec((tq, HEAD_DIM), lambda b, h, i: (i, 0)),
            pl.BlockSpec((tq, HEAD_DIM), lambda b, h, i: (i, 0)),
            pl.BlockSpec((t, HEAD_DIM), lambda b, h, i: (0, 0)),
            pl.BlockSpec((t, HEAD_DIM), lambda b, h, i: (0, 0)),
            pl.BlockSpec((None, None, past, HEAD_DIM), lambda b, h, i: (b, layer, 0, h)),
            pl.BlockSpec((None, None, past, HEAD_DIM), lambda b, h, i: (b, layer, 0, h)),
        ]
        args += [cos, sin, cos, sin, past_k, past_v]
        out_specs = o_spec
        out_shape = o_shape
    else:
        kv_spec = pl.BlockSpec((None, t, HEAD_DIM), lambda b, h, i: (b, 0, h))
        kv_shape = jax.ShapeDtypeStruct((n_seq, t, kv_w), F32)
        out_specs = [o_spec, kv_spec, kv_spec]
        out_shape = [o_shape, kv_shape, kv_shape]

    return pl.pallas_call(
        functools.partial(_attn_kernel, has_ctx=has_ctx, t=t, tq=tq, q_group=q_group, unit_rows=unit_rows,
                          softmax_rows=softmax_rows),
        grid=(n_seq, n_kv, nq),
        in_specs=in_specs,
        out_specs=out_specs,
        out_shape=out_shape,
        scratch_shapes=[
            pltpu.VMEM((HEAD_DIM, tk), BF16),
            pltpu.VMEM((tk, 2 * HEAD_DIM), BF16),
            *[pltpu.VMEM((unit_rows, tk), F32)] * SCORE_BUFFERS,
            *[pltpu.VMEM((unit_rows, tk), BF16)] * PROB_BUFFERS,
        ],
        compiler_params=_params("parallel", "parallel", "arbitrary"),
        name=name,
    )(*args)


EXPM1_LINEAR_RANGE = 2.0 ** -11


def _lru_kernel(zx_ref, zg_ref, cw_ref, cb_ref, wa_ref, wx_ref, ba_ref, bx_ref, lam_ref, h0_ref,
                o_ref, hfin_ref, xpad_ref, xc_ref, hs_ref, *, t, tc):
    cw = zx_ref.shape[1]
    n_lane_blocks = cw // LANES
    n_chunks = t // tc
    pad = SUBLANES
    steps = tc // SUBLANES
    assert steps % SUBLANES == 0

    def natural_rows(r, m):
        seg, j0 = divmod(m * SUBLANES, steps)
        return pl.ds(r + SUBLANES * j0 + seg, SUBLANES, stride=SUBLANES)

    xpad_ref[0:pad, :] = jnp.zeros((pad, cw), F32)
    xpad_ref[t + pad:t + 2 * pad, :] = jnp.zeros((pad, cw), F32)

    def copy_chunk(c, carry):
        r = pl.multiple_of(c * tc, tc)
        xpad_ref[pl.ds(r + pad, tc), :] = zx_ref[pl.ds(r, tc), :].astype(F32)
        return carry

    lax.fori_loop(0, n_chunks, copy_chunk, 0)

    def conv_chunk(c, carry):
        r = pl.multiple_of(c * tc, tc)
        xe = xpad_ref[pl.ds(r, tc + 2 * pad), :]
        n = tc + 2 * pad
        y = cb_ref[...]
        for tap in range(CONV_W):
            shifted = pltpu.roll(xe, (CONV_PAD_L - tap) % n, 0) if tap != CONV_PAD_L else xe
            y = y + shifted[pad:pad + tc] * cw_ref[tap:tap + 1, :]
        for nb in range(n_lane_blocks):
            for m in range(steps):
                xc_ref[nb, natural_rows(r, m), :] = y[m * SUBLANES:(m + 1) * SUBLANES, nb * LANES:(nb + 1) * LANES]
        return carry

    lax.fori_loop(0, n_chunks, conv_chunk, 0)

    sublane = lax.broadcasted_iota(jnp.int32, (SUBLANES, LANES), 0)

    half_k = [[(-0.5 * LRU_C * LOG2_E) * jax.nn.softplus(-lam_ref[direction:direction + 1, nb * LANES:(nb + 1) * LANES])
               for nb in range(n_lane_blocks)] for direction in range(2)]

    half_ba = [[0.5 * ba_ref[direction:direction + 1, nb * LANES:(nb + 1) * LANES]
                for nb in range(n_lane_blocks)] for direction in range(2)]
    half_bx = [[0.5 * bx_ref[direction:direction + 1, nb * LANES:(nb + 1) * LANES]
                for nb in range(n_lane_blocks)] for direction in range(2)]

    def gates(xn, direction, nb):
        xb = xn.astype(BF16)
        tanh_r = jnp.tanh(jnp.dot(xb, wa_ref[direction, nb], preferred_element_type=F32) + half_ba[direction][nb])
        tanh_i = jnp.tanh(jnp.dot(xb, wx_ref[direction, nb], preferred_element_type=F32) + half_bx[direction][nb])
        log2_a = half_k[direction][nb] * tanh_r + half_k[direction][nb]
        a = jnp.exp2(log2_a)
        m = (-2.0 / LOG2_E) * log2_a
        u = jnp.where(m < EXPM1_LINEAR_RANGE, m, jnp.maximum(1.0 - a * a, 0.0))
        scale = jnp.exp2((0.5 * LOG2_E) * jnp.log(u))
        b = scale * ((0.5 * tanh_i + 0.5) * xn)
        return a, b

    def chunk_scan(a, b, carry, reverse):
        order = range(steps - 1, -1, -1) if reverse else range(steps)
        local, decay = [None] * steps, [None] * steps
        h, p = None, None
        for j in order:
            aj, bj = a[j * SUBLANES:(j + 1) * SUBLANES], b[j * SUBLANES:(j + 1) * SUBLANES]
            h = bj if h is None else aj * h + bj
            p = aj if p is None else aj * p
            local[j], decay[j] = h, p
        seg_h, seg_p = h, p
        for dist in (1, 2, 4):
            if reverse:
                edge, shift = sublane >= SUBLANES - dist, SUBLANES - dist
            else:
                edge, shift = sublane < dist, dist
            h_prev = jnp.where(edge, 0.0, pltpu.roll(seg_h, shift, 0))
            p_prev = jnp.where(edge, 1.0, pltpu.roll(seg_p, shift, 0))
            seg_h = seg_p * h_prev + seg_h
            seg_p = seg_p * p_prev
        seg_state = seg_h + seg_p * carry
        if reverse:
            entering = jnp.where(sublane == SUBLANES - 1, carry, pltpu.roll(seg_state, SUBLANES - 1, 0))
            leaving = jnp.broadcast_to(seg_state[0:1], (SUBLANES, LANES))
        else:
            entering = jnp.where(sublane == 0, carry, pltpu.roll(seg_state, 1, 0))
            leaving = jnp.broadcast_to(seg_state[SUBLANES - 1:SUBLANES], (SUBLANES, LANES))
        return [local[j] + decay[j] * entering for j in range(steps)], leaving

    def forward_chunk(c, carry):
        r = pl.multiple_of(c * tc, tc)
        new_carry = []
        for nb in range(n_lane_blocks):
            lanes = slice(nb * LANES, (nb + 1) * LANES)
            a, b = gates(xc_ref[nb, pl.ds(r, tc), :], 0, nb)
            states, leaving = chunk_scan(a, b, carry[nb], reverse=False)
            for j in range(steps):
                hs_ref[nb, pl.ds(r + j * SUBLANES, SUBLANES), :] = states[j]
            new_carry.append(leaving)
        return tuple(new_carry)

    h0f = tuple(jnp.broadcast_to(h0_ref[0:1, nb * LANES:(nb + 1) * LANES], (SUBLANES, LANES))
                for nb in range(n_lane_blocks))
    scan_unroll = min(SCAN_UNROLL, n_chunks)
    hf_last = lax.fori_loop(0, n_chunks, forward_chunk, h0f, unroll=scan_unroll)

    def backward_chunk(c, carry):
        r = pl.multiple_of((n_chunks - 1 - c) * tc, tc)
        new_carry = []
        for nb in range(n_lane_blocks):
            lanes = slice(nb * LANES, (nb + 1) * LANES)
            a, b = gates(xc_ref[nb, pl.ds(r, tc), :], 1, nb)
            states, leaving = chunk_scan(a, b, carry[nb], reverse=True)
            for j in range(steps):
                hs_ref[nb, pl.ds(r + j * SUBLANES, SUBLANES), :] += states[j]
            new_carry.append(leaving)
        return tuple(new_carry)

    h0b = tuple(jnp.broadcast_to(h0_ref[1:2, nb * LANES:(nb + 1) * LANES], (SUBLANES, LANES))
                for nb in range(n_lane_blocks))
    hb_first = lax.fori_loop(0, n_chunks, backward_chunk, h0b, unroll=scan_unroll)

    for nb in range(n_lane_blocks):
        lanes = slice(nb * LANES, (nb + 1) * LANES)
        hfin_ref[0:1, lanes] = hf_last[nb][0:1]
        hfin_ref[1:2, lanes] = hb_first[nb][0:1]

    def gate_chunk(c, carry):
        r = pl.multiple_of(c * tc, tc)
        for nb in range(n_lane_blocks):
            lanes = slice(nb * LANES, (nb + 1) * LANES)
            for m in range(0, steps, 2):
                rows = pl.ds(r + m * SUBLANES, BF16_ROWS)
                hsum = jnp.concatenate([hs_ref[nb, natural_rows(r, m), :], hs_ref[nb, natural_rows(r, m + 1), :]],
                                       axis=0)
                o_ref[rows, lanes] = (hsum * jax.nn.gelu(zg_ref[rows, lanes].astype(F32))).astype(o_ref.dtype)
        return carry

    lax.fori_loop(0, n_chunks, gate_chunk, 0)


def _rglru(z, conv_w, conv_b, wa, wx, ba, bx, lam, h0, *, n_seq, t, width, x_col, g_col, name):
    cw = _row_tile(width, 1024)
    tc = _row_tile(t, 128)
    lb = cw // LANES
    assert wa.shape[-1] == LANES and x_col % cw == 0 and g_col % cw == 0
    vec_spec = pl.BlockSpec((2, cw), lambda s, c: (0, c))
    w_spec = pl.BlockSpec((2, lb, LANES, LANES), lambda s, c: (0, c, 0, 0))
    return pl.pallas_call(
        functools.partial(_lru_kernel, t=t, tc=tc),
        grid=(n_seq, width // cw),
        in_specs=[
            pl.BlockSpec((t, cw), lambda s, c: (s, x_col // cw + c)),
            pl.BlockSpec((t, cw), lambda s, c: (s, g_col // cw + c)),
            pl.BlockSpec((CONV_W, cw), lambda s, c: (0, c)),
            pl.BlockSpec((1, cw), lambda s, c: (0, c)),
            w_spec, w_spec, vec_spec, vec_spec, vec_spec,
            pl.BlockSpec((None, 2, cw), lambda s, c: (s, 0, c)),
        ],
        out_specs=[
            pl.BlockSpec((t, cw), lambda s, c: (s, c)),
            pl.BlockSpec((None, 2, cw), lambda s, c: (s, 0, c)),
        ],
        out_shape=[
            jax.ShapeDtypeStruct((n_seq * t, width), BF16),
            jax.ShapeDtypeStruct((n_seq, 2, width), F32),
        ],
        scratch_shapes=[
            pltpu.VMEM((t + 2 * SUBLANES, cw), F32),
            pltpu.VMEM((lb, t, LANES), F32),
            pltpu.VMEM((lb, t, LANES), F32),
        ],
        compiler_params=_params("parallel", "parallel"),
        name=name,
    )(z, z, conv_w, conv_b.reshape(1, width), wa, wx, ba, bx, lam, h0)


def _cmix_kernel(zu_ref, zv_ref, g_ref, ws_ref, bs_ref, o_ref, v_ref, *, n_groups):
    rows, width = zu_ref.shape
    gw = width // n_groups
    def norm_chunk(c, carry):
        r = pl.ds(pl.multiple_of(c * ROW_CHUNK, ROW_CHUNK), ROW_CHUNK)
        v_ref[r, :] = _rms(jax.nn.gelu(zv_ref[r, :].astype(F32)), g_ref[...]).astype(BF16)
        return carry

    lax.fori_loop(0, rows // ROW_CHUNK, norm_chunk, 0, unroll=NORM_UNROLL)
    for ch in range(rows // CHUNK):
        r = slice(ch * CHUNK, (ch + 1) * CHUNK)
        for g in range(n_groups):
            cols = slice(g * gw, (g + 1) * gw)
            mixed = jnp.dot(ws_ref[g], v_ref[r, cols], preferred_element_type=F32) + bs_ref[g]
            o_ref[r, cols] = (jax.nn.gelu(zu_ref[r, cols].astype(F32)) * mixed).astype(o_ref.dtype)


def _chunk_mix(z, cm_g, ws, bs, *, width, u_col, v_col, name):
    m = z.shape[0]
    n_groups = ws.shape[0]
    tr = _row_tile(m, 2 * CHUNK)
    assert u_col % width == 0 and v_col % width == 0
    return pl.pallas_call(
        functools.partial(_cmix_kernel, n_groups=n_groups),
        grid=(m // tr,),
        in_specs=[
            pl.BlockSpec((tr, width), lambda i: (i, u_col // width)),
            pl.BlockSpec((tr, width), lambda i: (i, v_col // width)),
            pl.BlockSpec((1, width), lambda i: (0, 0)),
            pl.BlockSpec(ws.shape, lambda i: (0, 0, 0)),
            pl.BlockSpec(bs.shape, lambda i: (0, 0, 0)),
        ],
        out_specs=pl.BlockSpec((tr, width), lambda i: (i, 0)),
        out_shape=jax.ShapeDtypeStruct((m, width), BF16),
        scratch_shapes=[pltpu.VMEM((tr, width), BF16)],
        compiler_params=_params("parallel"),
        name=name,
    )(z, z, cm_g.reshape(1, width), ws, bs)


def _rope_tables(n_tokens):
    rows = n_tokens // GRID_W
    pos_row = jnp.repeat(jnp.arange(rows), GRID_W).astype(F32)
    pos_col = (jnp.arange(n_tokens) % GRID_W).astype(F32)
    inv = ROPE_BASE ** (-jnp.arange(ROPE_FREQS, dtype=F32) / ROPE_FREQS)
    cr, sr = jnp.cos(pos_row[:, None] * inv), jnp.sin(pos_row[:, None] * inv)
    cc, sc = jnp.cos(pos_col[:, None] * inv), jnp.sin(pos_col[:, None] * inv)
    return (jnp.concatenate([cr, cr, cc, cc], axis=-1), jnp.concatenate([-sr, sr, -sc, sc], axis=-1))


def _layer(x, mod, p, tag, *, n_seq, t, rows_per_cond, rope=None, past_k=None, past_v=None, layer=None, h0=None):
    d = x.shape[1]
    kv_w = p["kv_w"]
    col_lx, col_lg, col_cu, col_cv, col_g, col_k = d, 2 * d, 3 * d, 4 * d, 5 * d, 8 * d
    col_v = col_k + kv_w

    tn = 2 * kv_w
    assert d % tn == 0
    q_blocks, rest_blocks = d // tn, 7 * d // tn

    def w_col_block(j):
        return jnp.where(j < q_blocks, j, jnp.where(j < q_blocks + rest_blocks, j + 1, q_blocks))

    li = p["layer"]
    z = _prenorm_matmul(x, p["g_pre_mix"], mod, p["w_in"], li, rows_per_cond=rows_per_cond,
                        shift_row=MOD_SHIFT1, scale_row=MOD_SCALE1, relu2=False, name=f"in_proj_{tag}",
                        tn=tn, w_col_block=w_col_block)
    attn = _attention(z, p["g_q"], p["g_k"], n_seq=n_seq, t=t, d=d, kv_w=kv_w, k_col=col_k, v_col=col_v,
                      rope=rope, past_k=past_k, past_v=past_v, layer=layer, name=f"attention_{tag}")
    if rope is None:
        attn_o, k_new, v_new = attn
    else:
        attn_o, k_new, v_new = attn, None, None
    lru_o, h_fin = _rglru(z, p["conv_w"], p["conv_b"], p["lru_wa"], p["lru_wx"], p["lru_ba"], p["lru_bx"],
                          p["lru_lam"], h0, n_seq=n_seq, t=t, width=d, x_col=col_lx, g_col=col_lg,
                          name=f"rglru_{tag}")
    cm_o = _chunk_mix(z, p["cm_g"], p["cm_ws"], p["cm_bs"], width=d, u_col=col_cu, v_col=col_cv,
                      name=f"chunk_mix_{tag}")
    merged = _merge((attn_o, lru_o, cm_o), (p["w_attn_out"], p["w_lru_out"], p["w_cm_out"]), li, z, col_g,
                    name=f"merge_{tag}")
    x = _matmul_norm_residual(merged, p["w_out"], li, p["g_post_mix"], mod, x, rows_per_cond=rows_per_cond,
                              gate_row=MOD_GATE1, name=f"out_proj_{tag}")
    f = _prenorm_matmul(x, p["g_pre_ff"], mod, p["w_ff1"], li, rows_per_cond=rows_per_cond,
                        shift_row=MOD_SHIFT2, scale_row=MOD_SCALE2, relu2=True, name=f"ff1_{tag}")
    x = _matmul_norm_residual(f, p["w_ff2"], li, p["g_post_ff"], mod, x, rows_per_cond=rows_per_cond,
                              gate_row=MOD_GATE2, name=f"ff2_{tag}")
    return x, k_new, v_new, h_fin


def kernel(x_prompt, x_sample, cache_k, cache_v, state_lru, c, c_ctx, w_mod, b_mod, g_pre_mix, g_post_mix, g_pre_ff, g_post_ff, w_in, g_q, g_k, w_attn_out, conv_w, conv_b, lru_wa, lru_ba, lru_wx, lru_bx, lru_lam, w_lru_out, cm_g, cm_ws, cm_bs, w_cm_out, w_out, w_ff1, w_ff2):
    batch, seq, d = x_prompt.shape
    dec_batch, dec_seq, _ = x_sample.shape
    depth = w_in.shape[0]
    in_w = w_in.shape[2]
    kv_w = (in_w - 8 * d) // 2
    n_kv = kv_w // HEAD_DIM
    past = cache_k.shape[2]
    n_groups = cm_ws.shape[1]

    n_cond = 1 + dec_batch
    cond = jnp.concatenate([c_ctx[None, :], c], axis=0)
    cond = jnp.pad(cond, ((0, -n_cond % SUBLANES), (0, 0)))
    mod = _modulation(cond, w_mod, b_mod)

    w_in_b = w_in.astype(BF16)
    w_attn_out_b, w_lru_out_b, w_cm_out_b = w_attn_out.astype(BF16), w_lru_out.astype(BF16), w_cm_out.astype(BF16)
    w_out_b, w_ff1_b, w_ff2_b = w_out.astype(BF16), w_ff1.astype(BF16), w_ff2.astype(BF16)
    lru_wa_b, lru_wx_b, cm_ws_b = (0.5 * lru_wa).astype(BF16), (0.5 * lru_wx).astype(BF16), cm_ws.astype(BF16)
    cm_bs_b = jnp.broadcast_to(cm_bs[:, :, :, None], cm_bs.shape + (d // n_groups,))

    rope = _rope_tables(dec_seq)
    past_k = cache_k.reshape(dec_batch, depth, past, kv_w)
    past_v = cache_v.reshape(dec_batch, depth, past, kv_w)
    h0_prompt = jnp.zeros((batch, 2, d), F32)

    y_p = x_prompt.reshape(batch * seq, d)
    y_s = x_sample.reshape(dec_batch * dec_seq, d)
    new_k, new_v, new_s = [], [], []
    for l in range(depth):
        p = {
            "kv_w": kv_w, "layer": l, "g_pre_mix": g_pre_mix[l], "g_post_mix": g_post_mix[l],
            "g_pre_ff": g_pre_ff[l], "g_post_ff": g_post_ff[l], "w_in": w_in_b, "g_q": g_q[l], "g_k": g_k[l],
            "w_attn_out": w_attn_out_b, "conv_w": conv_w[l], "conv_b": conv_b[l],
            "lru_wa": lru_wa_b[l], "lru_ba": lru_ba[l], "lru_wx": lru_wx_b[l], "lru_bx": lru_bx[l],
            "lru_lam": lru_lam[l], "w_lru_out": w_lru_out_b, "cm_g": cm_g[l], "cm_ws": cm_ws_b[l],
            "cm_bs": cm_bs_b[l], "w_cm_out": w_cm_out_b, "w_out": w_out_b, "w_ff1": w_ff1_b,
            "w_ff2": w_ff2_b,
        }
        y_p, k_l, v_l, s_l = _layer(y_p, mod[l, 0:1], p, f"ctx{l}", n_seq=batch, t=seq,
                                    rows_per_cond=batch * seq, h0=h0_prompt)
        new_k.append(k_l.reshape(batch, seq, n_kv, HEAD_DIM))
        new_v.append(v_l.reshape(batch, seq, n_kv, HEAD_DIM))
        new_s.append(s_l)
        y_s, _, _, _ = _layer(y_s, mod[l, 1:1 + dec_batch], p, f"lat{l}", n_seq=dec_batch, t=dec_seq,
                              rows_per_cond=dec_seq, rope=rope, past_k=past_k, past_v=past_v, layer=l,
                              h0=state_lru[:, l])
    return (y_p.reshape(batch, seq, d), y_s.reshape(dec_batch, dec_seq, d),
            jnp.stack(new_k, axis=1), jnp.stack(new_v, axis=1), jnp.stack(new_s, axis=1))
```

```python
import functools

import jax
import jax.numpy as jnp
from jax import lax
from jax.experimental import pallas as pl
from jax.experimental.pallas import tpu as pltpu

F32 = jnp.float32
BF16 = jnp.bfloat16

EPS = 1e-6
HEAD_DIM = 128
GRID_W = 64
ROPE_BASE = 10000.0
ROPE_FREQS = HEAD_DIM // 4
CONV_W = 4
CONV_PAD_L = 2
LRU_C = 8.0
CHUNK = 128
N_MOD = 6
N_BRANCH = 3
LOG2_E = 1.4426950408889634

LANES = 128
SUBLANES = 8
VMEM_LIMIT_BYTES = 56 * 1024 * 1024
BF16_ROWS = 2 * SUBLANES
ROW_CHUNK = BF16_ROWS
SCORE_BUFFERS = 3
PROB_BUFFERS = 2
SCAN_UNROLL = 4
NORM_UNROLL = 8
VREG_BUDGET_ELEMS = 32 * SUBLANES * LANES

MOD_SHIFT1, MOD_SCALE1, MOD_GATE1, MOD_SHIFT2, MOD_SCALE2, MOD_GATE2 = range(N_MOD)


def _params(*semantics):
    return pltpu.CompilerParams(dimension_semantics=semantics, vmem_limit_bytes=VMEM_LIMIT_BYTES)


def _rms(x, g):
    return x * lax.rsqrt(jnp.mean(x * x, axis=-1, keepdims=True) + EPS) * g


def _sigmoid(x):
    return 0.5 * jnp.tanh(0.5 * x) + 0.5


def _row_tile(m, cap):
    if m <= cap:
        return m
    t = cap - cap % LANES
    while m % t:
        t -= LANES
    assert t > 0
    return t


def _mod_kernel(c_ref, w_ref, b_ref, o_ref):
    c = c_ref[...]
    s = (c * jax.nn.sigmoid(c)).astype(BF16)
    o_ref[...] = jnp.dot(s, w_ref[...].astype(BF16), preferred_element_type=F32) + b_ref[...]


def _modulation(cond, w_mod, b_mod):
    nc, d = cond.shape
    depth, _, n = w_mod.shape
    tn = _row_tile(n, 1024)
    out = pl.pallas_call(
        _mod_kernel,
        grid=(depth, n // tn),
        in_specs=[
            pl.BlockSpec((nc, d), lambda l, j: (0, 0)),
            pl.BlockSpec((None, d, tn), lambda l, j: (l, 0, j)),
            pl.BlockSpec((None, 1, tn), lambda l, j: (l, 0, j)),
        ],
        out_specs=pl.BlockSpec((None, nc, tn), lambda l, j: (l, 0, j)),
        out_shape=jax.ShapeDtypeStruct((depth, nc, n), F32),
        compiler_params=_params("parallel", "parallel"),
        name="modulation",
    )(cond, w_mod, b_mod.reshape(depth, 1, n))
    return out.reshape(depth, nc, N_MOD, d)


def _prenorm_mm_kernel(x_ref, g_ref, mod_ref, w_ref, o_ref, h_ref, *, shift_row, scale_row, relu2,
                       rows_per_cond, n_tiles, n_slices):
    i, j = pl.program_id(0), pl.program_id(1)
    tm = x_ref.shape[0]

    def gain_shift(tile):
        cond = (tile * tm) // rows_per_cond
        return (g_ref[...] * (1.0 + mod_ref[cond, scale_row:scale_row + 1, :]),
                mod_ref[cond, shift_row:shift_row + 1, :])

    def normalise(rows, slot, gain, shift):
        x = x_ref[rows, :]
        inv = lax.rsqrt(jnp.mean(x * x, axis=-1, keepdims=True) + EPS)
        h_ref[slot, rows, :] = (x * inv * gain + shift).astype(BF16)

    @pl.when(jnp.logical_and(i == 0, j == 0))
    def _():
        gain, shift = gain_shift(0)

        def chunk(c, carry):
            normalise(pl.ds(pl.multiple_of(c * ROW_CHUNK, ROW_CHUNK), ROW_CHUNK), 0, gain, shift)
            return carry

        lax.fori_loop(0, tm // ROW_CHUNK, chunk, 0, unroll=NORM_UNROLL)

    slice_rows = tm // n_slices
    gain, shift = gain_shift(jnp.minimum(i + 1, n_tiles - 1))

    def normalise_slice(s):
        row0 = pl.multiple_of(s * slice_rows, slice_rows)
        for c in range(slice_rows // ROW_CHUNK):
            normalise(pl.ds(row0 + c * ROW_CHUNK, ROW_CHUNK), (i + 1) % 2, gain, shift)

    @pl.when(jnp.logical_and(i == 0, j == 1))
    def _():
        normalise_slice(0)

    normalise_slice(jnp.minimum(j, n_slices - 1))

    acc = jnp.dot(h_ref[i % 2], w_ref[...], preferred_element_type=F32)
    if relu2:
        acc = jnp.square(jnp.maximum(acc, 0.0))
    o_ref[...] = acc.astype(o_ref.dtype)


def _prenorm_matmul(x, g, mod, w, layer, *, rows_per_cond, shift_row, scale_row, relu2, name, tn=None,
                    w_col_block=lambda j: j):
    m, d = x.shape
    n = w.shape[2]
    tm = _row_tile(rows_per_cond, 1024)
    tn = tn or _row_tile(n, 1024)
    n_tiles, nj = m // tm, n // tn
    assert rows_per_cond % tm == 0 and n % tn == 0 and nj >= 2
    n_slices = max(s for s in range(1, nj + 1) if (tm // ROW_CHUNK) % s == 0)

    def x_tile(i, j):
        return jnp.where(jnp.logical_and(i == 0, j == 0), 0, jnp.minimum(i + 1, n_tiles - 1))

    return pl.pallas_call(
        functools.partial(_prenorm_mm_kernel, shift_row=shift_row, scale_row=scale_row, relu2=relu2,
                          rows_per_cond=rows_per_cond, n_tiles=n_tiles, n_slices=n_slices),
        grid=(n_tiles, nj),
        in_specs=[
            pl.BlockSpec((tm, d), lambda i, j: (x_tile(i, j), 0)),
            pl.BlockSpec((1, d), lambda i, j: (0, 0)),
            pl.BlockSpec(mod.shape, lambda i, j: (0, 0, 0)),
            pl.BlockSpec((None, d, tn), lambda i, j: (layer, 0, w_col_block(j))),
        ],
        out_specs=pl.BlockSpec((tm, tn), lambda i, j: (i, j)),
        out_shape=jax.ShapeDtypeStruct((m, n), BF16),
        scratch_shapes=[pltpu.VMEM((2, tm, d), BF16)],
        compiler_params=_params("arbitrary", "arbitrary"),
        name=name,
    )(x, g.reshape(1, d), mod, w)


def _mm_norm_res_kernel(a_ref, w_ref, g_ref, mod_ref, x_ref, o_ref, acc_ref, *, gate_row, nk):
    def product():
        return jnp.dot(a_ref[...], w_ref[...], preferred_element_type=F32)

    def finish():
        gain = mod_ref[gate_row:gate_row + 1, :] * g_ref[...]

        def chunk(c, carry):
            rows = pl.ds(pl.multiple_of(c * ROW_CHUNK, ROW_CHUNK), ROW_CHUNK)
            y = acc_ref[rows, :]
            inv = lax.rsqrt(jnp.mean(y * y, axis=-1, keepdims=True) + EPS)
            o_ref[rows, :] = x_ref[rows, :] + y * inv * gain
            return carry

        lax.fori_loop(0, o_ref.shape[0] // ROW_CHUNK, chunk, 0, unroll=NORM_UNROLL)

    if nk == 1:
        acc_ref[...] = product()
        finish()
        return
    k = pl.program_id(1)

    @pl.when(k == 0)
    def _():
        acc_ref[...] = product()

    @pl.when(k > 0)
    def _():
        acc_ref[...] += product()

    @pl.when(k == nk - 1)
    def _():
        finish()


def _matmul_norm_residual(a, w, layer, g, mod, x, *, rows_per_cond, gate_row, name):
    m, kdim = a.shape
    d = w.shape[2]
    tk = _row_tile(kdim, 2048)
    nk = kdim // tk
    tm = _row_tile(rows_per_cond, 512)
    assert rows_per_cond % tm == 0
    return pl.pallas_call(
        functools.partial(_mm_norm_res_kernel, gate_row=gate_row, nk=nk),
        grid=(m // tm, nk),
        in_specs=[
            pl.BlockSpec((tm, tk), lambda i, k: (i, k)),
            pl.BlockSpec((None, tk, d), lambda i, k: (layer, k, 0)),
            pl.BlockSpec((1, d), lambda i, k: (0, 0)),
            pl.BlockSpec((None, N_MOD, d), lambda i, k: ((i * tm) // rows_per_cond, 0, 0)),
            pl.BlockSpec((tm, d), lambda i, k: (i, 0)),
        ],
        out_specs=pl.BlockSpec((tm, d), lambda i, k: (i, 0)),
        out_shape=jax.ShapeDtypeStruct((m, d), F32),
        scratch_shapes=[pltpu.VMEM((tm, d), F32)],
        compiler_params=_params("parallel", "arbitrary"),
        name=name,
    )(a, w, g.reshape(1, d), mod, x)


def _merge_kernel(a0_ref, a1_ref, a2_ref, w0_ref, w1_ref, w2_ref, g0_ref, g1_ref, g2_ref, o_ref):
    acc = _sigmoid(g0_ref[...].astype(F32)) * jnp.dot(a0_ref[...], w0_ref[...], preferred_element_type=F32)
    acc += _sigmoid(g1_ref[...].astype(F32)) * jnp.dot(a1_ref[...], w1_ref[...], preferred_element_type=F32)
    acc += _sigmoid(g2_ref[...].astype(F32)) * jnp.dot(a2_ref[...], w2_ref[...], preferred_element_type=F32)
    o_ref[...] = acc.astype(o_ref.dtype)


def _merge(branches, weights, layer, z, gate_col0, name):
    m, d = branches[0].shape
    tm = _row_tile(m, 1024)
    tn = _row_tile(d, 512)
    assert gate_col0 % tn == 0 and d % tn == 0
    a_spec = pl.BlockSpec((tm, d), lambda i, j: (i, 0))
    w_spec = pl.BlockSpec((None, d, tn), lambda i, j: (layer, 0, j))

    def gate_spec(b):
        off = (gate_col0 + b * d) // tn
        return pl.BlockSpec((tm, tn), lambda i, j: (i, off + j))

    return pl.pallas_call(
        _merge_kernel,
        grid=(m // tm, d // tn),
        in_specs=[a_spec] * N_BRANCH + [w_spec] * N_BRANCH + [gate_spec(b) for b in range(N_BRANCH)],
        out_specs=pl.BlockSpec((tm, tn), lambda i, j: (i, j)),
        out_shape=jax.ShapeDtypeStruct((m, d), BF16),
        compiler_params=_params("parallel", "arbitrary"),
        name=name,
    )(*branches, *weights, z, z, z)


def _rope(x, c, s):
    lane = lax.broadcasted_iota(jnp.int32, x.shape, 1)
    partner = jnp.where((lane % (2 * ROPE_FREQS)) < ROPE_FREQS,
                        pltpu.roll(x, HEAD_DIM - ROPE_FREQS, 1),
                        pltpu.roll(x, ROPE_FREQS, 1))
    return x * c + partner * s


def _attn_kernel(*refs, has_ctx, t, tq, q_group, unit_rows, softmax_rows):
    if has_ctx:
        (zq_ref, zk_ref, zv_ref, gq_ref, gk_ref, cq_ref, sq_ref, ck_ref, sk_ref, pk_ref, pv_ref,
         o_ref, kt_ref, v_ref, *buffers) = refs
    else:
        zq_ref, zk_ref, zv_ref, gq_ref, gk_ref, o_ref, ko_ref, vo_ref, kt_ref, v_ref, *buffers = refs
    s_refs, p_refs = buffers[:SCORE_BUFFERS], buffers[SCORE_BUFFERS:]

    @pl.when(pl.program_id(2) == 0)
    def _():
        k = _rms(zk_ref[...].astype(F32), gk_ref[...])
        v_ref[:, HEAD_DIM:] = jnp.ones((v_ref.shape[0], HEAD_DIM), BF16)
        if has_ctx:
            k = _rope(k, ck_ref[...], sk_ref[...])
            kt_ref[:, 0:t] = k.T.astype(BF16)
            kt_ref[:, t:] = pk_ref[...].T.astype(BF16)
            v_ref[0:t, 0:HEAD_DIM] = zv_ref[...]
            v_ref[t:, 0:HEAD_DIM] = pv_ref[...].astype(BF16)
        else:
            ko_ref[...] = k
            vo_ref[...] = zv_ref[...].astype(F32)
            kt_ref[...] = k.T.astype(BF16)
            v_ref[:, 0:HEAD_DIM] = zv_ref[...]

    q_scale = LOG2_E * HEAD_DIM ** -0.5

    units = [(g, r0) for r0 in range(0, tq, unit_rows) for g in range(q_group)]

    def scores(u):
        g, r0 = units[u]
        rows = slice(r0, r0 + unit_rows)
        q = _rms(zq_ref[rows, g * HEAD_DIM:(g + 1) * HEAD_DIM].astype(F32), gq_ref[...])
        if has_ctx:
            q = _rope(q, cq_ref[rows, :], sq_ref[rows, :])
        s_refs[u % SCORE_BUFFERS][...] = jnp.dot((q * q_scale).astype(BF16), kt_ref[...],
                                                 preferred_element_type=F32)

    def exponentials(u):
        s_ref, p_ref = s_refs[u % SCORE_BUFFERS], p_refs[u % PROB_BUFFERS]
        groups = [slice(r0, r0 + softmax_rows) for r0 in range(0, unit_rows, softmax_rows)]
        maxima = [jnp.max(s_ref[rows, :], axis=-1, keepdims=True) for rows in groups]
        for rows, m in zip(groups, maxima):
            p_ref[rows, :] = jnp.exp2(s_ref[rows, :] - m).astype(BF16)

    def weighted_values(u):
        g, r0 = units[u]
        o = jnp.dot(p_refs[u % PROB_BUFFERS][...], v_ref[...], preferred_element_type=F32)
        o_ref[r0:r0 + unit_rows, g * HEAD_DIM:(g + 1) * HEAD_DIM] = (
            o[:, :HEAD_DIM] * (1.0 / o[:, HEAD_DIM:])).astype(o_ref.dtype)

    lookahead = SCORE_BUFFERS - 1
    for u in range(min(lookahead, len(units))):
        scores(u)
    for u in range(len(units)):
        if u + lookahead < len(units):
            scores(u + lookahead)
        exponentials(u)
        weighted_values(u)


def _attention(z, g_q, g_k, *, n_seq, t, d, kv_w, k_col, v_col, rope=None, past_k=None, past_v=None, layer=None, name):
    has_ctx = rope is not None
    n_kv = kv_w // HEAD_DIM
    q_group = d // kv_w
    qw = q_group * HEAD_DIM
    past = past_k.shape[2] if has_ctx else 0
    tk = t + past
    tq = _row_tile(t, 512)
    nq = t // tq
    unit_rows = _row_tile(tq, 128)
    softmax_rows = min(unit_rows, max(BF16_ROWS, (VREG_BUDGET_ELEMS // tk) // BF16_ROWS * BF16_ROWS))
    assert unit_rows % softmax_rows == 0

    in_specs = [
        pl.BlockSpec((tq, qw), lambda b, h, i: (b * nq + i, h)),
        pl.BlockSpec((t, HEAD_DIM), lambda b, h, i: (b, k_col // HEAD_DIM + h)),
        pl.BlockSpec((t, HEAD_DIM), lambda b, h, i: (b, v_col // HEAD_DIM + h)),
        pl.BlockSpec((1, HEAD_DIM), lambda b, h, i: (0, 0)),
        pl.BlockSpec((1, HEAD_DIM), lambda b, h, i: (0, 0)),
    ]
    args = [z, z, z, g_q.reshape(1, HEAD_DIM), g_k.reshape(1, HEAD_DIM)]
    o_spec = pl.BlockSpec((tq, qw), lambda b, h, i: (b * nq + i, h))
    o_shape = jax.ShapeDtypeStruct((n_seq * t, d), BF16)
    if has_ctx:
        cos, sin = rope
        in_specs += [
            pl.BlockSpec((tq, HEAD_DIM), lambda b, h, i: (i, 0)),
            pl.BlockSpec((tq, HEAD_DIM), lambda b, h, i: (i, 0)),
            pl.BlockSpec((t, HEAD_DIM), lambda b, h, i: (0, 0)),
            pl.BlockSpec((t, HEAD_DIM), lambda b, h, i: (0, 0)),
            pl.BlockSpec((None, None, past, HEAD_DIM), lambda b, h, i: (b, layer, 0, h)),
            pl.BlockSpec((None, None, past, HEAD_DIM), lambda b, h, i: (b, layer, 0, h)),
        ]
        args += [cos, sin, cos, sin, past_k, past_v]
        out_specs = o_spec
        out_shape = o_shape
    else:
        kv_spec = pl.BlockSpec((None, t, HEAD_DIM), lambda b, h, i: (b, 0, h))
        kv_shape = jax.ShapeDtypeStruct((n_seq, t, kv_w), F32)
        out_specs = [o_spec, kv_spec, kv_spec]
        out_shape = [o_shape, kv_shape, kv_shape]

    return pl.pallas_call(
        functools.partial(_attn_kernel, has_ctx=has_ctx, t=t, tq=tq, q_group=q_group, unit_rows=unit_rows,
                          softmax_rows=softmax_rows),
        grid=(n_seq, n_kv, nq),
        in_specs=in_specs,
        out_specs=out_specs,
        out_shape=out_shape,
        scratch_shapes=[
            pltpu.VMEM((HEAD_DIM, tk), BF16),
            pltpu.VMEM((tk, 2 * HEAD_DIM), BF16),
            *[pltpu.VMEM((unit_rows, tk), F32)] * SCORE_BUFFERS,
            *[pltpu.VMEM((unit_rows, tk), BF16)] * PROB_BUFFERS,
        ],
        compiler_params=_params("parallel", "parallel", "arbitrary"),
        name=name,
    )(*args)


EXPM1_SERIES_RANGE = 2.0 ** -11


def _lru_kernel(zx_ref, zg_ref, cw_ref, cb_ref, wa_ref, wx_ref, ba_ref, bx_ref, lam_ref, h0_ref,
                o_ref, hfin_ref, xpad_ref, xc_ref, hs_ref, *, t, tc):
    cw = zx_ref.shape[1]
    n_lane_blocks = cw // LANES
    n_chunks = t // tc
    pad = SUBLANES
    steps = tc // SUBLANES
    assert steps % SUBLANES == 0

    def natural_rows(r, m):
        seg, j0 = divmod(m * SUBLANES, steps)
        return pl.ds(r + SUBLANES * j0 + seg, SUBLANES, stride=SUBLANES)

    xpad_ref[0:pad, :] = jnp.zeros((pad, cw), F32)
    xpad_ref[t + pad:t + 2 * pad, :] = jnp.zeros((pad, cw), F32)

    def copy_chunk(c, carry):
        r = pl.multiple_of(c * tc, tc)
        xpad_ref[pl.ds(r + pad, tc), :] = zx_ref[pl.ds(r, tc), :].astype(F32)
        return carry

    lax.fori_loop(0, n_chunks, copy_chunk, 0)

    def conv_chunk(c, carry):
        r = pl.multiple_of(c * tc, tc)
        xe = xpad_ref[pl.ds(r, tc + 2 * pad), :]
        n = tc + 2 * pad
        y = cb_ref[...]
        for tap in range(CONV_W):
            shifted = pltpu.roll(xe, (CONV_PAD_L - tap) % n, 0) if tap != CONV_PAD_L else xe
            y = y + shifted[pad:pad + tc] * cw_ref[tap:tap + 1, :]
        for nb in range(n_lane_blocks):
            for m in range(steps):
                xc_ref[nb, natural_rows(r, m), :] = y[m * SUBLANES:(m + 1) * SUBLANES, nb * LANES:(nb + 1) * LANES]
        return carry

    lax.fori_loop(0, n_chunks, conv_chunk, 0)

    sublane = lax.broadcasted_iota(jnp.int32, (SUBLANES, LANES), 0)

    half_k = [[(-0.5 * LRU_C * LOG2_E) * jax.nn.softplus(-lam_ref[direction:direction + 1, nb * LANES:(nb + 1) * LANES])
               for nb in range(n_lane_blocks)] for direction in range(2)]

    half_ba = [[0.5 * ba_ref[direction:direction + 1, nb * LANES:(nb + 1) * LANES]
                for nb in range(n_lane_blocks)] for direction in range(2)]
    half_bx = [[0.5 * bx_ref[direction:direction + 1, nb * LANES:(nb + 1) * LANES]
                for nb in range(n_lane_blocks)] for direction in range(2)]

    def gates(xn, direction, nb):
        xb = xn.astype(BF16)
        tanh_r = jnp.tanh(jnp.dot(xb, wa_ref[direction, nb], preferred_element_type=F32) + half_ba[direction][nb])
        tanh_i = jnp.tanh(jnp.dot(xb, wx_ref[direction, nb], preferred_element_type=F32) + half_bx[direction][nb])
        log2_a = half_k[direction][nb] * tanh_r + half_k[direction][nb]
        a = jnp.exp2(log2_a)
        m = (-2.0 / LOG2_E) * log2_a
        u = jnp.where(m < EXPM1_SERIES_RANGE, m * (1.0 - 0.5 * m), jnp.maximum(1.0 - a * a, 0.0))
        scale = jnp.exp2((0.5 * LOG2_E) * jnp.log(u))
        b = scale * ((0.5 * tanh_i + 0.5) * xn)
        return a, b

    def chunk_scan(a, b, carry, reverse):
        order = range(steps - 1, -1, -1) if reverse else range(steps)
        local, decay = [None] * steps, [None] * steps
        h, p = None, None
        for j in order:
            aj, bj = a[j * SUBLANES:(j + 1) * SUBLANES], b[j * SUBLANES:(j + 1) * SUBLANES]
            h = bj if h is None else aj * h + bj
            p = aj if p is None else aj * p
            local[j], decay[j] = h, p
        seg_h, seg_p = h, p
        for dist in (1, 2, 4):
            if reverse:
                edge, shift = sublane >= SUBLANES - dist, SUBLANES - dist
            else:
                edge, shift = sublane < dist, dist
            h_prev = jnp.where(edge, 0.0, pltpu.roll(seg_h, shift, 0))
            p_prev = jnp.where(edge, 1.0, pltpu.roll(seg_p, shift, 0))
            seg_h = seg_p * h_prev + seg_h
            seg_p = seg_p * p_prev
        seg_state = seg_h + seg_p * carry
        if reverse:
            entering = jnp.where(sublane == SUBLANES - 1, carry, pltpu.roll(seg_state, SUBLANES - 1, 0))
            leaving = jnp.broadcast_to(seg_state[0:1], (SUBLANES, LANES))
        else:
            entering = jnp.where(sublane == 0, carry, pltpu.roll(seg_state, 1, 0))
            leaving = jnp.broadcast_to(seg_state[SUBLANES - 1:SUBLANES], (SUBLANES, LANES))
        return [local[j] + decay[j] * entering for j in range(steps)], leaving

    def forward_chunk(c, carry):
        r = pl.multiple_of(c * tc, tc)
        new_carry = []
        for nb in range(n_lane_blocks):
            lanes = slice(nb * LANES, (nb + 1) * LANES)
            a, b = gates(xc_ref[nb, pl.ds(r, tc), :], 0, nb)
            states, leaving = chunk_scan(a, b, carry[nb], reverse=False)
            for j in range(steps):
                hs_ref[nb, pl.ds(r + j * SUBLANES, SUBLANES), :] = states[j]
            new_carry.append(leaving)
        return tuple(new_carry)

    h0f = tuple(jnp.broadcast_to(h0_ref[0:1, nb * LANES:(nb + 1) * LANES], (SUBLANES, LANES))
                for nb in range(n_lane_blocks))
    scan_unroll = min(SCAN_UNROLL, n_chunks)
    hf_last = lax.fori_loop(0, n_chunks, forward_chunk, h0f, unroll=scan_unroll)

    def backward_chunk(c, carry):
        r = pl.multiple_of((n_chunks - 1 - c) * tc, tc)
        new_carry = []
        for nb in range(n_lane_blocks):
            lanes = slice(nb * LANES, (nb + 1) * LANES)
            a, b = gates(xc_ref[nb, pl.ds(r, tc), :], 1, nb)
            states, leaving = chunk_scan(a, b, carry[nb], reverse=True)
            for j in range(steps):
                hs_ref[nb, pl.ds(r + j * SUBLANES, SUBLANES), :] += states[j]
            new_carry.append(leaving)
        return tuple(new_carry)

    h0b = tuple(jnp.broadcast_to(h0_ref[1:2, nb * LANES:(nb + 1) * LANES], (SUBLANES, LANES))
                for nb in range(n_lane_blocks))
    hb_first = lax.fori_loop(0, n_chunks, backward_chunk, h0b, unroll=scan_unroll)

    for nb in range(n_lane_blocks):
        lanes = slice(nb * LANES, (nb + 1) * LANES)
        hfin_ref[0:1, lanes] = hf_last[nb][0:1]
        hfin_ref[1:2, lanes] = hb_first[nb][0:1]

    def gate_chunk(c, carry):
        r = pl.multiple_of(c * tc, tc)
        for nb in range(n_lane_blocks):
            lanes = slice(nb * LANES, (nb + 1) * LANES)
            for m in range(0, steps, 2):
                rows = pl.ds(r + m * SUBLANES, BF16_ROWS)
                hsum = jnp.concatenate([hs_ref[nb, natural_rows(r, m), :], hs_ref[nb, natural_rows(r, m + 1), :]],
                                       axis=0)
                o_ref[rows, lanes] = (hsum * jax.nn.gelu(zg_ref[rows, lanes].astype(F32))).astype(o_ref.dtype)
        return carry

    lax.fori_loop(0, n_chunks, gate_chunk, 0)


def _rglru(z, conv_w, conv_b, wa, wx, ba, bx, lam, h0, *, n_seq, t, width, x_col, g_col, name):
    cw = _row_tile(width, 1024)
    tc = _row_tile(t, 128)
    lb = cw // LANES
    assert wa.shape[-1] == LANES and x_col % cw == 0 and g_col % cw == 0
    vec_spec = pl.BlockSpec((2, cw), lambda s, c: (0, c))
    w_spec = pl.BlockSpec((2, lb, LANES, LANES), lambda s, c: (0, c, 0, 0))
    return pl.pallas_call(
        functools.partial(_lru_kernel, t=t, tc=tc),
        grid=(n_seq, width // cw),
        in_specs=[
            pl.BlockSpec((t, cw), lambda s, c: (s, x_col // cw + c)),
            pl.BlockSpec((t, cw), lambda s, c: (s, g_col // cw + c)),
            pl.BlockSpec((CONV_W, cw), lambda s, c: (0, c)),
            pl.BlockSpec((1, cw), lambda s, c: (0, c)),
            w_spec, w_spec, vec_spec, vec_spec, vec_spec,
            pl.BlockSpec((None, 2, cw), lambda s, c: (s, 0, c)),
        ],
        out_specs=[
            pl.BlockSpec((t, cw), lambda s, c: (s, c)),
            pl.BlockSpec((None, 2, cw), lambda s, c: (s, 0, c)),
        ],
        out_shape=[
            jax.ShapeDtypeStruct((n_seq * t, width), BF16),
            jax.ShapeDtypeStruct((n_seq, 2, width), F32),
        ],
        scratch_shapes=[
            pltpu.VMEM((t + 2 * SUBLANES, cw), F32),
            pltpu.VMEM((lb, t, LANES), F32),
            pltpu.VMEM((lb, t, LANES), F32),
        ],
        compiler_params=_params("parallel", "parallel"),
        name=name,
    )(z, z, conv_w, conv_b.reshape(1, width), wa, wx, ba, bx, lam, h0)


def _cmix_kernel(zu_ref, zv_ref, g_ref, ws_ref, bs_ref, o_ref, v_ref, *, n_groups):
    rows, width = zu_ref.shape
    gw = width // n_groups
    def norm_chunk(c, carry):
        r = pl.ds(pl.multiple_of(c * ROW_CHUNK, ROW_CHUNK), ROW_CHUNK)
        v_ref[r, :] = _rms(jax.nn.gelu(zv_ref[r, :].astype(F32)), g_ref[...]).astype(BF16)
        return carry

    lax.fori_loop(0, rows // ROW_CHUNK, norm_chunk, 0, unroll=NORM_UNROLL)
    for ch in range(rows // CHUNK):
        r = slice(ch * CHUNK, (ch + 1) * CHUNK)
        for g in range(n_groups):
            cols = slice(g * gw, (g + 1) * gw)
            mixed = jnp.dot(ws_ref[g], v_ref[r, cols], preferred_element_type=F32) + bs_ref[g]
            o_ref[r, cols] = (jax.nn.gelu(zu_ref[r, cols].astype(F32)) * mixed).astype(o_ref.dtype)


def _chunk_mix(z, cm_g, ws, bs, *, width, u_col, v_col, name):
    m = z.shape[0]
    n_groups = ws.shape[0]
    tr = _row_tile(m, 2 * CHUNK)
    assert u_col % width == 0 and v_col % width == 0
    return pl.pallas_call(
        functools.partial(_cmix_kernel, n_groups=n_groups),
        grid=(m // tr,),
        in_specs=[
            pl.BlockSpec((tr, width), lambda i: (i, u_col // width)),
            pl.BlockSpec((tr, width), lambda i: (i, v_col // width)),
            pl.BlockSpec((1, width), lambda i: (0, 0)),
            pl.BlockSpec(ws.shape, lambda i: (0, 0, 0)),
            pl.BlockSpec(bs.shape, lambda i: (0, 0, 0)),
        ],
        out_specs=pl.BlockSpec((tr, width), lambda i: (i, 0)),
        out_shape=jax.ShapeDtypeStruct((m, width), BF16),
        scratch_shapes=[pltpu.VMEM((tr, width), BF16)],
        compiler_params=_params("parallel"),
        name=name,
    )(z, z, cm_g.reshape(1, width), ws, bs)


def _rope_tables(n_tokens):
    rows = n_tokens // GRID_W
    pos_row = jnp.repeat(jnp.arange(rows), GRID_W).astype(F32)
    pos_col = (jnp.arange(n_tokens) % GRID_W).astype(F32)
    inv = ROPE_BASE ** (-jnp.arange(ROPE_FREQS, dtype=F32) / ROPE_FREQS)
    cr, sr = jnp.cos(pos_row[:, None] * inv), jnp.sin(pos_row[:, None] * inv)
    cc, sc = jnp.cos(pos_col[:, None] * inv), jnp.sin(pos_col[:, None] * inv)
    return (jnp.concatenate([cr, cr, cc, cc], axis=-1), jnp.concatenate([-sr, sr, -sc, sc], axis=-1))


def _layer(x, mod, p, tag, *, n_seq, t, rows_per_cond, rope=None, past_k=None, past_v=None, layer=None, h0=None):
    d = x.shape[1]
    kv_w = p["kv_w"]
    col_lx, col_lg, col_cu, col_cv, col_g, col_k = d, 2 * d, 3 * d, 4 * d, 5 * d, 8 * d
    col_v = col_k + kv_w

    tn = 2 * kv_w
    assert d % tn == 0
    q_blocks, rest_blocks = d // tn, 7 * d // tn

    def w_col_block(j):
        return jnp.where(j < q_blocks, j, jnp.where(j < q_blocks + rest_blocks, j + 1, q_blocks))

    li = p["layer"]
    z = _prenorm_matmul(x, p["g_pre_mix"], mod, p["w_in"], li, rows_per_cond=rows_per_cond,
                        shift_row=MOD_SHIFT1, scale_row=MOD_SCALE1, relu2=False, name=f"in_proj_{tag}",
                        tn=tn, w_col_block=w_col_block)
    attn = _attention(z, p["g_q"], p["g_k"], n_seq=n_seq, t=t, d=d, kv_w=kv_w, k_col=col_k, v_col=col_v,
                      rope=rope, past_k=past_k, past_v=past_v, layer=layer, name=f"attention_{tag}")
    if rope is None:
        attn_o, k_new, v_new = attn
    else:
        attn_o, k_new, v_new = attn, None, None
    lru_o, h_fin = _rglru(z, p["conv_w"], p["conv_b"], p["lru_wa"], p["lru_wx"], p["lru_ba"], p["lru_bx"],
                          p["lru_lam"], h0, n_seq=n_seq, t=t, width=d, x_col=col_lx, g_col=col_lg,
                          name=f"rglru_{tag}")
    cm_o = _chunk_mix(z, p["cm_g"], p["cm_ws"], p["cm_bs"], width=d, u_col=col_cu, v_col=col_cv,
                      name=f"chunk_mix_{tag}")
    merged = _merge((attn_o, lru_o, cm_o), (p["w_attn_out"], p["w_lru_out"], p["w_cm_out"]), li, z, col_g,
                    name=f"merge_{tag}")
    x = _matmul_norm_residual(merged, p["w_out"], li, p["g_post_mix"], mod, x, rows_per_cond=rows_per_cond,
                              gate_row=MOD_GATE1, name=f"out_proj_{tag}")
    f = _prenorm_matmul(x, p["g_pre_ff"], mod, p["w_ff1"], li, rows_per_cond=rows_per_cond,
                        shift_row=MOD_SHIFT2, scale_row=MOD_SCALE2, relu2=True, name=f"ff1_{tag}")
    x = _matmul_norm_residual(f, p["w_ff2"], li, p["g_post_ff"], mod, x, rows_per_cond=rows_per_cond,
                              gate_row=MOD_GATE2, name=f"ff2_{tag}")
    return x, k_new, v_new, h_fin


def kernel(x_prompt, x_sample, cache_k, cache_v, state_lru, c, c_ctx, w_mod, b_mod, g_pre_mix, g_post_mix, g_pre_ff, g_post_ff, w_in, g_q, g_k, w_attn_out, conv_w, conv_b, lru_wa, lru_ba, lru_wx, lru_bx, lru_lam, w_lru_out, cm_g, cm_ws, cm_bs, w_cm_out, w_out, w_ff1, w_ff2):
    batch, seq, d = x_prompt.shape
    dec_batch, dec_seq, _ = x_sample.shape
    depth = w_in.shape[0]
    in_w = w_in.shape[2]
    kv_w = (in_w - 8 * d) // 2
    n_kv = kv_w // HEAD_DIM
    past = cache_k.shape[2]
    n_groups = cm_ws.shape[1]

    n_cond = 1 + dec_batch
    cond = jnp.concatenate([c_ctx[None, :], c], axis=0)
    cond = jnp.pad(cond, ((0, -n_cond % SUBLANES), (0, 0)))
    mod = _modulation(cond, w_mod, b_mod)

    w_in_b = w_in.astype(BF16)
    w_attn_out_b, w_lru_out_b, w_cm_out_b = w_attn_out.astype(BF16), w_lru_out.astype(BF16), w_cm_out.astype(BF16)
    w_out_b, w_ff1_b, w_ff2_b = w_out.astype(BF16), w_ff1.astype(BF16), w_ff2.astype(BF16)
    lru_wa_b, lru_wx_b, cm_ws_b = (0.5 * lru_wa).astype(BF16), (0.5 * lru_wx).astype(BF16), cm_ws.astype(BF16)
    cm_bs_b = jnp.broadcast_to(cm_bs[:, :, :, None], cm_bs.shape + (d // n_groups,))

    rope = _rope_tables(dec_seq)
    past_k = cache_k.reshape(dec_batch, depth, past, kv_w)
    past_v = cache_v.reshape(dec_batch, depth, past, kv_w)
    h0_prompt = jnp.zeros((batch, 2, d), F32)

    y_p = x_prompt.reshape(batch * seq, d)
    y_s = x_sample.reshape(dec_batch * dec_seq, d)
    new_k, new_v, new_s = [], [], []
    for l in range(depth):
        p = {
            "kv_w": kv_w, "layer": l, "g_pre_mix": g_pre_mix[l], "g_post_mix": g_post_mix[l],
            "g_pre_ff": g_pre_ff[l], "g_post_ff": g_post_ff[l], "w_in": w_in_b, "g_q": g_q[l], "g_k": g_k[l],
            "w_attn_out": w_attn_out_b, "conv_w": conv_w[l], "conv_b": conv_b[l],
            "lru_wa": lru_wa_b[l], "lru_ba": lru_ba[l], "lru_wx": lru_wx_b[l], "lru_bx": lru_bx[l],
            "lru_lam": lru_lam[l], "w_lru_out": w_lru_out_b, "cm_g": cm_g[l], "cm_ws": cm_ws_b[l],
            "cm_bs": cm_bs_b[l], "w_cm_out": w_cm_out_b, "w_out": w_out_b, "w_ff1": w_ff1_b,
            "w_ff2": w_ff2_b,
        }
        y_p, k_l, v_l, s_l = _layer(y_p, mod[l, 0:1], p, f"ctx{l}", n_seq=batch, t=seq,
                                    rows_per_cond=batch * seq, h0=h0_prompt)
        new_k.append(k_l.reshape(batch, seq, n_kv, HEAD_DIM))
        new_v.append(v_l.reshape(batch, seq, n_kv, HEAD_DIM))
        new_s.append(s_l)
        y_s, _, _, _ = _layer(y_s, mod[l, 1:1 + dec_batch], p, f"lat{l}", n_seq=dec_batch, t=dec_seq,
                              rows_per_cond=dec_seq, rope=rope, past_k=past_k, past_v=past_v, layer=l,
                              h0=state_lru[:, l])
    return (y_p.reshape(batch, seq, d), y_s.reshape(dec_batch, dec_seq, d),
            jnp.stack(new_k, axis=1), jnp.stack(new_v, axis=1), jnp.stack(new_s, axis=1))
```
